```python
import jax, jax.numpy as jnp
from jax import lax
import numpy as np

D_MODEL = 2048
BATCH = 32
SEQ = 256
DEPTH = 4
DEC_BATCH = 2
DEC_SEQ = 1024
PAST_LEN = 512

GRID_W = 64
A_GROUPS = 8
A_GROUP_DIM = 128
A_WIDTH = A_GROUPS * A_GROUP_DIM
A_CHUNK = 128
B_HEADS = 8
B_HEAD_DIM = 128
B_WIDTH = B_HEADS * B_HEAD_DIM
NB_ROWS_MAX = 8
NB_COLS = 16
ROPE_THETA = 10000.0
CTX_Q_BLOCK = 128
C_HEADS = 8
C_KEY_DIM = 128
C_VAL_DIM = 128
C_FWIDTH = C_HEADS * C_KEY_DIM
C_VWIDTH = C_HEADS * C_VAL_DIM
C_CHUNK = 32
D_FF = -(-8 * D_MODEL // (3 * 256)) * 256
NORM_EPS = 1e-6
NEG_INF = -1e30
IN_SIZES = (A_WIDTH, A_WIDTH, B_WIDTH, B_WIDTH, B_WIDTH, C_FWIDTH, C_FWIDTH, C_FWIDTH,
            C_VWIDTH, C_VWIDTH, D_MODEL, D_MODEL, D_MODEL)
IN_WIDTH = sum(IN_SIZES)

kernel_name = 'hybrid_flow_prefix_trunk'


def rmsnorm(x, g=None):
    xf = x.astype(jnp.float32)
    y = xf * lax.rsqrt(jnp.mean(xf * xf, -1, keepdims=True) + NORM_EPS)
    if g is not None:
        y = y * g.astype(jnp.float32)
    return y.astype(x.dtype)


def layernorm_plain(x):
    xf = x.astype(jnp.float32)
    mu = jnp.mean(xf, -1, keepdims=True)
    var = jnp.mean(jnp.square(xf - mu), -1, keepdims=True)
    return ((xf - mu) * lax.rsqrt(var + NORM_EPS)).astype(x.dtype)


def split_cols(z):
    out, start = [], 0
    for size in IN_SIZES:
        out.append(z[..., start:start + size])
        start += size
    return out


def ada_mod(cond, w_ada_l, b_ada_l):
    m = jax.nn.silu(cond) @ w_ada_l + b_ada_l
    return jnp.split(m[:, None, :], 6, axis=-1)


def axial_rope(x):
    n, hd = x.shape[1], x.shape[-1]
    ax = hd // 2
    pos = jnp.arange(n)
    inv = ROPE_THETA ** (-jnp.arange(0, ax, 2, dtype=jnp.float32) / ax)

    def rot(xa, p):
        ang = p.astype(jnp.float32)[:, None] * inv[None, :]
        ang = jnp.concatenate([ang, ang], -1)[None, :, None, :]
        x1, x2 = jnp.split(xa, 2, axis=-1)
        return xa * jnp.cos(ang) + jnp.concatenate([-x2, x1], -1) * jnp.sin(ang)

    xf = x.astype(jnp.float32)
    out = jnp.concatenate([rot(xf[..., :ax], pos // GRID_W), rot(xf[..., ax:], pos % GRID_W)], -1)
    return out.astype(x.dtype)


def chunk_mlp(u, v, w_s, b_s):
    bsz, n, _ = u.shape
    u = jax.nn.gelu(u)
    v = layernorm_plain(jax.nn.gelu(v))
    vr = v.reshape(bsz, n // A_CHUNK, A_CHUNK, A_GROUPS, A_GROUP_DIM)
    z = jnp.einsum('gts,bnsgc->bntgc', w_s, vr) + b_s.T[:, :, None]
    return u * z.reshape(bsz, n, A_WIDTH)


def ctx_attention(q, k, v):
    bsz, n, h, hd = q.shape
    scale = hd ** -0.5
    qb = q.reshape(bsz, n // CTX_Q_BLOCK, CTX_Q_BLOCK, h, hd).swapaxes(0, 1)

    def block(q_blk):
        s = jnp.einsum('bqhd,blhd->bhql', q_blk, k).astype(jnp.float32) * scale
        p = jax.nn.softmax(s, axis=-1).astype(v.dtype)
        return jnp.einsum('bhql,blhd->bqhd', p, v)

    return lax.map(block, qb).swapaxes(0, 1).reshape(bsz, n, h, hd)


def nbhd_attention(q, k, v, k_ctx, v_ctx, rpb):
    bsz, n, h, hd = q.shape
    rows = n // GRID_W
    wr = min(NB_ROWS_MAX, rows)
    scale = hd ** -0.5
    grid = lambda t: t.reshape(bsz, rows, GRID_W, h, hd)
    qg, kg, vg = grid(q), grid(k), grid(v)
    cols = jnp.arange(GRID_W)
    c_start = jnp.clip(cols - NB_COLS // 2, 0, GRID_W - NB_COLS)
    col_ok = (cols[None, :] >= c_start[:, None]) & (cols[None, :] < c_start[:, None] + NB_COLS)
    dc_idx = jnp.clip(cols[None, :] - cols[:, None] + NB_COLS - 1, 0, 2 * NB_COLS - 2)
    rpb_cols = rpb[:, :, dc_idx]
    n_loc = wr * GRID_W

    def row_block(r):
        r_start = jnp.clip(r - wr // 2, 0, rows - wr)
        q_r = lax.dynamic_index_in_dim(qg, r, axis=1, keepdims=False)
        k_r = lax.dynamic_slice_in_dim(kg, r_start, wr, axis=1)
        v_r = lax.dynamic_slice_in_dim(vg, r_start, wr, axis=1)
        dr_idx = r_start + jnp.arange(wr) - r + NB_ROWS_MAX - 1
        bias = jnp.take(rpb_cols, dr_idx, axis=1).transpose(0, 2, 1, 3).astype(jnp.float32)
        s_loc = jnp.einsum('bqhd,bjkhd->bhqjk', q_r, k_r).astype(jnp.float32) * scale + bias[None]
        s_loc = jnp.where(col_ok[:, None, :], s_loc, NEG_INF)
        s_ctx = jnp.einsum('bqhd,blhd->bhql', q_r, k_ctx).astype(jnp.float32) * scale
        s = jnp.concatenate([s_loc.reshape(bsz, h, GRID_W, n_loc), s_ctx], -1)
        p = jax.nn.softmax(s, axis=-1).astype(v.dtype)
        p_loc = p[..., :n_loc].reshape(bsz, h, GRID_W, wr, GRID_W)
        return (jnp.einsum('bhqjk,bjkhd->bqhd', p_loc, v_r)
                + jnp.einsum('bhql,blhd->bqhd', p[..., n_loc:], v_ctx))

    o = lax.map(row_block, jnp.arange(rows))
    return o.swapaxes(0, 1).reshape(bsz, n, h, hd)


def hgrn_lower_bounds(lb_logits):
    p = jax.nn.softmax(lb_logits.astype(jnp.float32), axis=1)
    cs = jnp.cumsum(p, axis=1)
    return cs - cs[:, :1]


def hgrn_gates(z, lb):
    zf = z.astype(jnp.float32)
    log_f = jnp.logaddexp(jnp.log(lb), jnp.log1p(-lb) + jax.nn.log_sigmoid(zf))
    k = (1.0 - lb) * jax.nn.sigmoid(-zf)
    return log_f, k


def hgrn_scan(q, k, v, log_f, s0):
    bsz, n, h, _ = q.shape
    dv = v.shape[-1]
    nch = n // C_CHUNK

    def chunks(t):
        return t.astype(jnp.float32).reshape(bsz, nch, C_CHUNK, h, t.shape[-1]).swapaxes(0, 1)

    causal = jnp.tril(jnp.ones((C_CHUNK, C_CHUNK), dtype=bool))[None, :, :, None, None]

    def step(state, xs):
        qc, kc, vc, lf = xs
        b = jnp.cumsum(lf, axis=1)
        o_inter = jnp.einsum('bthd,bhde->bthe', qc * jnp.exp(b), state)
        decay = jnp.exp(jnp.where(causal, b[:, :, None] - b[:, None, :], -jnp.inf))
        attn = jnp.einsum('bthd,bshd,btshd->bhts', qc, kc, decay)
        o_intra = jnp.einsum('bhts,bshe->bthe', attn, vc)
        b_last = b[:, -1]
        state = (jnp.exp(b_last)[..., None] * state
                 + jnp.einsum('bshd,bshe->bhde', kc * jnp.exp(b_last[:, None] - b), vc))
        return state, o_inter + o_intra

    s_fin, o = lax.scan(step, s0.astype(jnp.float32), (chunks(q), chunks(k), chunks(v), chunks(log_f)))
    return o.swapaxes(0, 1).reshape(bsz, n, h, dv), s_fin


def hgrn_bidir(q, k_f, k_b, log_f_f, log_f_b, v, s0_f, s0_b):
    o_f, s_f = hgrn_scan(q, k_f, v, log_f_f, s0_f)
    rev = lambda t: jnp.flip(t, axis=1)
    o_b, s_b = hgrn_scan(rev(q), rev(k_b), rev(v), rev(log_f_b), s0_b)
    return o_f + rev(o_b), s_f, s_b


def swiglu(h, w_i, w_o):
    up, gate = jnp.split(h @ w_i, 2, axis=-1)
    return (jax.nn.silu(gate) * up) @ w_o


def trunk_layer(x, cond, p, lb, l, ctx):
    bsz, n, _ = x.shape
    sh1, sc1, g1, sh2, sc2, g2 = ada_mod(cond, p['w_ada'][l], p['b_ada'][l])
    gains = p['norm_g'][l]
    h = rmsnorm(x, gains[0]) * (1.0 + sc1) + sh1
    (a_u, a_v, b_q, b_k, b_v, c_q, c_ff, c_fb, c_i, c_g,
     gt_a, gt_b, gt_c) = split_cols(h @ p['w_in'][l])
    y_a = chunk_mlp(a_u, a_v, p['a_w'][l], p['a_b'][l])
    heads = lambda t: t.reshape(bsz, n, B_HEADS, B_HEAD_DIM)
    q, k, v = heads(b_q), heads(b_k), heads(b_v)
    if ctx is None:
        o_b = ctx_attention(q, k, v)
        s0_f = jnp.zeros((bsz, C_HEADS, C_KEY_DIM, C_VAL_DIM), jnp.float32)
        s0_b = s0_f
    else:
        k_ctx, v_ctx, s0_f, s0_b = ctx
        o_b = nbhd_attention(axial_rope(q), axial_rope(k), v, k_ctx, v_ctx, p['rpb'][l])
    y_b = o_b.reshape(bsz, n, B_WIDTH)
    chead = lambda t, d: t.reshape(bsz, n, C_HEADS, d)
    log_f_f, k_f = hgrn_gates(c_ff, lb[0, l])
    log_f_b, k_b = hgrn_gates(c_fb, lb[1, l])
    o_c, s_f, s_b = hgrn_bidir(chead(jax.nn.silu(c_q), C_KEY_DIM), chead(k_f, C_KEY_DIM), chead(k_b, C_KEY_DIM),
                               chead(log_f_f, C_KEY_DIM), chead(log_f_b, C_KEY_DIM), chead(c_i, C_VAL_DIM),
                               s0_f, s0_b)
    y_c = rmsnorm(o_c).reshape(bsz, n, C_VWIDTH).astype(x.dtype) * jax.nn.silu(c_g)
    merged = (jax.nn.sigmoid(gt_a) * (y_a @ p['w_up_a'][l])
              + jax.nn.sigmoid(gt_b) * (y_b @ p['w_up_b'][l])
              + jax.nn.sigmoid(gt_c) * (y_c @ p['w_up_c'][l]))
    x = x + g1 * rmsnorm(merged @ p['w_out'][l], gains[1])
    h2 = rmsnorm(x, gains[2]) * (1.0 + sc2) + sh2
    x = x + g2 * rmsnorm(swiglu(h2, p['w_ffn_in'][l], p['w_ffn_out'][l]), gains[3])
    if ctx is None:
        return x, (k, v, jnp.stack([s_f, s_b], axis=1))
    return x, None


def setup_inputs(seed: int = 0) -> dict:
    key = jax.random.key(seed)
    ks = jax.random.split(key, 22)
    nrm = lambda kk, shape, s: jax.random.normal(kk, shape, jnp.float32) * s
    return {
        'x_prompt': nrm(ks[0], (BATCH, SEQ, D_MODEL), 1.0),
        'x_sample': nrm(ks[1], (DEC_BATCH, DEC_SEQ, D_MODEL), 1.0),
        'cache_attn_k': nrm(ks[2], (DEC_BATCH, DEPTH, PAST_LEN, B_HEADS, B_HEAD_DIM), 1.0),
        'cache_attn_v': nrm(ks[3], (DEC_BATCH, DEPTH, PAST_LEN, B_HEADS, B_HEAD_DIM), 1.0),
        'state_hgrn': nrm(ks[4], (DEC_BATCH, DEPTH, 2, C_HEADS, C_KEY_DIM, C_VAL_DIM), 0.3),
        'c': nrm(ks[5], (DEC_BATCH, D_MODEL), 1.0),
        'c_ctx': nrm(ks[6], (D_MODEL,), 1.0),
        'w_ada': nrm(ks[7], (DEPTH, D_MODEL, 6 * D_MODEL), 0.5 * D_MODEL ** -0.5),
        'b_ada': nrm(ks[8], (DEPTH, 6 * D_MODEL), 0.02),
        'norm_g': 1.0 + nrm(ks[9], (DEPTH, 4, D_MODEL), 0.02),
        'w_in': nrm(ks[10], (DEPTH, D_MODEL, IN_WIDTH), D_MODEL ** -0.5),
        'a_spatial_w': nrm(ks[11], (DEPTH, A_GROUPS, A_CHUNK, A_CHUNK), A_CHUNK ** -0.5),
        'a_spatial_b': 1.0 + nrm(ks[12], (DEPTH, A_GROUPS, A_CHUNK), 0.1),
        'nb_rpb': nrm(ks[13], (DEPTH, B_HEADS, 2 * NB_ROWS_MAX - 1, 2 * NB_COLS - 1), 0.5),
        'hgrn_lb_logits': nrm(ks[14], (2, DEPTH, C_FWIDTH), 0.5),
        'w_up_a': nrm(ks[15], (DEPTH, A_WIDTH, D_MODEL), A_WIDTH ** -0.5),
        'w_up_b': nrm(ks[16], (DEPTH, B_WIDTH, D_MODEL), B_WIDTH ** -0.5),
        'w_up_c': nrm(ks[17], (DEPTH, C_VWIDTH, D_MODEL), C_VWIDTH ** -0.5),
        'w_out': nrm(ks[18], (DEPTH, D_MODEL, D_MODEL), D_MODEL ** -0.5),
        'w_ffn_in': nrm(ks[19], (DEPTH, D_MODEL, 2 * D_FF), D_MODEL ** -0.5),
        'w_ffn_out': nrm(ks[20], (DEPTH, D_FF, D_MODEL), D_FF ** -0.5),
    }


def reference(x_prompt, x_sample, cache_attn_k, cache_attn_v, state_hgrn, c, c_ctx,
              w_ada, b_ada, norm_g, w_in, a_spatial_w, a_spatial_b, nb_rpb, hgrn_lb_logits,
              w_up_a, w_up_b, w_up_c, w_out, w_ffn_in, w_ffn_out):
    p = {'w_ada': w_ada, 'b_ada': b_ada, 'norm_g': norm_g, 'w_in': w_in,
         'a_w': a_spatial_w, 'a_b': a_spatial_b, 'rpb': nb_rpb,
         'w_up_a': w_up_a, 'w_up_b': w_up_b, 'w_up_c': w_up_c, 'w_out': w_out,
         'w_ffn_in': w_ffn_in, 'w_ffn_out': w_ffn_out}
    lb = hgrn_lower_bounds(hgrn_lb_logits)

    y_prompt = x_prompt
    ks_, vs_, ss_ = [], [], []
    for l in range(DEPTH):
        y_prompt, (k_l, v_l, s_l) = trunk_layer(y_prompt, c_ctx[None, :], p, lb, l, None)
        ks_.append(k_l)
        vs_.append(v_l)
        ss_.append(s_l)
    new_attn_k = jnp.stack(ks_, axis=1)
    new_attn_v = jnp.stack(vs_, axis=1)
    new_state_hgrn = jnp.stack(ss_, axis=1)

    y_sample = x_sample
    for l in range(DEPTH):
        ctx = (cache_attn_k[:, l], cache_attn_v[:, l], state_hgrn[:, l, 0], state_hgrn[:, l, 1])
        y_sample, _ = trunk_layer(y_sample, c, p, lb, l, ctx)

    return (y_prompt, y_sample, new_attn_k, new_attn_v, new_state_hgrn)
```

```python
import functools
import math

import jax
import jax.numpy as jnp
from jax import lax
from jax.experimental import pallas as pl
from jax.experimental.pallas import tpu as pltpu

F32 = jnp.float32
BF16 = jnp.bfloat16

D_MODEL = 2048
BATCH = 32
SEQ = 256
DEPTH = 4
DEC_BATCH = 2
DEC_SEQ = 1024
PAST_LEN = 512
GRID_W = 64
HEADS = 8
HEAD_DIM = 128
WIDTH = HEADS * HEAD_DIM
A_CHUNK = 128
NB_ROWS_MAX = 8
NB_COLS = 16
ROPE_THETA = 10000.0
D_FF = -(-8 * D_MODEL // (3 * 256)) * 256
NORM_EPS = 1e-6
NEG_INF = -1e30
IN_WIDTH = 10 * WIDTH + 3 * D_MODEL

N_CTX = BATCH * SEQ
N_LAT = DEC_BATCH * DEC_SEQ
N_TOK = N_CTX + N_LAT
N_COND = 1 + DEC_BATCH
COND_ROWS = 8

COL_AU, COL_AV, COL_BQ, COL_BK, COL_BV, COL_CQ, COL_CFF, COL_CFB, COL_CI, COL_CG = range(10)
COL_GATES = 10 * WIDTH

HGRN_CHUNK = 128
HGRN_BAND = 8
HGRN_LEVELS = (16, 32, 64, 128)

VMEM_LIMIT = 48 * 1024 * 1024


def _cparams(n_axes):
    return pltpu.CompilerParams(dimension_semantics=("arbitrary",) * n_axes,
                                vmem_limit_bytes=VMEM_LIMIT)


def _cond_row(i, tm):
    n_ctx_tiles = N_CTX // tm
    return jnp.where(i < n_ctx_tiles, 0, 1 + (i - n_ctx_tiles) // (DEC_SEQ // tm))


def _silu(x):
    return x * jax.nn.sigmoid(x)


def _gelu_tanh(x):
    c = math.sqrt(2.0 / math.pi)
    return x * (0.5 * (1.0 + jnp.tanh(c * (x + 0.044715 * (x * x * x)))))


def _rms(x):
    return x * lax.rsqrt(jnp.mean(x * x, -1, keepdims=True) + NORM_EPS)


def _ada_kernel(c_ref, w_ref, b_ref, o_ref):
    s = _silu(c_ref[...]).astype(BF16)
    o_ref[...] = jnp.dot(s, w_ref[...].astype(BF16), preferred_element_type=F32) + b_ref[...]


def ada_all_layers(cond, w_ada, b_ada):
    tn = 1024
    n_out = 6 * D_MODEL
    return pl.pallas_call(
        _ada_kernel,
        grid=(DEPTH, n_out // tn),
        in_specs=[pl.BlockSpec((COND_ROWS, D_MODEL), lambda l, j: (0, 0)),
                  pl.BlockSpec((None, D_MODEL, tn), lambda l, j: (l, 0, j)),
                  pl.BlockSpec((None, 1, tn), lambda l, j: (l, 0, j))],
        out_specs=pl.BlockSpec((None, COND_ROWS, tn), lambda l, j: (l, 0, j)),
        out_shape=jax.ShapeDtypeStruct((DEPTH, COND_ROWS, n_out), F32),
        compiler_params=_cparams(2),
        name="ada",
    )(cond, w_ada, b_ada.reshape(DEPTH, 1, n_out))


def _prenorm_kernel(x_ref, g_ref, sc_ref, sh_ref, h_ref):
    y = _rms(x_ref[...]) * g_ref[...]
    h_ref[...] = (y * (1.0 + sc_ref[...]) + sh_ref[...]).astype(BF16)


def prenorm(x, gain, sc, sh):
    tm = 512
    row = lambda i: (_cond_row(i, tm), 0, 0)
    return pl.pallas_call(
        _prenorm_kernel,
        grid=(N_TOK // tm,),
        in_specs=[pl.BlockSpec((tm, D_MODEL), lambda i: (i, 0)),
                  pl.BlockSpec((1, D_MODEL), lambda i: (0, 0)),
                  pl.BlockSpec((None, 1, D_MODEL), row),
                  pl.BlockSpec((None, 1, D_MODEL), row)],
        out_specs=pl.BlockSpec((tm, D_MODEL), lambda i: (i, 0)),
        out_shape=jax.ShapeDtypeStruct((N_TOK, D_MODEL), BF16),
        compiler_params=_cparams(1),
        name="prenorm",
    )(x, gain, sc, sh)


def _resnorm_kernel(t_ref, x_ref, g_ref, gate_ref, *rest, with_next):
    x_new = x_ref[...] + gate_ref[...] * (_rms(t_ref[...]) * g_ref[...])
    if with_next:
        g2_ref, sc_ref, sh_ref, xo_ref, h_ref = rest
        xo_ref[...] = x_new
        h_ref[...] = (_rms(x_new) * g2_ref[...] * (1.0 + sc_ref[...]) + sh_ref[...]).astype(BF16)
    else:
        (xo_ref,) = rest
        xo_ref[...] = x_new


def resnorm(t, x, gain, gate, nxt=None):
    tm = 512
    row = lambda i: (_cond_row(i, tm), 0, 0)
    tile = pl.BlockSpec((tm, D_MODEL), lambda i: (i, 0))
    vec = pl.BlockSpec((1, D_MODEL), lambda i: (0, 0))
    mod = pl.BlockSpec((None, 1, D_MODEL), row)
    in_specs = [tile, tile, vec, mod]
    args = [t, x, gain, gate]
    out_specs = [tile]
    out_shape = [jax.ShapeDtypeStruct((N_TOK, D_MODEL), F32)]
    if nxt is not None:
        in_specs += [vec, mod, mod]
        args += list(nxt)
        out_specs.append(tile)
        out_shape.append(jax.ShapeDtypeStruct((N_TOK, D_MODEL), BF16))
    out = pl.pallas_call(
        functools.partial(_resnorm_kernel, with_next=nxt is not None),
        grid=(N_TOK // tm,),
        in_specs=in_specs, out_specs=out_specs, out_shape=out_shape,
        compiler_params=_cparams(1),
        name="resnorm",
    )(*args)
    return out if nxt is not None else (out[0], None)


def _mm_kernel(a_ref, w_ref, o_ref, wb_ref):
    @pl.when(pl.program_id(1) == 0)
    def _():
        wb_ref[...] = w_ref[...].astype(BF16)

    o_ref[...] = jnp.dot(a_ref[...], wb_ref[...], preferred_element_type=F32)


def matmul_layer(a, w, l, tm, tn):
    m, k = a.shape
    n = w.shape[2]
    return pl.pallas_call(
        _mm_kernel,
        grid=(n // tn, m // tm),
        in_specs=[pl.BlockSpec((tm, k), lambda j, i: (i, 0)),
                  pl.BlockSpec((None, k, tn), lambda j, i: (l, 0, j))],
        out_specs=pl.BlockSpec((tm, tn), lambda j, i: (i, j)),
        out_shape=jax.ShapeDtypeStruct((m, n), F32),
        scratch_shapes=[pltpu.VMEM((k, tn), BF16)],
        compiler_params=_cparams(2),
        name="matmul",
    )(a, w)


def _merge_kernel(ya_ref, yb_ref, yc_ref, wa_ref, wb_ref, wc_ref, ga_ref, gb_ref, gc_ref,
                  o_ref, sa_ref, sb_ref, sc_ref):
    @pl.when(pl.program_id(1) == 0)
    def _():
        sa_ref[...] = wa_ref[...].astype(BF16)
        sb_ref[...] = wb_ref[...].astype(BF16)
        sc_ref[...] = wc_ref[...].astype(BF16)

    acc = jax.nn.sigmoid(ga_ref[...]) * jnp.dot(ya_ref[...], sa_ref[...], preferred_element_type=F32)
    acc += jax.nn.sigmoid(gb_ref[...]) * jnp.dot(yb_ref[...], sb_ref[...], preferred_element_type=F32)
    acc += jax.nn.sigmoid(gc_ref[...]) * jnp.dot(yc_ref[...], sc_ref[...], preferred_element_type=F32)
    o_ref[...] = acc.astype(BF16)


def merge_branches(y_a, y_b, y_c, z, w_up_a, w_up_b, w_up_c, l):
    tm, tn = 1024, 512
    gate0 = COL_GATES // tn
    per_gate = D_MODEL // tn
    y_spec = pl.BlockSpec((tm, WIDTH), lambda j, i: (i, 0))
    w_spec = pl.BlockSpec((None, WIDTH, tn), lambda j, i: (l, 0, j))
    g_spec = lambda b: pl.BlockSpec((tm, tn), lambda j, i: (i, gate0 + b * per_gate + j))
    return pl.pallas_call(
        _merge_kernel,
        grid=(D_MODEL // tn, N_TOK // tm),
        in_specs=[y_spec, y_spec, y_spec, w_spec, w_spec, w_spec, g_spec(0), g_spec(1), g_spec(2)],
        out_specs=pl.BlockSpec((tm, tn), lambda j, i: (i, j)),
        out_shape=jax.ShapeDtypeStruct((N_TOK, D_MODEL), BF16),
        scratch_shapes=[pltpu.VMEM((WIDTH, tn), BF16)] * 3,
        compiler_params=_cparams(2),
        name="merge",
    )(y_a, y_b, y_c, w_up_a, w_up_b, w_up_c, z, z, z)


def _swiglu_kernel(h_ref, wu_ref, wg_ref, o_ref, su_ref, sg_ref):
    @pl.when(pl.program_id(1) == 0)
    def _():
        su_ref[...] = wu_ref[...].astype(BF16)
        sg_ref[...] = wg_ref[...].astype(BF16)

    h = h_ref[...]
    up = jnp.dot(h, su_ref[...], preferred_element_type=F32)
    gate = jnp.dot(h, sg_ref[...], preferred_element_type=F32)
    o_ref[...] = (_silu(gate) * up).astype(BF16)


def swiglu_in(h, w_ffn_in, l):
    tm, tn = 1024, 512
    n_up = D_FF // tn
    return pl.pallas_call(
        _swiglu_kernel,
        grid=(n_up, N_TOK // tm),
        in_specs=[pl.BlockSpec((tm, D_MODEL), lambda j, i: (i, 0)),
                  pl.BlockSpec((None, D_MODEL, tn), lambda j, i: (l, 0, j)),
                  pl.BlockSpec((None, D_MODEL, tn), lambda j, i: (l, 0, n_up + j))],
        out_specs=pl.BlockSpec((tm, tn), lambda j, i: (i, j)),
        out_shape=jax.ShapeDtypeStruct((N_TOK, D_FF), BF16),
        scratch_shapes=[pltpu.VMEM((D_MODEL, tn), BF16)] * 2,
        compiler_params=_cparams(2),
        name="swiglu_in",
    )(h, w_ffn_in, w_ffn_in)


def _mix_a_kernel(u_ref, v_ref, w_ref, bt_ref, o_ref, *, rows):
    w = w_ref[...].astype(BF16)
    for c in range(rows // A_CHUNK):
        rs = slice(c * A_CHUNK, (c + 1) * A_CHUNK)
        u = _gelu_tanh(u_ref[rs, :])
        v = _gelu_tanh(v_ref[rs, :])
        mu = jnp.mean(v, -1, keepdims=True)
        vc = v - mu
        vn = (vc * lax.rsqrt(jnp.mean(vc * vc, -1, keepdims=True) + NORM_EPS)).astype(BF16)
        for g in range(HEADS):
            cs = slice(g * HEAD_DIM, (g + 1) * HEAD_DIM)
            zg = jnp.dot(w[g], vn[:, cs], preferred_element_type=F32) + bt_ref[:, g:g + 1]
            o_ref[rs, cs] = (u[:, cs] * zg).astype(BF16)


def mix_a(z, a_w_l, a_b_l):
    rows = 512
    return pl.pallas_call(
        functools.partial(_mix_a_kernel, rows=rows),
        grid=(N_TOK // rows,),
        in_specs=[pl.BlockSpec((rows, WIDTH), lambda i: (i, COL_AU)),
                  pl.BlockSpec((rows, WIDTH), lambda i: (i, COL_AV)),
                  pl.BlockSpec((HEADS, A_CHUNK, A_CHUNK), lambda i: (0, 0, 0)),
                  pl.BlockSpec((A_CHUNK, HEADS), lambda i: (0, 0))],
        out_specs=pl.BlockSpec((rows, WIDTH), lambda i: (i, 0)),
        out_shape=jax.ShapeDtypeStruct((N_TOK, WIDTH), BF16),
        compiler_params=_cparams(1),
        name="mix_a",
    )(z, z, a_w_l, a_b_l.T)


def _softmax_pv(scores, values):
    m = scores[0].max(-1, keepdims=True)
    for s in scores[1:]:
        m = jnp.maximum(m, s.max(-1, keepdims=True))
    den = 0.0
    acc = 0.0
    for s, v in zip(scores, values):
        p = jnp.exp(s - m)
        den = den + p.sum(-1, keepdims=True)
        acc = acc + jnp.dot(p.astype(BF16), v, preferred_element_type=F32)
    return acc / den


def _qkt(q, k):
    return lax.dot_general(q, k, (((1,), (1,)), ((), ())), preferred_element_type=F32)


def _ctx_attn_kernel(q_ref, k_ref, v_ref, o_ref):
    scale = HEAD_DIM ** -0.5
    for h in range(HEADS):
        cs = slice(h * HEAD_DIM, (h + 1) * HEAD_DIM)
        q = q_ref[:, cs].astype(BF16)
        k = k_ref[:, cs].astype(BF16)
        v = v_ref[:, cs].astype(BF16)
        o_ref[:, cs] = _softmax_pv([_qkt(q, k) * scale], [v]).astype(BF16)


def ctx_attention(z):
    spec = lambda col: pl.BlockSpec((SEQ, WIDTH), lambda b: (b, col))
    return pl.pallas_call(
        _ctx_attn_kernel,
        grid=(BATCH,),
        in_specs=[spec(COL_BQ), spec(COL_BK), spec(COL_BV)],
        out_specs=pl.BlockSpec((SEQ, WIDTH), lambda b: (b, 0)),
        out_shape=jax.ShapeDtypeStruct((N_CTX, WIDTH), BF16),
        compiler_params=_cparams(1),
        name="ctx_attn",
    )(z, z, z)


def _rope(x, cos, sin_signed, first_quarter):
    partner = jnp.where(first_quarter, pltpu.roll(x, HEAD_DIM - 32, 1), pltpu.roll(x, 32, 1))
    return x * cos + partner * sin_signed


def _nbhd_attn_kernel(q_ref, k_ref, v_ref, kc_ref, vc_ref, bias_ref, cos_ref, sin_ref, o_ref, *, q_tile):
    scale = HEAD_DIM ** -0.5
    lane = lax.broadcasted_iota(jnp.int32, (1, HEAD_DIM), 1)
    first_quarter = (lane % 64) < 32
    k = _rope(k_ref[...], cos_ref[...], sin_ref[...], first_quarter).astype(BF16)
    v = v_ref[...].astype(BF16)
    kc = kc_ref[...].astype(BF16)
    vc = vc_ref[...].astype(BF16)
    for t in range(DEC_SEQ // q_tile):
        rs = slice(t * q_tile, (t + 1) * q_tile)
        q = _rope(q_ref[rs, :], cos_ref[rs, :], sin_ref[rs, :], first_quarter).astype(BF16)
        s_loc = _qkt(q, k) * scale + bias_ref[rs, :]
        s_ctx = _qkt(q, kc) * scale
        o_ref[rs, :] = _softmax_pv([s_loc, s_ctx], [v, vc]).astype(BF16)


def nbhd_attention(z, cache_k, cache_v, bias, cos, sin_signed, l):
    lat0 = N_CTX // DEC_SEQ
    col = lambda c: (lambda h, b: (lat0 + b, c * HEADS + h))
    tok = lambda c: pl.BlockSpec((DEC_SEQ, HEAD_DIM), col(c))
    cache = pl.BlockSpec((None, None, PAST_LEN, HEAD_DIM), lambda h, b: (b, l, 0, h))
    table = pl.BlockSpec((DEC_SEQ, HEAD_DIM), lambda h, b: (0, 0))
    return pl.pallas_call(
        functools.partial(_nbhd_attn_kernel, q_tile=256),
        grid=(HEADS, DEC_BATCH),
        in_specs=[tok(COL_BQ), tok(COL_BK), tok(COL_BV), cache, cache,
                  pl.BlockSpec((None, DEC_SEQ, DEC_SEQ), lambda h, b: (h, 0, 0)), table, table],
        out_specs=pl.BlockSpec((DEC_SEQ, HEAD_DIM), lambda h, b: (b, h)),
        out_shape=jax.ShapeDtypeStruct((N_LAT, WIDTH), BF16),
        compiler_params=_cparams(2),
        name="nbhd_attn",
    )(z, z, z, cache_k, cache_v, bias, cos, sin_signed)


def nbhd_bias_table(rpb_l):
    rows = DEC_SEQ // GRID_W
    wr = min(NB_ROWS_MAX, rows)
    cols = jnp.arange(GRID_W)
    c_start = jnp.clip(cols - NB_COLS // 2, 0, GRID_W - NB_COLS)
    col_ok = (cols[None, :] >= c_start[:, None]) & (cols[None, :] < c_start[:, None] + NB_COLS)
    dc_idx = jnp.clip(cols[None, :] - cols[:, None] + NB_COLS - 1, 0, 2 * NB_COLS - 2)
    r = jnp.arange(rows)
    r_start = jnp.clip(r - wr // 2, 0, rows - wr)
    row_ok = (r[None, :] >= r_start[:, None]) & (r[None, :] < r_start[:, None] + wr)
    dr_idx = jnp.clip(r[None, :] - r[:, None] + NB_ROWS_MAX - 1, 0, 2 * NB_ROWS_MAX - 2)
    t = rpb_l[:, dr_idx][:, :, :, dc_idx]
    ok = row_ok[:, :, None, None] & col_ok[None, None, :, :]
    t = jnp.where(ok[None], t, NEG_INF).transpose(0, 1, 3, 2, 4)
    return t.reshape(HEADS, DEC_SEQ, DEC_SEQ).astype(F32)


def rope_tables():
    ax = HEAD_DIM // 2
    pos = jnp.arange(DEC_SEQ)
    inv = ROPE_THETA ** (-jnp.arange(0, ax, 2, dtype=F32) / ax)

    def ang(p):
        a = p.astype(F32)[:, None] * inv[None, :]
        return jnp.concatenate([a, a], -1)

    a = jnp.concatenate([ang(pos // GRID_W), ang(pos % GRID_W)], -1)
    sign = jnp.where((jnp.arange(HEAD_DIM) % ax) < ax // 2, -1.0, 1.0).astype(F32)
    return jnp.cos(a), jnp.sin(a) * sign[None, :]


def _split3(x):
    hi = x.astype(BF16)
    r = x - hi.astype(F32)
    mid = r.astype(BF16)
    lo = (r - mid.astype(F32)).astype(BF16)
    return hi, mid, lo


def _hgrn_chunk(qs, zf, v, loglb, log1mlb, onemlb, st, tri, lev, b_scr, rev):
    c = HGRN_CHUNK
    ls = jnp.minimum(zf, 0.0) - jnp.log1p(jnp.exp(-jnp.abs(zf)))
    x2 = log1mlb + ls
    lf = jnp.maximum(loglb, x2) + jnp.log1p(jnp.exp(-jnp.abs(loglb - x2)))
    kk = onemlb * jax.nn.sigmoid(-zf)

    b = sum(jnp.dot(tri, p, preferred_element_type=F32) for p in _split3(lf))
    b_scr[...] = b
    end = 0 if rev else c - 1
    b_end = b_scr[end:end + 1, :]

    o = _qkt((qs * jnp.exp(b)).astype(BF16), st.astype(BF16))
    k_dec = (kk * jnp.exp(b_end - b)).astype(BF16)
    vb = v.astype(BF16)
    st_new = st * jnp.exp(b_end) + lax.dot_general(vb, k_dec, (((0,), (0,)), ((), ())),
                                                   preferred_element_type=F32)

    pos8 = lax.broadcasted_iota(jnp.int32, (c, 1), 0) % HGRN_BAND
    o += (qs * kk).sum(-1, keepdims=True) * v
    for delta in range(1, HGRN_BAND):
        shift = (c - delta) if rev else delta
        ok = (pos8 + delta < HGRN_BAND) if rev else (pos8 >= delta)
        decay = jnp.exp(jnp.minimum(b - pltpu.roll(b, shift, 0), 0.0))
        a = jnp.where(ok, qs * pltpu.roll(kk, shift, 0) * decay, 0.0).sum(-1, keepdims=True)
        o += a * pltpu.roll(v, shift, 0)

    attn = jnp.zeros((c, c), F32)
    for m in HGRN_LEVELS:
        seam = m // 2 if rev else m // 2 - 1
        g = jnp.concatenate([jnp.broadcast_to(b_scr[j * m + seam:j * m + seam + 1, :], (m, HEAD_DIM))
                             for j in range(c // m)], 0)
        e = jnp.exp(-jnp.abs(b - g))
        attn = jnp.where(lev == m, _qkt((qs * e).astype(BF16), (kk * e).astype(BF16)), attn)
    o += jnp.dot(attn.astype(BF16), vb, preferred_element_type=F32)
    return o, st_new


def _hgrn_kernel(q_ref, ff_ref, fb_ref, v_ref, g_ref, lbp_ref, tri_ref, lev_ref, *rest, n, zero_init):
    if zero_init:
        y_ref, s_ref, o_scr, b_scr = rest
    else:
        s0_ref, y_ref, s_ref, o_scr, b_scr = rest
    c = HGRN_CHUNK
    nc = n // c

    def direction(d, rev, finish):
        z_ref = fb_ref if rev else ff_ref
        loglb, log1mlb, onemlb = (lbp_ref[3 * d + r:3 * d + r + 1, :] for r in range(3))
        tri = tri_ref[d]
        lev = lev_ref[d]
        st0 = jnp.zeros((HEAD_DIM, HEAD_DIM), F32) if zero_init else s0_ref[d].T

        def body(i, st):
            ci = (nc - 1 - i) if rev else i
            rs = pl.ds(pl.multiple_of(ci * c, c), c)
            o, st = _hgrn_chunk(_silu(q_ref[rs, :]), z_ref[rs, :], v_ref[rs, :],
                                loglb, log1mlb, onemlb, st, tri, lev, b_scr, rev)
            if finish:
                y_ref[rs, :] = (_rms(o_scr[rs, :] + o) * _silu(g_ref[rs, :])).astype(BF16)
            else:
                o_scr[rs, :] = o
            return st

        s_ref[d] = lax.fori_loop(0, nc, body, st0).T

    direction(0, False, False)
    direction(1, True, True)


def _hgrn_constants():
    c = HGRN_CHUNK
    t = jnp.arange(c)[:, None]
    s = jnp.arange(c)[None, :]
    tri_f = (s <= t)
    lev_f = jnp.zeros((c, c), jnp.int32)
    for m in HGRN_LEVELS:
        h = m // 2
        lev_f = jnp.where((t // h == s // h + 1) & ((t // h) % 2 == 1), m, lev_f)
    tri = jnp.stack([tri_f, tri_f.T]).astype(BF16)
    lev = jnp.stack([lev_f, lev_f.T])
    return tri, lev


def hgrn(z, lbp, n, row0, s0=None, l=0):
    n_tok = (N_CTX if s0 is None else N_LAT)
    bsz = n_tok // n
    blk0 = row0 // n
    tri, lev = _hgrn_constants()
    tok = lambda c: pl.BlockSpec((n, HEAD_DIM), lambda b, h: (blk0 + b, c * HEADS + h))
    in_specs = [tok(COL_CQ), tok(COL_CFF), tok(COL_CFB), tok(COL_CI), tok(COL_CG),
                pl.BlockSpec((8, HEAD_DIM), lambda b, h: (0, h)),
                pl.BlockSpec((2, HGRN_CHUNK, HGRN_CHUNK), lambda b, h: (0, 0, 0)),
                pl.BlockSpec((2, HGRN_CHUNK, HGRN_CHUNK), lambda b, h: (0, 0, 0))]
    args = [z, z, z, z, z, lbp, tri, lev]
    if s0 is not None:
        in_specs.append(pl.BlockSpec((None, None, 2, None, HEAD_DIM, HEAD_DIM),
                                     lambda b, h: (b, l, 0, h, 0, 0)))
        args.append(s0)
    return pl.pallas_call(
        functools.partial(_hgrn_kernel, n=n, zero_init=s0 is None),
        grid=(bsz, HEADS),
        in_specs=in_specs,
        out_specs=[pl.BlockSpec((n, HEAD_DIM), lambda b, h: (b, h)),
                   pl.BlockSpec((None, 2, None, HEAD_DIM, HEAD_DIM), lambda b, h: (b, 0, h, 0, 0))],
        out_shape=[jax.ShapeDtypeStruct((n_tok, WIDTH), BF16),
                   jax.ShapeDtypeStruct((bsz, 2, HEADS, HEAD_DIM, HEAD_DIM), F32)],
        scratch_shapes=[pltpu.VMEM((n, HEAD_DIM), F32), pltpu.VMEM((HGRN_CHUNK, HEAD_DIM), F32)],
        compiler_params=_cparams(2),
        name="hgrn",
    )(*args)


def hgrn_gate_params(lb_logits):
    p = jax.nn.softmax(lb_logits.astype(F32), axis=1)
    cs = jnp.cumsum(p, axis=1)
    lb = cs - cs[:, :1]
    rows = [jnp.log(lb[0]), jnp.log1p(-lb[0]), 1.0 - lb[0],
            jnp.log(lb[1]), jnp.log1p(-lb[1]), 1.0 - lb[1],
            jnp.zeros_like(lb[0]), jnp.zeros_like(lb[0])]
    return jnp.stack(rows, axis=1)


def kernel(x_prompt, x_sample, cache_attn_k, cache_attn_v, state_hgrn, c, c_ctx, w_ada, b_ada, norm_g, w_in,
           a_spatial_w, a_spatial_b, nb_rpb, hgrn_lb_logits, w_up_a, w_up_b, w_up_c, w_out, w_ffn_in, w_ffn_out):
    x = jnp.concatenate([x_prompt.reshape(N_CTX, D_MODEL), x_sample.reshape(N_LAT, D_MODEL)], 0)
    cond = jnp.zeros((COND_ROWS, D_MODEL), F32).at[0].set(c_ctx).at[1:N_COND].set(c)
    mod = ada_all_layers(cond, w_ada, b_ada)
    mod = mod.reshape(DEPTH, COND_ROWS, 6, 1, D_MODEL).transpose(0, 2, 1, 3, 4)
    lbp = hgrn_gate_params(hgrn_lb_logits)
    cos, sin_signed = rope_tables()
    cache_k = cache_attn_k.reshape(DEC_BATCH, DEPTH, PAST_LEN, WIDTH)
    cache_v = cache_attn_v.reshape(DEC_BATCH, DEPTH, PAST_LEN, WIDTH)
    gain = lambda l, i: norm_g[l, i][None, :]

    ks, vs, ss = [], [], []
    h = prenorm(x, gain(0, 0), mod[0, 1], mod[0, 0])
    for l in range(DEPTH):
        sh1, sc1, g1, sh2, sc2, g2 = (mod[l, i] for i in range(6))
        z = matmul_layer(h, w_in, l, tm=1024, tn=1024)
        y_a = mix_a(z, a_spatial_w[l], a_spatial_b[l])
        y_b = jnp.concatenate([
            ctx_attention(z),
            nbhd_attention(z, cache_k, cache_v, nbhd_bias_table(nb_rpb[l]), cos, sin_signed, l)], 0)
        yc_ctx, s_ctx = hgrn(z, lbp[l], SEQ, 0)
        yc_lat, _ = hgrn(z, lbp[l], DEC_SEQ, N_CTX, s0=state_hgrn, l=l)
        y_c = jnp.concatenate([yc_ctx, yc_lat], 0)
        merged = merge_branches(y_a, y_b, y_c, z, w_up_a, w_up_b, w_up_c, l)
        t = matmul_layer(merged, w_out, l, tm=1024, tn=1024)
        x, h2 = resnorm(t, x, gain(l, 1), g1, nxt=(gain(l, 2), sc2, sh2))
        act = swiglu_in(h2, w_ffn_in, l)
        t = matmul_layer(act, w_ffn_out, l, tm=512, tn=512)
        nxt = None if l == DEPTH - 1 else (gain(l + 1, 0), mod[l + 1, 1], mod[l + 1, 0])
        x, h = resnorm(t, x, gain(l, 3), g2, nxt=nxt)

        kv = z[:N_CTX, COL_BK * WIDTH:(COL_BV + 1) * WIDTH].reshape(BATCH, SEQ, 2, HEADS, HEAD_DIM)
        ks.append(kv[:, :, 0])
        vs.append(kv[:, :, 1])
        ss.append(s_ctx)

    y_prompt = x[:N_CTX].reshape(BATCH, SEQ, D_MODEL)
    y_sample = x[N_CTX:].reshape(DEC_BATCH, DEC_SEQ, D_MODEL)
    return (y_prompt, y_sample, jnp.stack(ks, axis=1), jnp.stack(vs, axis=1), jnp.stack(ss, axis=1))
```

```python
import functools
import math

import jax
import jax.numpy as jnp
from jax import lax
from jax.experimental import pallas as pl
from jax.experimental.pallas import tpu as pltpu

F32 = jnp.float32
BF16 = jnp.bfloat16

D_MODEL = 2048
BATCH = 32
SEQ = 256
DEPTH = 4
DEC_BATCH = 2
DEC_SEQ = 1024
PAST_LEN = 512
GRID_W = 64
HEADS = 8
HEAD_DIM = 128
WIDTH = HEADS * HEAD_DIM
A_CHUNK = 128
NB_ROWS_MAX = 8
NB_COLS = 16
ROPE_THETA = 10000.0
D_FF = -(-8 * D_MODEL // (3 * 256)) * 256
NORM_EPS = 1e-6
NEG_INF = -1e30
IN_WIDTH = 10 * WIDTH + 3 * D_MODEL

N_CTX = BATCH * SEQ
N_LAT = DEC_BATCH * DEC_SEQ
N_TOK = N_CTX + N_LAT
N_COND = 1 + DEC_BATCH
COND_ROWS = 8

COL_AU, COL_AV, COL_BQ, COL_BK, COL_BV, COL_CQ, COL_CFF, COL_CFB, COL_CI, COL_CG = range(10)
COL_GATES = 10 * WIDTH

HGRN_CHUNK = 128
HGRN_BAND = 8
HGRN_LEVELS = (16, 32, 64, 128)

VMEM_LIMIT = 48 * 1024 * 1024


def _cparams(n_axes):
    return pltpu.CompilerParams(dimension_semantics=("arbitrary",) * n_axes,
                                vmem_limit_bytes=VMEM_LIMIT)


def _cond_row(i, tm):
    n_ctx_tiles = N_CTX // tm
    return jnp.where(i < n_ctx_tiles, 0, 1 + (i - n_ctx_tiles) // (DEC_SEQ // tm))


def _silu(x):
    return x * jax.nn.sigmoid(x)


def _sigmoid(x):
    return 0.5 * (1.0 + jnp.tanh(0.5 * x))


def _silu_tanh(x):
    return x * _sigmoid(x)


def _gelu_tanh(x):
    c = math.sqrt(2.0 / math.pi)
    return x * (0.5 * (1.0 + jnp.tanh(c * (x + 0.044715 * (x * x * x)))))


def _rms(x):
    return x * lax.rsqrt(jnp.mean(x * x, -1, keepdims=True) + NORM_EPS)


def _ada_kernel(c_ref, w_ref, b_ref, o_ref):
    s = _silu(c_ref[...]).astype(BF16)
    o_ref[...] = jnp.dot(s, w_ref[...].astype(BF16), preferred_element_type=F32) + b_ref[...]


def ada_all_layers(cond, w_ada, b_ada):
    tn = 1024
    n_out = 6 * D_MODEL
    return pl.pallas_call(
        _ada_kernel,
        grid=(DEPTH, n_out // tn),
        in_specs=[pl.BlockSpec((COND_ROWS, D_MODEL), lambda l, j: (0, 0)),
                  pl.BlockSpec((None, D_MODEL, tn), lambda l, j: (l, 0, j)),
                  pl.BlockSpec((None, 1, tn), lambda l, j: (l, 0, j))],
        out_specs=pl.BlockSpec((None, COND_ROWS, tn), lambda l, j: (l, 0, j)),
        out_shape=jax.ShapeDtypeStruct((DEPTH, COND_ROWS, n_out), F32),
        compiler_params=_cparams(2),
        name="ada",
    )(cond, w_ada, b_ada.reshape(DEPTH, 1, n_out))


def _prenorm_kernel(x_ref, g_ref, sc_ref, sh_ref, h_ref):
    y = _rms(x_ref[...]) * g_ref[...]
    h_ref[...] = (y * (1.0 + sc_ref[...]) + sh_ref[...]).astype(BF16)


def prenorm(x, gain, sc, sh):
    tm = 512
    row = lambda i: (_cond_row(i, tm), 0, 0)
    return pl.pallas_call(
        _prenorm_kernel,
        grid=(N_TOK // tm,),
        in_specs=[pl.BlockSpec((tm, D_MODEL), lambda i: (i, 0)),
                  pl.BlockSpec((1, D_MODEL), lambda i: (0, 0)),
                  pl.BlockSpec((None, 1, D_MODEL), row),
                  pl.BlockSpec((None, 1, D_MODEL), row)],
        out_specs=pl.BlockSpec((tm, D_MODEL), lambda i: (i, 0)),
        out_shape=jax.ShapeDtypeStruct((N_TOK, D_MODEL), BF16),
        compiler_params=_cparams(1),
        name="prenorm",
    )(x, gain, sc, sh)


def _resnorm_kernel(t_ref, x_ref, g_ref, gate_ref, *rest, with_next):
    x_new = x_ref[...] + gate_ref[...] * (_rms(t_ref[...]) * g_ref[...])
    if with_next:
        g2_ref, sc_ref, sh_ref, xo_ref, h_ref = rest
        xo_ref[...] = x_new
        h_ref[...] = (_rms(x_new) * g2_ref[...] * (1.0 + sc_ref[...]) + sh_ref[...]).astype(BF16)
    else:
        (xo_ref,) = rest
        xo_ref[...] = x_new


def resnorm(t, x, gain, gate, nxt=None):
    tm = 512
    row = lambda i: (_cond_row(i, tm), 0, 0)
    tile = pl.BlockSpec((tm, D_MODEL), lambda i: (i, 0))
    vec = pl.BlockSpec((1, D_MODEL), lambda i: (0, 0))
    mod = pl.BlockSpec((None, 1, D_MODEL), row)
    in_specs = [tile, tile, vec, mod]
    args = [t, x, gain, gate]
    out_specs = [tile]
    out_shape = [jax.ShapeDtypeStruct((N_TOK, D_MODEL), F32)]
    if nxt is not None:
        in_specs += [vec, mod, mod]
        args += list(nxt)
        out_specs.append(tile)
        out_shape.append(jax.ShapeDtypeStruct((N_TOK, D_MODEL), BF16))
    out = pl.pallas_call(
        functools.partial(_resnorm_kernel, with_next=nxt is not None),
        grid=(N_TOK // tm,),
        in_specs=in_specs, out_specs=out_specs, out_shape=out_shape,
        compiler_params=_cparams(1),
        name="resnorm",
    )(*args)
    return out if nxt is not None else (out[0], None)


def _mm_kernel(a_ref, w_ref, o_ref, wb_ref):
    @pl.when(pl.program_id(1) == 0)
    def _():
        wb_ref[...] = w_ref[...].astype(BF16)

    o_ref[...] = jnp.dot(a_ref[...], wb_ref[...], preferred_element_type=F32)


def matmul_layer(a, w, l, tm, tn):
    m, k = a.shape
    n = w.shape[2]
    return pl.pallas_call(
        _mm_kernel,
        grid=(n // tn, m // tm),
        in_specs=[pl.BlockSpec((tm, k), lambda j, i: (i, 0)),
                  pl.BlockSpec((None, k, tn), lambda j, i: (l, 0, j))],
        out_specs=pl.BlockSpec((tm, tn), lambda j, i: (i, j)),
        out_shape=jax.ShapeDtypeStruct((m, n), F32),
        scratch_shapes=[pltpu.VMEM((k, tn), BF16)],
        compiler_params=_cparams(2),
        name="matmul",
    )(a, w)


def _merge_kernel(ya_ref, yb_ref, yc_ref, wa_ref, wb_ref, wc_ref, ga_ref, gb_ref, gc_ref,
                  o_ref, sa_ref, sb_ref, sc_ref):
    @pl.when(pl.program_id(1) == 0)
    def _():
        sa_ref[...] = wa_ref[...].astype(BF16)
        sb_ref[...] = wb_ref[...].astype(BF16)
        sc_ref[...] = wc_ref[...].astype(BF16)

    acc = jax.nn.sigmoid(ga_ref[...]) * jnp.dot(ya_ref[...], sa_ref[...], preferred_element_type=F32)
    acc += jax.nn.sigmoid(gb_ref[...]) * jnp.dot(yb_ref[...], sb_ref[...], preferred_element_type=F32)
    acc += jax.nn.sigmoid(gc_ref[...]) * jnp.dot(yc_ref[...], sc_ref[...], preferred_element_type=F32)
    o_ref[...] = acc.astype(BF16)


def merge_branches(y_a, y_b, y_c, z, w_up_a, w_up_b, w_up_c, l):
    tm, tn = 1024, 512
    gate0 = COL_GATES // tn
    per_gate = D_MODEL // tn
    y_spec = pl.BlockSpec((tm, WIDTH), lambda j, i: (i, 0))
    w_spec = pl.BlockSpec((None, WIDTH, tn), lambda j, i: (l, 0, j))
    g_spec = lambda b: pl.BlockSpec((tm, tn), lambda j, i: (i, gate0 + b * per_gate + j))
    return pl.pallas_call(
        _merge_kernel,
        grid=(D_MODEL // tn, N_TOK // tm),
        in_specs=[y_spec, y_spec, y_spec, w_spec, w_spec, w_spec, g_spec(0), g_spec(1), g_spec(2)],
        out_specs=pl.BlockSpec((tm, tn), lambda j, i: (i, j)),
        out_shape=jax.ShapeDtypeStruct((N_TOK, D_MODEL), BF16),
        scratch_shapes=[pltpu.VMEM((WIDTH, tn), BF16)] * 3,
        compiler_params=_cparams(2),
        name="merge",
    )(y_a, y_b, y_c, w_up_a, w_up_b, w_up_c, z, z, z)


def _swiglu_kernel(h_ref, wu_ref, wg_ref, o_ref, su_ref, sg_ref):
    @pl.when(pl.program_id(1) == 0)
    def _():
        su_ref[...] = wu_ref[...].astype(BF16)
        sg_ref[...] = wg_ref[...].astype(BF16)

    h = h_ref[...]
    up = jnp.dot(h, su_ref[...], preferred_element_type=F32)
    gate = jnp.dot(h, sg_ref[...], preferred_element_type=F32)
    o_ref[...] = (_silu(gate) * up).astype(BF16)


def swiglu_in(h, w_ffn_in, l):
    tm, tn = 1024, 512
    n_up = D_FF // tn
    return pl.pallas_call(
        _swiglu_kernel,
        grid=(n_up, N_TOK // tm),
        in_specs=[pl.BlockSpec((tm, D_MODEL), lambda j, i: (i, 0)),
                  pl.BlockSpec((None, D_MODEL, tn), lambda j, i: (l, 0, j)),
                  pl.BlockSpec((None, D_MODEL, tn), lambda j, i: (l, 0, n_up + j))],
        out_specs=pl.BlockSpec((tm, tn), lambda j, i: (i, j)),
        out_shape=jax.ShapeDtypeStruct((N_TOK, D_FF), BF16),
        scratch_shapes=[pltpu.VMEM((D_MODEL, tn), BF16)] * 2,
        compiler_params=_cparams(2),
        name="swiglu_in",
    )(h, w_ffn_in, w_ffn_in)


def _mix_a_kernel(u_ref, v_ref, w_ref, bt_ref, o_ref, *, rows):
    w = w_ref[...].astype(BF16)
    for c in range(rows // A_CHUNK):
        rs = slice(c * A_CHUNK, (c + 1) * A_CHUNK)
        u = _gelu_tanh(u_ref[rs, :])
        v = _gelu_tanh(v_ref[rs, :])
        mu = jnp.mean(v, -1, keepdims=True)
        vc = v - mu
        vn = (vc * lax.rsqrt(jnp.mean(vc * vc, -1, keepdims=True) + NORM_EPS)).astype(BF16)
        for g in range(HEADS):
            cs = slice(g * HEAD_DIM, (g + 1) * HEAD_DIM)
            zg = jnp.dot(w[g], vn[:, cs], preferred_element_type=F32) + bt_ref[:, g:g + 1]
            o_ref[rs, cs] = (u[:, cs] * zg).astype(BF16)


def mix_a(z, a_w_l, a_b_l):
    rows = 512
    return pl.pallas_call(
        functools.partial(_mix_a_kernel, rows=rows),
        grid=(N_TOK // rows,),
        in_specs=[pl.BlockSpec((rows, WIDTH), lambda i: (i, COL_AU)),
                  pl.BlockSpec((rows, WIDTH), lambda i: (i, COL_AV)),
                  pl.BlockSpec((HEADS, A_CHUNK, A_CHUNK), lambda i: (0, 0, 0)),
                  pl.BlockSpec((A_CHUNK, HEADS), lambda i: (0, 0))],
        out_specs=pl.BlockSpec((rows, WIDTH), lambda i: (i, 0)),
        out_shape=jax.ShapeDtypeStruct((N_TOK, WIDTH), BF16),
        compiler_params=_cparams(1),
        name="mix_a",
    )(z, z, a_w_l, a_b_l.T)


def _softmax_pv(scores, values):
    m = scores[0].max(-1, keepdims=True)
    for s in scores[1:]:
        m = jnp.maximum(m, s.max(-1, keepdims=True))
    den = 0.0
    acc = 0.0
    for s, v in zip(scores, values):
        p = jnp.exp(s - m)
        den = den + p.sum(-1, keepdims=True)
        acc = acc + jnp.dot(p.astype(BF16), v, preferred_element_type=F32)
    return acc / den


def _qkt(q, k):
    return lax.dot_general(q, k, (((1,), (1,)), ((), ())), preferred_element_type=F32)


def _ctx_attn_kernel(q_ref, k_ref, v_ref, *rest):
    o_ref, ko_ref, vo_ref = rest[-3:]
    scale = HEAD_DIM ** -0.5
    ko_ref[...] = k_ref[...]
    vo_ref[...] = v_ref[...]
    for h in range(HEADS):
        cs = slice(h * HEAD_DIM, (h + 1) * HEAD_DIM)
        q = q_ref[:, cs].astype(BF16)
        k = k_ref[:, cs].astype(BF16)
        v = v_ref[:, cs].astype(BF16)
        o_ref[:, cs] = _softmax_pv([_qkt(q, k) * scale], [v]).astype(BF16)


def ctx_attention(z, l, kv_prev=None):
    spec = lambda col: pl.BlockSpec((SEQ, WIDTH), lambda b: (b, col))
    kv_spec = pl.BlockSpec((None, None, SEQ, WIDTH), lambda b: (b, l, 0, 0))
    kv_shape = jax.ShapeDtypeStruct((BATCH, DEPTH, SEQ, WIDTH), F32)
    in_specs = [spec(COL_BQ), spec(COL_BK), spec(COL_BV)]
    args = [z, z, z]
    aliases = {}
    if kv_prev is not None:
        in_specs += [pl.BlockSpec(memory_space=pl.ANY)] * 2
        args += list(kv_prev)
        aliases = {3: 1, 4: 2}
    return pl.pallas_call(
        _ctx_attn_kernel,
        grid=(BATCH,),
        in_specs=in_specs,
        out_specs=[pl.BlockSpec((SEQ, WIDTH), lambda b: (b, 0)), kv_spec, kv_spec],
        out_shape=[jax.ShapeDtypeStruct((N_TOK, WIDTH), BF16), kv_shape, kv_shape],
        input_output_aliases=aliases,
        compiler_params=_cparams(1),
        name="ctx_attn",
    )(*args)


def _rope(x, cos, sin_signed, first_quarter):
    partner = jnp.where(first_quarter, pltpu.roll(x, HEAD_DIM - 32, 1), pltpu.roll(x, 32, 1))
    return x * cos + partner * sin_signed


NB_GRID_ROWS = DEC_SEQ // GRID_W
NB_WIN_ROWS = min(NB_ROWS_MAX, NB_GRID_ROWS)
NB_Q_ROWS = 4


def _nb_row_start(qr):
    return min(max(qr - NB_WIN_ROWS // 2, 0), NB_GRID_ROWS - NB_WIN_ROWS)


def _nb_tiles():
    tiles = []
    for q0 in range(0, NB_GRID_ROWS, NB_Q_ROWS):
        starts = [_nb_row_start(qr) for qr in range(q0, q0 + NB_Q_ROWS)]
        lo, hi = min(starts), max(starts) + NB_WIN_ROWS
        tiles.append((q0, lo - lo % 2, hi + hi % 2))
    return tiles


def _nbhd_attn_kernel(q_ref, k_ref, v_ref, kc_ref, vc_ref, pair_ref, cos_ref, sin_ref, y_hbm, o_ref, bias_scr):
    del y_hbm
    scale = HEAD_DIM ** -0.5
    lane = lax.broadcasted_iota(jnp.int32, (1, HEAD_DIM), 1)
    first_quarter = (lane % 64) < 32

    @pl.when(pl.program_id(1) == 0)
    def _():
        left = lane < GRID_W
        neg = jnp.full((GRID_W, 2 * GRID_W), NEG_INF, F32)
        for q0, lo, hi in _nb_tiles():
            for qr in range(q0, q0 + NB_Q_ROWS):
                r0 = _nb_row_start(qr)
                for p in range(lo // 2, hi // 2):
                    ok0 = r0 <= 2 * p < r0 + NB_WIN_ROWS
                    ok1 = r0 <= 2 * p + 1 < r0 + NB_WIN_ROWS
                    tile = neg
                    if ok0 or ok1:
                        tile = pair_ref[2 * p - qr + NB_ROWS_MAX]
                        if not ok0:
                            tile = jnp.where(left, NEG_INF, tile)
                        if not ok1:
                            tile = jnp.where(left, tile, NEG_INF)
                    bias_scr[qr * GRID_W:(qr + 1) * GRID_W, p * 2 * GRID_W:(p + 1) * 2 * GRID_W] = tile

    k = _rope(k_ref[...], cos_ref[...], sin_ref[...], first_quarter).astype(BF16)
    v = v_ref[...].astype(BF16)
    kc = kc_ref[...].astype(BF16)
    vc = vc_ref[...].astype(BF16)
    for q0, lo, hi in _nb_tiles():
        rs = slice(q0 * GRID_W, (q0 + NB_Q_ROWS) * GRID_W)
        ks = slice(lo * GRID_W, hi * GRID_W)
        q = _rope(q_ref[rs, :], cos_ref[rs, :], sin_ref[rs, :], first_quarter).astype(BF16)
        s_loc = _qkt(q, k[ks]) * scale + bias_scr[rs, ks]
        s_ctx = _qkt(q, kc) * scale
        o_ref[rs, :] = _softmax_pv([s_loc, s_ctx], [v[ks], vc]).astype(BF16)


def nbhd_attention(z, cache_k, cache_v, pair_table, cos, sin_signed, y_b, l):
    lat0 = N_CTX // DEC_SEQ
    col = lambda c: (lambda h, b: (lat0 + b, c * HEADS + h))
    tok = lambda c: pl.BlockSpec((DEC_SEQ, HEAD_DIM), col(c))
    cache = pl.BlockSpec((None, None, PAST_LEN, HEAD_DIM), lambda h, b: (b, l, 0, h))
    table = pl.BlockSpec((DEC_SEQ, HEAD_DIM), lambda h, b: (0, 0))
    pairs = pl.BlockSpec((None, 2 * NB_ROWS_MAX, GRID_W, 2 * GRID_W), lambda h, b: (h, 0, 0, 0))
    return pl.pallas_call(
        _nbhd_attn_kernel,
        grid=(HEADS, DEC_BATCH),
        in_specs=[tok(COL_BQ), tok(COL_BK), tok(COL_BV), cache, cache, pairs, table, table,
                  pl.BlockSpec(memory_space=pl.ANY)],
        out_specs=pl.BlockSpec((DEC_SEQ, HEAD_DIM), lambda h, b: (lat0 + b, h)),
        out_shape=jax.ShapeDtypeStruct((N_TOK, WIDTH), BF16),
        scratch_shapes=[pltpu.VMEM((DEC_SEQ, DEC_SEQ), F32)],
        input_output_aliases={8: 0},
        compiler_params=_cparams(2),
        name="nbhd_attn",
    )(z, z, z, cache_k, cache_v, pair_table, cos, sin_signed, y_b)


def nbhd_pair_table(rpb_l):
    cols = jnp.arange(GRID_W)
    c_start = jnp.clip(cols - NB_COLS // 2, 0, GRID_W - NB_COLS)
    col_ok = (cols[None, :] >= c_start[:, None]) & (cols[None, :] < c_start[:, None] + NB_COLS)
    dc_idx = jnp.clip(cols[None, :] - cols[:, None] + NB_COLS - 1, 0, 2 * NB_COLS - 2)
    t = jnp.where(col_ok[None, None], rpb_l[:, :, dc_idx], NEG_INF).astype(F32)
    neg = jnp.full((HEADS, 1, GRID_W, GRID_W), NEG_INF, F32)
    tp = jnp.concatenate([neg, t, neg], 1)
    return jnp.concatenate([tp[:, :-1], tp[:, 1:]], -1)


def rope_tables():
    ax = HEAD_DIM // 2
    pos = jnp.arange(DEC_SEQ)
    inv = ROPE_THETA ** (-jnp.arange(0, ax, 2, dtype=F32) / ax)

    def ang(p):
        a = p.astype(F32)[:, None] * inv[None, :]
        return jnp.concatenate([a, a], -1)

    a = jnp.concatenate([ang(pos // GRID_W), ang(pos % GRID_W)], -1)
    sign = jnp.where((jnp.arange(HEAD_DIM) % ax) < ax // 2, -1.0, 1.0).astype(F32)
    return jnp.cos(a), jnp.sin(a) * sign[None, :]


def _split3(x):
    hi = x.astype(BF16)
    r = x - hi.astype(F32)
    mid = r.astype(BF16)
    lo = (r - mid.astype(F32)).astype(BF16)
    return hi, mid, lo


def _hgrn_chunk(qs, zf, v, loglb, log1mlb, onemlb, st, tri, pairs, b_scr, rev):
    c = HGRN_CHUNK
    ls = jnp.minimum(zf, 0.0) - jnp.log(1.0 + jnp.exp(-jnp.abs(zf)))
    x2 = log1mlb + ls
    lf = jnp.maximum(loglb, x2) + jnp.log(1.0 + jnp.exp(-jnp.abs(loglb - x2)))
    kk = onemlb * _sigmoid(-zf)
    f = 1.0 - kk

    b = sum(jnp.dot(tri, p, preferred_element_type=F32) for p in _split3(lf))
    b_scr[...] = b
    end = 0 if rev else c - 1
    b_end = b_scr[end:end + 1, :]

    o = _qkt((qs * jnp.exp(b)).astype(BF16), st.astype(BF16))
    k_dec = (kk * jnp.exp(b_end - b)).astype(BF16)
    vb = v.astype(BF16)
    st_new = st * jnp.exp(b_end) + lax.dot_general(vb, k_dec, (((0,), (0,)), ((), ())),
                                                   preferred_element_type=F32)

    attn = jnp.zeros((c, c), F32)
    kd = kk
    for delta in range(HGRN_BAND):
        if delta:
            kd3 = kd.reshape(c // HGRN_BAND, HGRN_BAND, HEAD_DIM)
            kd = f * pltpu.roll(kd3, HGRN_BAND - 1 if rev else 1, 1).reshape(c, HEAD_DIM)
        attn = jnp.where(pairs == delta, (qs * kd).sum(-1, keepdims=True), attn)

    for m in HGRN_LEVELS:
        seam = m // 2 if rev else m // 2 - 1
        g = jnp.concatenate([jnp.broadcast_to(b_scr[j * m + seam:j * m + seam + 1, :], (m, HEAD_DIM))
                             for j in range(c // m)], 0)
        e = jnp.exp(-jnp.abs(b - g))
        attn = jnp.where(pairs == m, _qkt((qs * e).astype(BF16), (kk * e).astype(BF16)), attn)
    o += jnp.dot(attn.astype(BF16), vb, preferred_element_type=F32)
    return o, st_new


def _hgrn_kernel(q_ref, ff_ref, fb_ref, v_ref, g_ref, lbp_ref, tri_ref, pairs_ref, *rest, n, latent):
    of_scr, ob_scr, bf_scr, bb_scr = rest[-4:]
    if latent:
        s0_ref, y_ref = rest[0], rest[2]
    else:
        y_ref, s_ref = rest[-6:-4]
    zero_init = not latent
    c = HGRN_CHUNK
    nc = n // c
    gate_f = [lbp_ref[r:r + 1, :] for r in range(3)]
    gate_b = [lbp_ref[3 + r:4 + r, :] for r in range(3)]

    def body(i, carry):
        st_f, st_b = carry
        rf = pl.ds(pl.multiple_of(i * c, c), c)
        rb = pl.ds(pl.multiple_of((nc - 1 - i) * c, c), c)
        o_f, st_f = _hgrn_chunk(_silu_tanh(q_ref[rf, :]), ff_ref[rf, :], v_ref[rf, :], *gate_f,
                                st_f, tri_ref[0], pairs_ref[0], bf_scr, False)
        o_b, st_b = _hgrn_chunk(_silu_tanh(q_ref[rb, :]), fb_ref[rb, :], v_ref[rb, :], *gate_b,
                                st_b, tri_ref[1], pairs_ref[1], bb_scr, True)
        of_scr[rf, :] = o_f
        ob_scr[rb, :] = o_b
        return st_f, st_b

    if zero_init:
        st0 = (jnp.zeros((HEAD_DIM, HEAD_DIM), F32),) * 2
    else:
        st0 = (s0_ref[0].T, s0_ref[1].T)
    st_f, st_b = lax.fori_loop(0, nc, body, st0)
    if not latent:
        s_ref[0] = st_f.T
        s_ref[1] = st_b.T

    def finish(i, carry):
        rs = pl.ds(pl.multiple_of(i * c, c), c)
        y_ref[rs, :] = (_rms(of_scr[rs, :] + ob_scr[rs, :]) * _silu_tanh(g_ref[rs, :])).astype(BF16)
        return carry

    lax.fori_loop(0, nc, finish, 0)


def _hgrn_constants():
    c = HGRN_CHUNK
    t = jnp.arange(c)[:, None]
    s = jnp.arange(c)[None, :]
    tri_f = (s <= t)
    pairs_f = jnp.where((t // HGRN_BAND == s // HGRN_BAND) & (s <= t), t - s, -1).astype(jnp.int32)
    for m in HGRN_LEVELS:
        h = m // 2
        pairs_f = jnp.where((t // h == s // h + 1) & ((t // h) % 2 == 1), m, pairs_f)
    tri = jnp.stack([tri_f, tri_f.T]).astype(BF16)
    pairs = jnp.stack([pairs_f, pairs_f.T])
    return tri, pairs


def hgrn(z, lbp, l, s_prev=None, latent=None):
    n = SEQ if latent is None else DEC_SEQ
    bsz = BATCH if latent is None else DEC_BATCH
    blk0 = 0 if latent is None else N_CTX // DEC_SEQ
    tri, pairs = _hgrn_constants()
    tok = lambda c: pl.BlockSpec((n, HEAD_DIM), lambda b, h: (blk0 + b, c * HEADS + h))
    raw = pl.BlockSpec(memory_space=pl.ANY)
    in_specs = [tok(COL_CQ), tok(COL_CFF), tok(COL_CFB), tok(COL_CI), tok(COL_CG),
                pl.BlockSpec((8, HEAD_DIM), lambda b, h: (0, h)),
                pl.BlockSpec((2, HGRN_CHUNK, HGRN_CHUNK), lambda b, h: (0, 0, 0)),
                pl.BlockSpec((2, HGRN_CHUNK, HGRN_CHUNK), lambda b, h: (0, 0, 0))]
    args = [z, z, z, z, z, lbp, tri, pairs]
    y_spec = pl.BlockSpec((n, HEAD_DIM), lambda b, h: (blk0 + b, h))
    y_shape = jax.ShapeDtypeStruct((N_TOK, WIDTH), BF16)
    if latent is None:
        out_specs = [y_spec, pl.BlockSpec((None, None, 2, None, HEAD_DIM, HEAD_DIM),
                                          lambda b, h: (b, l, 0, h, 0, 0))]
        out_shape = [y_shape, jax.ShapeDtypeStruct((BATCH, DEPTH, 2, HEADS, HEAD_DIM, HEAD_DIM), F32)]
        aliases = {}
        if s_prev is not None:
            in_specs.append(raw)
            args.append(s_prev)
            aliases = {8: 1}
    else:
        in_specs += [pl.BlockSpec((None, None, 2, None, HEAD_DIM, HEAD_DIM), lambda b, h: (b, l, 0, h, 0, 0)), raw]
        args += list(latent)
        out_specs, out_shape, aliases = y_spec, y_shape, {9: 0}
    return pl.pallas_call(
        functools.partial(_hgrn_kernel, n=n, latent=latent is not None),
        grid=(bsz, HEADS),
        in_specs=in_specs, out_specs=out_specs, out_shape=out_shape,
        scratch_shapes=[pltpu.VMEM((n, HEAD_DIM), F32)] * 2 + [pltpu.VMEM((HGRN_CHUNK, HEAD_DIM), F32)] * 2,
        input_output_aliases=aliases,
        compiler_params=_cparams(2),
        name="hgrn",
    )(*args)


def hgrn_gate_params(lb_logits):
    p = jax.nn.softmax(lb_logits.astype(F32), axis=1)
    cs = jnp.cumsum(p, axis=1)
    lb = cs - cs[:, :1]
    rows = [jnp.log(lb[0]), jnp.log1p(-lb[0]), 1.0 - lb[0],
            jnp.log(lb[1]), jnp.log1p(-lb[1]), 1.0 - lb[1],
            jnp.zeros_like(lb[0]), jnp.zeros_like(lb[0])]
    return jnp.stack(rows, axis=1)


def kernel(x_prompt, x_sample, cache_attn_k, cache_attn_v, state_hgrn, c, c_ctx, w_ada, b_ada, norm_g, w_in,
           a_spatial_w, a_spatial_b, nb_rpb, hgrn_lb_logits, w_up_a, w_up_b, w_up_c, w_out, w_ffn_in, w_ffn_out):
    x = jnp.concatenate([x_prompt.reshape(N_CTX, D_MODEL), x_sample.reshape(N_LAT, D_MODEL)], 0)
    cond = jnp.zeros((COND_ROWS, D_MODEL), F32).at[0].set(c_ctx).at[1:N_COND].set(c)
    mod = ada_all_layers(cond, w_ada, b_ada)
    mod = mod.reshape(DEPTH, COND_ROWS, 6, 1, D_MODEL).transpose(0, 2, 1, 3, 4)
    lbp = hgrn_gate_params(hgrn_lb_logits)
    cos, sin_signed = rope_tables()
    cache_k = cache_attn_k.reshape(DEC_BATCH, DEPTH, PAST_LEN, WIDTH)
    cache_v = cache_attn_v.reshape(DEC_BATCH, DEPTH, PAST_LEN, WIDTH)
    gain = lambda l, i: norm_g[l, i][None, :]

    kv_all, s_all = None, None
    h = prenorm(x, gain(0, 0), mod[0, 1], mod[0, 0])
    for l in range(DEPTH):
        sh1, sc1, g1, sh2, sc2, g2 = (mod[l, i] for i in range(6))
        z = matmul_layer(h, w_in, l, tm=1024, tn=1024)
        y_a = mix_a(z, a_spatial_w[l], a_spatial_b[l])
        y_b, *kv_all = ctx_attention(z, l, kv_all)
        y_b = nbhd_attention(z, cache_k, cache_v, nbhd_pair_table(nb_rpb[l]), cos, sin_signed, y_b, l)
        y_c, s_all = hgrn(z, lbp[l], l, s_prev=s_all)
        y_c = hgrn(z, lbp[l], l, latent=(state_hgrn, y_c))
        merged = merge_branches(y_a, y_b, y_c, z, w_up_a, w_up_b, w_up_c, l)
        t = matmul_layer(merged, w_out, l, tm=1024, tn=1024)
        x, h2 = resnorm(t, x, gain(l, 1), g1, nxt=(gain(l, 2), sc2, sh2))
        act = swiglu_in(h2, w_ffn_in, l)
        t = matmul_layer(act, w_ffn_out, l, tm=512, tn=512)
        nxt = None if l == DEPTH - 1 else (gain(l + 1, 0), mod[l + 1, 1], mod[l + 1, 0])
        x, h = resnorm(t, x, gain(l, 3), g2, nxt=nxt)

    y_prompt = x[:N_CTX].reshape(BATCH, SEQ, D_MODEL)
    y_sample = x[N_CTX:].reshape(DEC_BATCH, DEC_SEQ, D_MODEL)
    kv_shape = (BATCH, DEPTH, SEQ, HEADS, HEAD_DIM)
    return (y_prompt, y_sample, kv_all[0].reshape(kv_shape), kv_all[1].reshape(kv_shape), s_all)
```

```python
import functools
import math

import jax
import jax.numpy as jnp
from jax import lax
from jax.experimental import pallas as pl
from jax.experimental.pallas import tpu as pltpu

F32 = jnp.float32
BF16 = jnp.bfloat16

D_MODEL = 2048
BATCH = 32
SEQ = 256
DEPTH = 4
DEC_BATCH = 2
DEC_SEQ = 1024
PAST_LEN = 512
GRID_W = 64
HEADS = 8
HEAD_DIM = 128
WIDTH = HEADS * HEAD_DIM
A_CHUNK = 128
NB_ROWS_MAX = 8
NB_COLS = 16
ROPE_THETA = 10000.0
D_FF = -(-8 * D_MODEL // (3 * 256)) * 256
NORM_EPS = 1e-6
NEG_INF = -1e30
IN_WIDTH = 10 * WIDTH + 3 * D_MODEL

N_CTX = BATCH * SEQ
N_LAT = DEC_BATCH * DEC_SEQ
N_TOK = N_CTX + N_LAT
N_COND = 1 + DEC_BATCH
COND_ROWS = 8

COL_AU, COL_AV, COL_BQ, COL_BK, COL_BV, COL_CQ, COL_CFF, COL_CFB, COL_CI, COL_CG = range(10)
COL_GATES = 10 * WIDTH

HGRN_CHUNK = 128
HGRN_BAND = 4
HGRN_LEVELS = (8, 16, 32, 64, 128)
SUBLANES = 8
HGRN_CHUNKS_PER_TRIP = 2

LOG2E = math.log2(math.e)

VMEM_LIMIT = 48 * 1024 * 1024


def _cparams(n_axes):
    return pltpu.CompilerParams(dimension_semantics=("arbitrary",) * n_axes,
                                vmem_limit_bytes=VMEM_LIMIT)


def _cond_row(i, tm):
    n_ctx_tiles = N_CTX // tm
    return jnp.where(i < n_ctx_tiles, 0, 1 + (i - n_ctx_tiles) // (DEC_SEQ // tm))


def _silu(x):
    return x * jax.nn.sigmoid(x)


def _sigmoid(x):
    return 0.5 * (1.0 + jnp.tanh(0.5 * x))


def _silu_tanh(x):
    hx = 0.5 * x
    return hx + hx * jnp.tanh(hx)


def _gelu_tanh(x):
    c = math.sqrt(2.0 / math.pi)
    return x * (0.5 * (1.0 + jnp.tanh(c * (x + 0.044715 * (x * x * x)))))


def _rms(x):
    return x * lax.rsqrt(jnp.mean(x * x, -1, keepdims=True) + NORM_EPS)


def _ada_kernel(c_ref, w_ref, b_ref, o_ref):
    s = _silu(c_ref[...]).astype(BF16)
    o_ref[...] = jnp.dot(s, w_ref[...].astype(BF16), preferred_element_type=F32) + b_ref[...]


def ada_all_layers(cond, w_ada, b_ada):
    tn = 1024
    n_out = 6 * D_MODEL
    return pl.pallas_call(
        _ada_kernel,
        grid=(DEPTH, n_out // tn),
        in_specs=[pl.BlockSpec((COND_ROWS, D_MODEL), lambda l, j: (0, 0)),
                  pl.BlockSpec((None, D_MODEL, tn), lambda l, j: (l, 0, j)),
                  pl.BlockSpec((None, 1, tn), lambda l, j: (l, 0, j))],
        out_specs=pl.BlockSpec((None, COND_ROWS, tn), lambda l, j: (l, 0, j)),
        out_shape=jax.ShapeDtypeStruct((DEPTH, COND_ROWS, n_out), F32),
        compiler_params=_cparams(2),
        name="ada",
    )(cond, w_ada, b_ada.reshape(DEPTH, 1, n_out))


def _prenorm_kernel(x_ref, g_ref, sc_ref, sh_ref, h_ref):
    y = _rms(x_ref[...]) * g_ref[...]
    h_ref[...] = (y * (1.0 + sc_ref[...]) + sh_ref[...]).astype(BF16)


def prenorm(x, gain, sc, sh):
    tm = 512
    row = lambda i: (_cond_row(i, tm), 0, 0)
    return pl.pallas_call(
        _prenorm_kernel,
        grid=(N_TOK // tm,),
        in_specs=[pl.BlockSpec((tm, D_MODEL), lambda i: (i, 0)),
                  pl.BlockSpec((1, D_MODEL), lambda i: (0, 0)),
                  pl.BlockSpec((None, 1, D_MODEL), row),
                  pl.BlockSpec((None, 1, D_MODEL), row)],
        out_specs=pl.BlockSpec((tm, D_MODEL), lambda i: (i, 0)),
        out_shape=jax.ShapeDtypeStruct((N_TOK, D_MODEL), BF16),
        compiler_params=_cparams(1),
        name="prenorm",
    )(x, gain, sc, sh)


def _proj_resnorm_kernel(a_ref, w_ref, x_ref, g_ref, gate_ref, *rest, nk, with_next):
    if with_next:
        g2_ref, sc_ref, sh_ref, xo_ref, h_ref = rest[:5]
    else:
        xo_ref = rest[0]
    part = jnp.dot(a_ref[...], w_ref[...], preferred_element_type=F32)

    def finish(t):
        x_new = x_ref[...] + gate_ref[...] * (_rms(t) * g_ref[...])
        xo_ref[...] = x_new
        if with_next:
            h_ref[...] = (_rms(x_new) * g2_ref[...] * (1.0 + sc_ref[...]) + sh_ref[...]).astype(BF16)

    if nk == 1:
        finish(part)
        return
    acc_ref = rest[-1]
    k = pl.program_id(1)

    @pl.when(k == 0)
    def _():
        acc_ref[...] = part

    @pl.when((k > 0) & (k < nk - 1))
    def _():
        acc_ref[...] += part

    @pl.when(k == nk - 1)
    def _():
        finish(acc_ref[...] + part)


def proj_resnorm(a, w, l, x, gain, gate, nxt=None, tk=None):
    tm = 512
    tk = tk or a.shape[1]
    nk = a.shape[1] // tk
    row = lambda i, k: (_cond_row(i, tm), 0, 0)
    tile = pl.BlockSpec((tm, D_MODEL), lambda i, k: (i, 0))
    vec = pl.BlockSpec((1, D_MODEL), lambda i, k: (0, 0))
    mod = pl.BlockSpec((None, 1, D_MODEL), row)
    w_mode = dict(pipeline_mode=pl.Buffered(1)) if nk == 1 else {}
    in_specs = [pl.BlockSpec((tm, tk), lambda i, k: (i, k)),
                pl.BlockSpec((None, tk, D_MODEL), lambda i, k: (l, k, 0), **w_mode), tile, vec, mod]
    args = [a, w, x, gain, gate]
    out_specs = [tile]
    out_shape = [jax.ShapeDtypeStruct((N_TOK, D_MODEL), F32)]
    if nxt is not None:
        in_specs += [vec, mod, mod]
        args += list(nxt)
        out_specs.append(tile)
        out_shape.append(jax.ShapeDtypeStruct((N_TOK, D_MODEL), BF16))
    out = pl.pallas_call(
        functools.partial(_proj_resnorm_kernel, nk=nk, with_next=nxt is not None),
        grid=(N_TOK // tm, nk),
        in_specs=in_specs, out_specs=out_specs, out_shape=out_shape,
        scratch_shapes=[pltpu.VMEM((tm, D_MODEL), F32)] if nk > 1 else [],
        compiler_params=_cparams(2),
        name="proj_resnorm",
    )(*args)
    return out if nxt is not None else (out[0], None)


def _mm_kernel(a_ref, w_ref, o_ref, wb_ref):
    @pl.when(pl.program_id(1) == 0)
    def _():
        wb_ref[...] = w_ref[...].astype(BF16)

    o_ref[...] = jnp.dot(a_ref[...], wb_ref[...], preferred_element_type=F32)


def matmul_layer(a, w, l, tm, tn):
    m, k = a.shape
    n = w.shape[2]
    return pl.pallas_call(
        _mm_kernel,
        grid=(n // tn, m // tm),
        in_specs=[pl.BlockSpec((tm, k), lambda j, i: (i, 0)),
                  pl.BlockSpec((None, k, tn), lambda j, i: (l, 0, j))],
        out_specs=pl.BlockSpec((tm, tn), lambda j, i: (i, j)),
        out_shape=jax.ShapeDtypeStruct((m, n), F32),
        scratch_shapes=[pltpu.VMEM((k, tn), BF16)],
        compiler_params=_cparams(2),
        name="matmul",
    )(a, w)


def _merge_kernel(ya_ref, yb_ref, yc_ref, wa_ref, wb_ref, wc_ref, ga_ref, gb_ref, gc_ref, o_ref):
    acc = _sigmoid(ga_ref[...]) * jnp.dot(ya_ref[...], wa_ref[...], preferred_element_type=F32)
    acc += _sigmoid(gb_ref[...]) * jnp.dot(yb_ref[...], wb_ref[...], preferred_element_type=F32)
    acc += _sigmoid(gc_ref[...]) * jnp.dot(yc_ref[...], wc_ref[...], preferred_element_type=F32)
    o_ref[...] = acc.astype(BF16)


def merge_branches(y_a, y_b, y_c, z, w_up_a, w_up_b, w_up_c, l):
    tm, tn = 1024, 512
    gate0 = COL_GATES // tn
    per_gate = D_MODEL // tn
    y_spec = pl.BlockSpec((tm, WIDTH), lambda i, j: (i, 0))
    w_spec = pl.BlockSpec((None, WIDTH, tn), lambda i, j: (l, 0, j))
    g_spec = lambda b: pl.BlockSpec((tm, tn), lambda i, j: (i, gate0 + b * per_gate + j))
    return pl.pallas_call(
        _merge_kernel,
        grid=(N_TOK // tm, D_MODEL // tn),
        in_specs=[y_spec, y_spec, y_spec, w_spec, w_spec, w_spec, g_spec(0), g_spec(1), g_spec(2)],
        out_specs=pl.BlockSpec((tm, tn), lambda i, j: (i, j)),
        out_shape=jax.ShapeDtypeStruct((N_TOK, D_MODEL), BF16),
        compiler_params=_cparams(2),
        name="merge",
    )(y_a, y_b, y_c, w_up_a, w_up_b, w_up_c, z, z, z)


def _swiglu_kernel(h_ref, wu_ref, wg_ref, o_ref, su_ref, sg_ref):
    @pl.when(pl.program_id(1) == 0)
    def _():
        su_ref[...] = wu_ref[...].astype(BF16)
        sg_ref[...] = wg_ref[...].astype(BF16)

    h = h_ref[...]
    up = jnp.dot(h, su_ref[...], preferred_element_type=F32)
    gate = jnp.dot(h, sg_ref[...], preferred_element_type=F32)
    o_ref[...] = (_silu(gate) * up).astype(BF16)


def swiglu_in(h, w_ffn_in, l):
    tm, tn = 1024, 512
    n_up = D_FF // tn
    return pl.pallas_call(
        _swiglu_kernel,
        grid=(n_up, N_TOK // tm),
        in_specs=[pl.BlockSpec((tm, D_MODEL), lambda j, i: (i, 0)),
                  pl.BlockSpec((None, D_MODEL, tn), lambda j, i: (l, 0, j)),
                  pl.BlockSpec((None, D_MODEL, tn), lambda j, i: (l, 0, n_up + j))],
        out_specs=pl.BlockSpec((tm, tn), lambda j, i: (i, j)),
        out_shape=jax.ShapeDtypeStruct((N_TOK, D_FF), BF16),
        scratch_shapes=[pltpu.VMEM((D_MODEL, tn), BF16)] * 2,
        compiler_params=_cparams(2),
        name="swiglu_in",
    )(h, w_ffn_in, w_ffn_in)


def _mix_a_kernel(u_ref, v_ref, w_ref, bt_ref, o_ref, *, rows):
    w = w_ref[...].astype(BF16)
    for c in range(rows // A_CHUNK):
        rs = slice(c * A_CHUNK, (c + 1) * A_CHUNK)
        u = _gelu_tanh(u_ref[rs, :])
        v = _gelu_tanh(v_ref[rs, :])
        mu = jnp.mean(v, -1, keepdims=True)
        vc = v - mu
        vn = (vc * lax.rsqrt(jnp.mean(vc * vc, -1, keepdims=True) + NORM_EPS)).astype(BF16)
        for g in range(HEADS):
            cs = slice(g * HEAD_DIM, (g + 1) * HEAD_DIM)
            zg = jnp.dot(w[g], vn[:, cs], preferred_element_type=F32) + bt_ref[:, g:g + 1]
            o_ref[rs, cs] = (u[:, cs] * zg).astype(BF16)


def mix_a(z, a_w_l, a_b_l):
    rows = 512
    return pl.pallas_call(
        functools.partial(_mix_a_kernel, rows=rows),
        grid=(N_TOK // rows,),
        in_specs=[pl.BlockSpec((rows, WIDTH), lambda i: (i, COL_AU)),
                  pl.BlockSpec((rows, WIDTH), lambda i: (i, COL_AV)),
                  pl.BlockSpec((HEADS, A_CHUNK, A_CHUNK), lambda i: (0, 0, 0)),
                  pl.BlockSpec((A_CHUNK, HEADS), lambda i: (0, 0))],
        out_specs=pl.BlockSpec((rows, WIDTH), lambda i: (i, 0)),
        out_shape=jax.ShapeDtypeStruct((N_TOK, WIDTH), BF16),
        compiler_params=_cparams(1),
        name="mix_a",
    )(z, z, a_w_l, a_b_l.T)


def _softmax_pv(scores, values):
    m = scores[0].max(-1, keepdims=True)
    for s in scores[1:]:
        m = jnp.maximum(m, s.max(-1, keepdims=True))
    den = 0.0
    acc = 0.0
    for s, v in zip(scores, values):
        p = jnp.exp(s - m)
        den = den + p.sum(-1, keepdims=True)
        acc = acc + jnp.dot(p.astype(BF16), v, preferred_element_type=F32)
    return acc / den


def _qkt(q, k):
    return lax.dot_general(q, k, (((1,), (1,)), ((), ())), preferred_element_type=F32)


def _ctx_attn_kernel(q_ref, k_ref, v_ref, *rest):
    o_ref, ko_ref, vo_ref = rest[-3:]
    scale = HEAD_DIM ** -0.5
    ko_ref[...] = k_ref[...]
    vo_ref[...] = v_ref[...]
    for h in range(HEADS):
        cs = slice(h * HEAD_DIM, (h + 1) * HEAD_DIM)
        q = q_ref[:, cs].astype(BF16)
        k = k_ref[:, cs].astype(BF16)
        v = v_ref[:, cs].astype(BF16)
        o_ref[:, cs] = _softmax_pv([_qkt(q, k) * scale], [v]).astype(BF16)


def ctx_attention(z, l, kv_prev=None):
    spec = lambda col: pl.BlockSpec((SEQ, WIDTH), lambda b: (b, col))
    kv_spec = pl.BlockSpec((None, None, SEQ, WIDTH), lambda b: (b, l, 0, 0))
    kv_shape = jax.ShapeDtypeStruct((BATCH, DEPTH, SEQ, WIDTH), F32)
    in_specs = [spec(COL_BQ), spec(COL_BK), spec(COL_BV)]
    args = [z, z, z]
    aliases = {}
    if kv_prev is not None:
        in_specs += [pl.BlockSpec(memory_space=pl.ANY)] * 2
        args += list(kv_prev)
        aliases = {3: 1, 4: 2}
    return pl.pallas_call(
        _ctx_attn_kernel,
        grid=(BATCH,),
        in_specs=in_specs,
        out_specs=[pl.BlockSpec((SEQ, WIDTH), lambda b: (b, 0)), kv_spec, kv_spec],
        out_shape=[jax.ShapeDtypeStruct((N_TOK, WIDTH), BF16), kv_shape, kv_shape],
        input_output_aliases=aliases,
        compiler_params=_cparams(1),
        name="ctx_attn",
    )(*args)


def _rope(x, cos, sin_signed, first_quarter):
    partner = jnp.where(first_quarter, pltpu.roll(x, HEAD_DIM - 32, 1), pltpu.roll(x, 32, 1))
    return x * cos + partner * sin_signed


NB_GRID_ROWS = DEC_SEQ // GRID_W
NB_WIN_ROWS = min(NB_ROWS_MAX, NB_GRID_ROWS)
NB_Q_ROWS = 4


def _nb_row_start(qr):
    return min(max(qr - NB_WIN_ROWS // 2, 0), NB_GRID_ROWS - NB_WIN_ROWS)


def _nb_tiles():
    tiles = []
    for q0 in range(0, NB_GRID_ROWS, NB_Q_ROWS):
        starts = [_nb_row_start(qr) for qr in range(q0, q0 + NB_Q_ROWS)]
        lo, hi = min(starts), max(starts) + NB_WIN_ROWS
        tiles.append((q0, lo - lo % 2, hi + hi % 2))
    return tiles


def _nbhd_attn_kernel(q_ref, k_ref, v_ref, kc_ref, vc_ref, pair_ref, cos_ref, sin_ref, y_hbm, o_ref, bias_scr):
    del y_hbm
    scale = HEAD_DIM ** -0.5
    lane = lax.broadcasted_iota(jnp.int32, (1, HEAD_DIM), 1)
    first_quarter = (lane % 64) < 32

    @pl.when(pl.program_id(1) == 0)
    def _():
        left = lane < GRID_W
        neg = jnp.full((GRID_W, 2 * GRID_W), NEG_INF, F32)
        for q0, lo, hi in _nb_tiles():
            for qr in range(q0, q0 + NB_Q_ROWS):
                r0 = _nb_row_start(qr)
                for p in range(lo // 2, hi // 2):
                    ok0 = r0 <= 2 * p < r0 + NB_WIN_ROWS
                    ok1 = r0 <= 2 * p + 1 < r0 + NB_WIN_ROWS
                    tile = neg
                    if ok0 or ok1:
                        tile = pair_ref[2 * p - qr + NB_ROWS_MAX]
                        if not ok0:
                            tile = jnp.where(left, NEG_INF, tile)
                        if not ok1:
                            tile = jnp.where(left, tile, NEG_INF)
                    bias_scr[qr * GRID_W:(qr + 1) * GRID_W, p * 2 * GRID_W:(p + 1) * 2 * GRID_W] = tile

    k = _rope(k_ref[...], cos_ref[...], sin_ref[...], first_quarter).astype(BF16)
    v = v_ref[...].astype(BF16)
    kc = kc_ref[...].astype(BF16)
    vc = vc_ref[...].astype(BF16)
    for q0, lo, hi in _nb_tiles():
        rs = slice(q0 * GRID_W, (q0 + NB_Q_ROWS) * GRID_W)
        ks = slice(lo * GRID_W, hi * GRID_W)
        q = _rope(q_ref[rs, :], cos_ref[rs, :], sin_ref[rs, :], first_quarter).astype(BF16)
        s_loc = _qkt(q, k[ks]) * scale + bias_scr[rs, ks]
        s_ctx = _qkt(q, kc) * scale
        o_ref[rs, :] = _softmax_pv([s_loc, s_ctx], [v[ks], vc]).astype(BF16)


def nbhd_attention(z, cache_k, cache_v, pair_table, cos, sin_signed, y_b, l):
    lat0 = N_CTX // DEC_SEQ
    col = lambda c: (lambda h, b: (lat0 + b, c * HEADS + h))
    tok = lambda c: pl.BlockSpec((DEC_SEQ, HEAD_DIM), col(c))
    cache = pl.BlockSpec((None, None, PAST_LEN, HEAD_DIM), lambda h, b: (b, l, 0, h))
    table = pl.BlockSpec((DEC_SEQ, HEAD_DIM), lambda h, b: (0, 0))
    pairs = pl.BlockSpec((None, 2 * NB_ROWS_MAX, GRID_W, 2 * GRID_W), lambda h, b: (h, 0, 0, 0))
    return pl.pallas_call(
        _nbhd_attn_kernel,
        grid=(HEADS, DEC_BATCH),
        in_specs=[tok(COL_BQ), tok(COL_BK), tok(COL_BV), cache, cache, pairs, table, table,
                  pl.BlockSpec(memory_space=pl.ANY)],
        out_specs=pl.BlockSpec((DEC_SEQ, HEAD_DIM), lambda h, b: (lat0 + b, h)),
        out_shape=jax.ShapeDtypeStruct((N_TOK, WIDTH), BF16),
        scratch_shapes=[pltpu.VMEM((DEC_SEQ, DEC_SEQ), F32)],
        input_output_aliases={8: 0},
        compiler_params=_cparams(2),
        name="nbhd_attn",
    )(z, z, z, cache_k, cache_v, pair_table, cos, sin_signed, y_b)


def nbhd_pair_table(rpb_l):
    cols = jnp.arange(GRID_W)
    c_start = jnp.clip(cols - NB_COLS // 2, 0, GRID_W - NB_COLS)
    col_ok = (cols[None, :] >= c_start[:, None]) & (cols[None, :] < c_start[:, None] + NB_COLS)
    dc_idx = jnp.clip(cols[None, :] - cols[:, None] + NB_COLS - 1, 0, 2 * NB_COLS - 2)
    t = jnp.where(col_ok[None, None], rpb_l[:, :, dc_idx], NEG_INF).astype(F32)
    neg = jnp.full((HEADS, 1, GRID_W, GRID_W), NEG_INF, F32)
    tp = jnp.concatenate([neg, t, neg], 1)
    return jnp.concatenate([tp[:, :-1], tp[:, 1:]], -1)


def rope_tables():
    ax = HEAD_DIM // 2
    pos = jnp.arange(DEC_SEQ)
    inv = ROPE_THETA ** (-jnp.arange(0, ax, 2, dtype=F32) / ax)

    def ang(p):
        a = p.astype(F32)[:, None] * inv[None, :]
        return jnp.concatenate([a, a], -1)

    a = jnp.concatenate([ang(pos // GRID_W), ang(pos % GRID_W)], -1)
    sign = jnp.where((jnp.arange(HEAD_DIM) % ax) < ax // 2, -1.0, 1.0).astype(F32)
    return jnp.cos(a), jnp.sin(a) * sign[None, :]


def _split3(x):
    hi = x.astype(BF16)
    r = x - hi.astype(F32)
    mid = r.astype(BF16)
    lo = (r - mid.astype(F32)).astype(BF16)
    return hi, mid, lo


def _hgrn_chunks(chains, st, tri_ref, pairs_ref, side_ref):
    c = HGRN_CHUNK
    st = list(st)
    qs, kk, ff, bb, vb = [], [], [], [], []
    for q, zf, v, (log2lb, log21mlb, half1mlb), d, b_scr in chains:
        z2 = zf * LOG2E
        ls = jnp.minimum(z2, 0.0) - jnp.log2(1.0 + jnp.exp2(-jnp.abs(z2)))
        x2 = log21mlb + ls
        lf = jnp.maximum(log2lb, x2) + jnp.log2(1.0 + jnp.exp2(-jnp.abs(log2lb - x2)))
        k = half1mlb * (1.0 - jnp.tanh(0.5 * zf))
        b = sum(jnp.dot(tri_ref[d], p, preferred_element_type=F32) for p in _split3(lf))
        b_scr[...] = b
        qs.append(_silu_tanh(q))
        kk.append(k)
        ff.append(1.0 - k)
        bb.append(b)
        vb.append(v.astype(BF16))

    outs = []
    for i, (_, _, _, _, d, b_scr) in enumerate(chains):
        end = 0 if d else c - 1
        b_end = b_scr[end:end + 1, :]
        outs.append(_qkt((qs[i] * jnp.exp2(bb[i])).astype(BF16), st[d].astype(BF16)))
        k_dec = (kk[i] * jnp.exp2(b_end - bb[i])).astype(BF16)
        st[d] = st[d] * jnp.exp2(b_end) + lax.dot_general(vb[i], k_dec, (((0,), (0,)), ((), ())),
                                                          preferred_element_type=F32)

    attn = []
    for i, (_, _, _, _, d, _) in enumerate(chains):
        a = jnp.zeros((c, c), F32)
        kd = kk[i]
        for delta in range(HGRN_BAND):
            if delta:
                kd3 = kd.reshape(c // SUBLANES, SUBLANES, HEAD_DIM)
                kd = ff[i] * pltpu.roll(kd3, SUBLANES - 1 if d else 1, 1).reshape(c, HEAD_DIM)
            a = jnp.where(pairs_ref[d] == delta, (qs[i] * kd).sum(-1, keepdims=True), a)
        attn.append(a)

    qb = [x.astype(BF16) for x in qs]
    kb = [x.astype(BF16) for x in kk]
    for lv, m in enumerate(HGRN_LEVELS):
        for i, (_, _, _, _, d, b_scr) in enumerate(chains):
            seam = m // 2 if d else m // 2 - 1
            g = jnp.concatenate([jnp.broadcast_to(b_scr[j * m + seam:j * m + seam + 1, :], (m, HEAD_DIM))
                                 for j in range(c // m)], 0)
            e = jnp.exp2((bb[i] - g) * side_ref[d, lv]).astype(BF16)
            attn[i] = jnp.where(pairs_ref[d] == m, _qkt(qb[i] * e, kb[i] * e), attn[i])
    for i in range(len(chains)):
        outs[i] += jnp.dot(attn[i].astype(BF16), vb[i], preferred_element_type=F32)
    return outs, st


def _hgrn_kernel(q_ref, ff_ref, fb_ref, v_ref, g_ref, lbp_ref, tri_ref, pairs_ref, side_ref, *rest, n, latent):
    of_scr, ob_scr, b_scr = rest[-3:]
    if latent:
        s0_ref, y_ref = rest[0], rest[2]
    else:
        y_ref, s_ref = rest[-5:-3]
    zero_init = not latent
    c = HGRN_CHUNK
    nc = n // c
    gate_f = [lbp_ref[r:r + 1, :] for r in range(3)]
    gate_b = [lbp_ref[3 + r:4 + r, :] for r in range(3)]

    def body(i, st):
        chains, dst = [], []
        for u in range(HGRN_CHUNKS_PER_TRIP):
            cf = i * HGRN_CHUNKS_PER_TRIP + u
            rf = pl.ds(pl.multiple_of(cf * c, c), c)
            rb = pl.ds(pl.multiple_of((nc - 1 - cf) * c, c), c)
            chains.append((q_ref[rf, :], ff_ref[rf, :], v_ref[rf, :], gate_f, 0, b_scr.at[2 * u]))
            chains.append((q_ref[rb, :], fb_ref[rb, :], v_ref[rb, :], gate_b, 1, b_scr.at[2 * u + 1]))
            dst += [(of_scr, rf), (ob_scr, rb)]
        outs, st = _hgrn_chunks(chains, st, tri_ref, pairs_ref, side_ref)
        for (scr, rows), o in zip(dst, outs):
            scr[rows, :] = o
        return tuple(st)

    if zero_init:
        st0 = (jnp.zeros((HEAD_DIM, HEAD_DIM), F32),) * 2
    else:
        st0 = (s0_ref[0].T, s0_ref[1].T)
    st_f, st_b = lax.fori_loop(0, nc // HGRN_CHUNKS_PER_TRIP, body, st0)
    if not latent:
        s_ref[0] = st_f.T
        s_ref[1] = st_b.T

    def finish(i, carry):
        rs = pl.ds(pl.multiple_of(i * c, c), c)
        y_ref[rs, :] = (_rms(of_scr[rs, :] + ob_scr[rs, :]) * _silu_tanh(g_ref[rs, :])).astype(BF16)
        return carry

    lax.fori_loop(0, nc, finish, 0)


def _hgrn_constants():
    c = HGRN_CHUNK
    t = jnp.arange(c)[:, None]
    s = jnp.arange(c)[None, :]
    tri_f = (s <= t)
    pairs_f = jnp.where((t // HGRN_BAND == s // HGRN_BAND) & (s <= t), t - s, -1).astype(jnp.int32)
    for m in HGRN_LEVELS:
        h = m // 2
        pairs_f = jnp.where((t // h == s // h + 1) & ((t // h) % 2 == 1), m, pairs_f)
    tri = jnp.stack([tri_f, tri_f.T]).astype(BF16)
    pairs = jnp.stack([pairs_f, pairs_f.T])
    later_half = jnp.stack([(jnp.arange(c) % m) >= m // 2 for m in HGRN_LEVELS])
    side_f = jnp.broadcast_to(jnp.where(later_half, 1.0, -1.0)[:, :, None], (len(HGRN_LEVELS), c, HEAD_DIM))
    side = jnp.stack([side_f, -side_f]).astype(F32)
    return tri, pairs, side


def hgrn(z, lbp, l, s_prev=None, latent=None):
    n = SEQ if latent is None else DEC_SEQ
    bsz = BATCH if latent is None else DEC_BATCH
    blk0 = 0 if latent is None else N_CTX // DEC_SEQ
    tri, pairs, side = _hgrn_constants()
    tok = lambda c: pl.BlockSpec((n, HEAD_DIM), lambda b, h: (blk0 + b, c * HEADS + h))
    raw = pl.BlockSpec(memory_space=pl.ANY)
    in_specs = [tok(COL_CQ), tok(COL_CFF), tok(COL_CFB), tok(COL_CI), tok(COL_CG),
                pl.BlockSpec((8, HEAD_DIM), lambda b, h: (0, h)),
                pl.BlockSpec((2, HGRN_CHUNK, HGRN_CHUNK), lambda b, h: (0, 0, 0)),
                pl.BlockSpec((2, HGRN_CHUNK, HGRN_CHUNK), lambda b, h: (0, 0, 0)),
                pl.BlockSpec(side.shape, lambda b, h: (0, 0, 0, 0))]
    args = [z, z, z, z, z, lbp, tri, pairs, side]
    y_spec = pl.BlockSpec((n, HEAD_DIM), lambda b, h: (blk0 + b, h))
    y_shape = jax.ShapeDtypeStruct((N_TOK, WIDTH), BF16)
    if latent is None:
        out_specs = [y_spec, pl.BlockSpec((None, None, 2, None, HEAD_DIM, HEAD_DIM),
                                          lambda b, h: (b, l, 0, h, 0, 0))]
        out_shape = [y_shape, jax.ShapeDtypeStruct((BATCH, DEPTH, 2, HEADS, HEAD_DIM, HEAD_DIM), F32)]
        aliases = {}
        if s_prev is not None:
            aliases = {len(args): 1}
            in_specs.append(raw)
            args.append(s_prev)
    else:
        aliases = {len(args) + 1: 0}
        in_specs += [pl.BlockSpec((None, None, 2, None, HEAD_DIM, HEAD_DIM), lambda b, h: (b, l, 0, h, 0, 0)), raw]
        args += list(latent)
        out_specs, out_shape = y_spec, y_shape
    return pl.pallas_call(
        functools.partial(_hgrn_kernel, n=n, latent=latent is not None),
        grid=(bsz, HEADS),
        in_specs=in_specs, out_specs=out_specs, out_shape=out_shape,
        scratch_shapes=[pltpu.VMEM((n, HEAD_DIM), F32)] * 2
        + [pltpu.VMEM((2 * HGRN_CHUNKS_PER_TRIP, HGRN_CHUNK, HEAD_DIM), F32)],
        input_output_aliases=aliases,
        compiler_params=_cparams(2),
        name="hgrn",
    )(*args)


def hgrn_gate_params(lb_logits):
    p = jax.nn.softmax(lb_logits.astype(F32), axis=1)
    cs = jnp.cumsum(p, axis=1)
    lb = cs - cs[:, :1]
    rows = [jnp.log(lb[0]) * LOG2E, jnp.log1p(-lb[0]) * LOG2E, 0.5 * (1.0 - lb[0]),
            jnp.log(lb[1]) * LOG2E, jnp.log1p(-lb[1]) * LOG2E, 0.5 * (1.0 - lb[1]),
            jnp.zeros_like(lb[0]), jnp.zeros_like(lb[0])]
    return jnp.stack(rows, axis=1)


def kernel(x_prompt, x_sample, cache_attn_k, cache_attn_v, state_hgrn, c, c_ctx, w_ada, b_ada, norm_g, w_in,
           a_spatial_w, a_spatial_b, nb_rpb, hgrn_lb_logits, w_up_a, w_up_b, w_up_c, w_out, w_ffn_in, w_ffn_out):
    x = jnp.concatenate([x_prompt.reshape(N_CTX, D_MODEL), x_sample.reshape(N_LAT, D_MODEL)], 0)
    cond = jnp.zeros((COND_ROWS, D_MODEL), F32).at[0].set(c_ctx).at[1:N_COND].set(c)
    mod = ada_all_layers(cond, w_ada, b_ada)
    mod = mod.reshape(DEPTH, COND_ROWS, 6, 1, D_MODEL).transpose(0, 2, 1, 3, 4)
    lbp = hgrn_gate_params(hgrn_lb_logits)
    cos, sin_signed = rope_tables()
    cache_k = cache_attn_k.reshape(DEC_BATCH, DEPTH, PAST_LEN, WIDTH)
    cache_v = cache_attn_v.reshape(DEC_BATCH, DEPTH, PAST_LEN, WIDTH)
    gain = lambda l, i: norm_g[l, i][None, :]
    w_out_b = w_out.astype(BF16)
    w_ffn_out_b = w_ffn_out.astype(BF16)
    w_up_a, w_up_b, w_up_c = (w.astype(BF16) for w in (w_up_a, w_up_b, w_up_c))

    kv_all, s_all = None, None
    h = prenorm(x, gain(0, 0), mod[0, 1], mod[0, 0])
    for l in range(DEPTH):
        sh1, sc1, g1, sh2, sc2, g2 = (mod[l, i] for i in range(6))
        z = matmul_layer(h, w_in, l, tm=1024, tn=1024)
        y_a = mix_a(z, a_spatial_w[l], a_spatial_b[l])
        y_b, *kv_all = ctx_attention(z, l, kv_all)
        y_b = nbhd_attention(z, cache_k, cache_v, nbhd_pair_table(nb_rpb[l]), cos, sin_signed, y_b, l)
        y_c, s_all = hgrn(z, lbp[l], l, s_prev=s_all)
        y_c = hgrn(z, lbp[l], l, latent=(state_hgrn, y_c))
        merged = merge_branches(y_a, y_b, y_c, z, w_up_a, w_up_b, w_up_c, l)
        x, h2 = proj_resnorm(merged, w_out_b, l, x, gain(l, 1), g1, nxt=(gain(l, 2), sc2, sh2))
        act = swiglu_in(h2, w_ffn_in, l)
        nxt = None if l == DEPTH - 1 else (gain(l + 1, 0), mod[l + 1, 1], mod[l + 1, 0])
        x, h = proj_resnorm(act, w_ffn_out_b, l, x, gain(l, 3), g2, nxt=nxt, tk=512)

    y_prompt = x[:N_CTX].reshape(BATCH, SEQ, D_MODEL)
    y_sample = x[N_CTX:].reshape(DEC_BATCH, DEC_SEQ, D_MODEL)
    kv_shape = (BATCH, DEPTH, SEQ, HEADS, HEAD_DIM)
    return (y_prompt, y_sample, kv_all[0].reshape(kv_shape), kv_all[1].reshape(kv_shape), s_all)
```

```python
import functools
import math

import jax
import jax.numpy as jnp
from jax import lax
from jax.experimental import pallas as pl
from jax.experimental.pallas import tpu as pltpu

F32 = jnp.float32
BF16 = jnp.bfloat16

D_MODEL = 2048
BATCH = 32
SEQ = 256
DEPTH = 4
DEC_BATCH = 2
DEC_SEQ = 1024
PAST_LEN = 512
GRID_W = 64
HEADS = 8
HEAD_DIM = 128
WIDTH = HEADS * HEAD_DIM
A_CHUNK = 128
NB_ROWS_MAX = 8
NB_COLS = 16
ROPE_THETA = 10000.0
D_FF = -(-8 * D_MODEL // (3 * 256)) * 256
NORM_EPS = 1e-6
NEG_INF = -1e30
IN_WIDTH = 10 * WIDTH + 3 * D_MODEL

N_CTX = BATCH * SEQ
N_LAT = DEC_BATCH * DEC_SEQ
N_TOK = N_CTX + N_LAT
N_COND = 1 + DEC_BATCH
COND_ROWS = 8

COL_AU, COL_AV, COL_BQ, COL_BK, COL_BV, COL_CQ, COL_CFF, COL_CFB, COL_CI, COL_CG = range(10)
COL_GATES = 10 * WIDTH

HGRN_CHUNK = 128
HGRN_BAND = 4
HGRN_LEVELS = (8, 16, 32, 64, 128)
SUBLANES = 8
HGRN_CHUNKS_PER_TRIP = 2

LOG2E = math.log2(math.e)

VMEM_LIMIT = 48 * 1024 * 1024
VMEM_LIMIT_BIG = 56 * 1024 * 1024


def _cparams(n_axes, vmem=VMEM_LIMIT):
    return pltpu.CompilerParams(dimension_semantics=("arbitrary",) * n_axes, vmem_limit_bytes=vmem)


def _cond_row(i, tm):
    n_ctx_tiles = N_CTX // tm
    return jnp.where(i < n_ctx_tiles, 0, 1 + (i - n_ctx_tiles) // (DEC_SEQ // tm))


def _silu(x):
    return x * jax.nn.sigmoid(x)


def _sigmoid(x):
    return 0.5 * (1.0 + jnp.tanh(0.5 * x))


def _silu_tanh(x):
    hx = 0.5 * x
    return hx + hx * jnp.tanh(hx)


def _gelu_tanh(x):
    c = math.sqrt(2.0 / math.pi)
    return x * (0.5 * (1.0 + jnp.tanh(c * (x + 0.044715 * (x * x * x)))))


def _rms(x):
    return x * lax.rsqrt(jnp.mean(x * x, -1, keepdims=True) + NORM_EPS)


def _ada_kernel(c_ref, w_ref, b_ref, o_ref):
    s = _silu(c_ref[...]).astype(BF16)
    o_ref[...] = jnp.dot(s, w_ref[...].astype(BF16), preferred_element_type=F32) + b_ref[...]


def ada_all_layers(cond, w_ada, b_ada):
    tn = 1024
    n_out = 6 * D_MODEL
    return pl.pallas_call(
        _ada_kernel,
        grid=(DEPTH, n_out // tn),
        in_specs=[pl.BlockSpec((COND_ROWS, D_MODEL), lambda l, j: (0, 0)),
                  pl.BlockSpec((None, D_MODEL, tn), lambda l, j: (l, 0, j)),
                  pl.BlockSpec((None, 1, tn), lambda l, j: (l, 0, j))],
        out_specs=pl.BlockSpec((None, COND_ROWS, tn), lambda l, j: (l, 0, j)),
        out_shape=jax.ShapeDtypeStruct((DEPTH, COND_ROWS, n_out), F32),
        compiler_params=_cparams(2),
        name="ada",
    )(cond, w_ada, b_ada.reshape(DEPTH, 1, n_out))


def _prenorm_kernel(x_ref, g_ref, sc_ref, sh_ref, h_ref):
    y = _rms(x_ref[...]) * g_ref[...]
    h_ref[...] = (y * (1.0 + sc_ref[...]) + sh_ref[...]).astype(BF16)


def prenorm(x, gain, sc, sh):
    tm = 512
    row = lambda i: (_cond_row(i, tm), 0, 0)
    return pl.pallas_call(
        _prenorm_kernel,
        grid=(N_TOK // tm,),
        in_specs=[pl.BlockSpec((tm, D_MODEL), lambda i: (i, 0)),
                  pl.BlockSpec((1, D_MODEL), lambda i: (0, 0)),
                  pl.BlockSpec((None, 1, D_MODEL), row),
                  pl.BlockSpec((None, 1, D_MODEL), row)],
        out_specs=pl.BlockSpec((tm, D_MODEL), lambda i: (i, 0)),
        out_shape=jax.ShapeDtypeStruct((N_TOK, D_MODEL), BF16),
        compiler_params=_cparams(1),
        name="prenorm",
    )(x, gain, sc, sh)


def _proj_resnorm_kernel(a_ref, w_ref, x_ref, g_ref, gate_ref, *rest, nk, with_next):
    if with_next:
        g2_ref, sc_ref, sh_ref, xo_ref, h_ref = rest[:5]
    else:
        xo_ref = rest[0]
    part = jnp.dot(a_ref[...], w_ref[...], preferred_element_type=F32)

    def finish(t):
        x_new = x_ref[...] + gate_ref[...] * (_rms(t) * g_ref[...])
        xo_ref[...] = x_new
        if with_next:
            h_ref[...] = (_rms(x_new) * g2_ref[...] * (1.0 + sc_ref[...]) + sh_ref[...]).astype(BF16)

    if nk == 1:
        finish(part)
        return
    acc_ref = rest[-1]
    k = pl.program_id(1)

    @pl.when(k == 0)
    def _():
        acc_ref[...] = part

    @pl.when((k > 0) & (k < nk - 1))
    def _():
        acc_ref[...] += part

    @pl.when(k == nk - 1)
    def _():
        finish(acc_ref[...] + part)


def proj_resnorm(a, w, l, x, gain, gate, nxt=None, tk=None):
    tm = 512
    tk = tk or a.shape[1]
    nk = a.shape[1] // tk
    row = lambda i, k: (_cond_row(i, tm), 0, 0)
    tile = pl.BlockSpec((tm, D_MODEL), lambda i, k: (i, 0))
    vec = pl.BlockSpec((1, D_MODEL), lambda i, k: (0, 0))
    mod = pl.BlockSpec((None, 1, D_MODEL), row)
    w_mode = dict(pipeline_mode=pl.Buffered(1)) if nk == 1 else {}
    in_specs = [pl.BlockSpec((tm, tk), lambda i, k: (i, k)),
                pl.BlockSpec((None, tk, D_MODEL), lambda i, k: (l, k, 0), **w_mode), tile, vec, mod]
    args = [a, w, x, gain, gate]
    out_specs = [tile]
    out_shape = [jax.ShapeDtypeStruct((N_TOK, D_MODEL), F32)]
    if nxt is not None:
        in_specs += [vec, mod, mod]
        args += list(nxt)
        out_specs.append(tile)
        out_shape.append(jax.ShapeDtypeStruct((N_TOK, D_MODEL), BF16))
    out = pl.pallas_call(
        functools.partial(_proj_resnorm_kernel, nk=nk, with_next=nxt is not None),
        grid=(N_TOK // tm, nk),
        in_specs=in_specs, out_specs=out_specs, out_shape=out_shape,
        scratch_shapes=[pltpu.VMEM((tm, D_MODEL), F32)] if nk > 1 else [],
        compiler_params=_cparams(2, VMEM_LIMIT_BIG),
        name="proj_resnorm",
    )(*args)
    return out if nxt is not None else (out[0], None)


def _mm_kernel(a_ref, w_ref, o_ref, wb_ref):
    @pl.when(pl.program_id(1) == 0)
    def _():
        wb_ref[...] = w_ref[...].astype(BF16)

    o_ref[...] = jnp.dot(a_ref[...], wb_ref[...], preferred_element_type=F32)


def matmul_layer(a, w, l, tm, tn, vmem=VMEM_LIMIT):
    m, k = a.shape
    n = w.shape[2]
    return pl.pallas_call(
        _mm_kernel,
        grid=(n // tn, m // tm),
        in_specs=[pl.BlockSpec((tm, k), lambda j, i: (i, 0)),
                  pl.BlockSpec((None, k, tn), lambda j, i: (l, 0, j))],
        out_specs=pl.BlockSpec((tm, tn), lambda j, i: (i, j)),
        out_shape=jax.ShapeDtypeStruct((m, n), F32),
        scratch_shapes=[pltpu.VMEM((k, tn), BF16)],
        compiler_params=_cparams(2, vmem),
        name="matmul",
    )(a, w)


def _merge_kernel(ya_ref, ybc_ref, ybl_ref, ycc_ref, ycl_ref, wa_ref, wb_ref, wc_ref, ga_ref, gb_ref, gc_ref,
                  o_ref, *, n_ctx_tiles):
    def run(yb_ref, yc_ref):
        acc = _sigmoid(ga_ref[...]) * jnp.dot(ya_ref[...], wa_ref[...], preferred_element_type=F32)
        acc += _sigmoid(gb_ref[...]) * jnp.dot(yb_ref[...], wb_ref[...], preferred_element_type=F32)
        acc += _sigmoid(gc_ref[...]) * jnp.dot(yc_ref[...], wc_ref[...], preferred_element_type=F32)
        o_ref[...] = acc.astype(BF16)

    pl.when(pl.program_id(0) < n_ctx_tiles)(lambda: run(ybc_ref, ycc_ref))
    pl.when(pl.program_id(0) >= n_ctx_tiles)(lambda: run(ybl_ref, ycl_ref))


def merge_branches(y_a, y_b, y_c, z, w_up_a, w_up_b, w_up_c, l):
    tm, tn = 1024, 512
    gate0 = COL_GATES // tn
    per_gate = D_MODEL // tn
    n_ctx_tiles = N_CTX // tm
    y_spec = pl.BlockSpec((tm, WIDTH), lambda i, j: (i, 0))
    ctx_spec = pl.BlockSpec((tm, WIDTH), lambda i, j: (jnp.minimum(i, n_ctx_tiles - 1), 0))
    lat_spec = pl.BlockSpec((tm, WIDTH), lambda i, j: (jnp.maximum(i - n_ctx_tiles, 0), 0))
    w_spec = pl.BlockSpec((None, WIDTH, tn), lambda i, j: (l, 0, j))
    g_spec = lambda b: pl.BlockSpec((tm, tn), lambda i, j: (i, gate0 + b * per_gate + j))
    return pl.pallas_call(
        functools.partial(_merge_kernel, n_ctx_tiles=n_ctx_tiles),
        grid=(N_TOK // tm, D_MODEL // tn),
        in_specs=[y_spec, ctx_spec, lat_spec, ctx_spec, lat_spec, w_spec, w_spec, w_spec,
                  g_spec(0), g_spec(1), g_spec(2)],
        out_specs=pl.BlockSpec((tm, tn), lambda i, j: (i, j)),
        out_shape=jax.ShapeDtypeStruct((N_TOK, D_MODEL), BF16),
        compiler_params=_cparams(2),
        name="merge",
    )(y_a, y_b[0], y_b[1], y_c[0], y_c[1], w_up_a, w_up_b, w_up_c, z, z, z)


def _swiglu_kernel(h_ref, wu_ref, wg_ref, o_ref, su_ref, sg_ref):
    @pl.when(pl.program_id(1) == 0)
    def _():
        su_ref[...] = wu_ref[...].astype(BF16)
        sg_ref[...] = wg_ref[...].astype(BF16)

    h = h_ref[...]
    up = jnp.dot(h, su_ref[...], preferred_element_type=F32)
    gate = jnp.dot(h, sg_ref[...], preferred_element_type=F32)
    o_ref[...] = (_silu(gate) * up).astype(BF16)


def swiglu_in(h, w_ffn_in, l):
    tm, tn = 2048, 512
    n_up = D_FF // tn
    return pl.pallas_call(
        _swiglu_kernel,
        grid=(n_up, N_TOK // tm),
        in_specs=[pl.BlockSpec((tm, D_MODEL), lambda j, i: (i, 0)),
                  pl.BlockSpec((None, D_MODEL, tn), lambda j, i: (l, 0, j)),
                  pl.BlockSpec((None, D_MODEL, tn), lambda j, i: (l, 0, n_up + j))],
        out_specs=pl.BlockSpec((tm, tn), lambda j, i: (i, j)),
        out_shape=jax.ShapeDtypeStruct((N_TOK, D_FF), BF16),
        scratch_shapes=[pltpu.VMEM((D_MODEL, tn), BF16)] * 2,
        compiler_params=_cparams(2, VMEM_LIMIT_BIG),
        name="swiglu_in",
    )(h, w_ffn_in, w_ffn_in)


def _mix_a_kernel(u_ref, v_ref, w_ref, bt_ref, o_ref, *, rows):
    w = w_ref[...].astype(BF16)
    for c in range(rows // A_CHUNK):
        rs = slice(c * A_CHUNK, (c + 1) * A_CHUNK)
        u = _gelu_tanh(u_ref[rs, :])
        v = _gelu_tanh(v_ref[rs, :])
        mu = jnp.mean(v, -1, keepdims=True)
        vc = v - mu
        vn = (vc * lax.rsqrt(jnp.mean(vc * vc, -1, keepdims=True) + NORM_EPS)).astype(BF16)
        for g in range(HEADS):
            cs = slice(g * HEAD_DIM, (g + 1) * HEAD_DIM)
            zg = jnp.dot(w[g], vn[:, cs], preferred_element_type=F32) + bt_ref[:, g:g + 1]
            o_ref[rs, cs] = (u[:, cs] * zg).astype(BF16)


def mix_a(z, a_w_l, a_b_l):
    rows = 512
    return pl.pallas_call(
        functools.partial(_mix_a_kernel, rows=rows),
        grid=(N_TOK // rows,),
        in_specs=[pl.BlockSpec((rows, WIDTH), lambda i: (i, COL_AU)),
                  pl.BlockSpec((rows, WIDTH), lambda i: (i, COL_AV)),
                  pl.BlockSpec((HEADS, A_CHUNK, A_CHUNK), lambda i: (0, 0, 0)),
                  pl.BlockSpec((A_CHUNK, HEADS), lambda i: (0, 0))],
        out_specs=pl.BlockSpec((rows, WIDTH), lambda i: (i, 0)),
        out_shape=jax.ShapeDtypeStruct((N_TOK, WIDTH), BF16),
        compiler_params=_cparams(1),
        name="mix_a",
    )(z, z, a_w_l, a_b_l.T)


def _softmax_pv(scores, values):
    m = scores[0].max(-1, keepdims=True)
    for s in scores[1:]:
        m = jnp.maximum(m, s.max(-1, keepdims=True))
    den = 0.0
    acc = 0.0
    for s, v in zip(scores, values):
        p = jnp.exp(s - m)
        den = den + p.sum(-1, keepdims=True)
        acc = acc + jnp.dot(p.astype(BF16), v, preferred_element_type=F32)
    return acc / den


def _qkt(q, k):
    return lax.dot_general(q, k, (((1,), (1,)), ((), ())), preferred_element_type=F32)


def _ctx_attn_kernel(q_ref, k_ref, v_ref, *rest):
    o_ref, ko_ref, vo_ref = rest[-3:]
    scale = HEAD_DIM ** -0.5
    ko_ref[...] = k_ref[...]
    vo_ref[...] = v_ref[...]
    for h in range(HEADS):
        cs = slice(h * HEAD_DIM, (h + 1) * HEAD_DIM)
        q = q_ref[:, cs].astype(BF16)
        k = k_ref[:, cs].astype(BF16)
        v = v_ref[:, cs].astype(BF16)
        o_ref[:, cs] = _softmax_pv([_qkt(q, k) * scale], [v]).astype(BF16)


def ctx_attention(z, l, k_all, v_all):
    spec = lambda col: pl.BlockSpec((SEQ, WIDTH), lambda b: (b, col))
    kv_spec = pl.BlockSpec((None, None, SEQ, WIDTH), lambda b: (b, l, 0, 0))
    kv_shape = jax.ShapeDtypeStruct((BATCH, DEPTH, SEQ, WIDTH), F32)
    raw = pl.BlockSpec(memory_space=pl.ANY)
    return pl.pallas_call(
        _ctx_attn_kernel,
        grid=(BATCH,),
        in_specs=[spec(COL_BQ), spec(COL_BK), spec(COL_BV), raw, raw],
        out_specs=[pl.BlockSpec((SEQ, WIDTH), lambda b: (b, 0)), kv_spec, kv_spec],
        out_shape=[jax.ShapeDtypeStruct((N_CTX, WIDTH), BF16), kv_shape, kv_shape],
        input_output_aliases={3: 1, 4: 2},
        compiler_params=_cparams(1),
        name="ctx_attn",
    )(z, z, z, k_all, v_all)


def _rope(x, cos, sin_signed, first_quarter):
    partner = jnp.where(first_quarter, pltpu.roll(x, HEAD_DIM - 32, 1), pltpu.roll(x, 32, 1))
    return x * cos + partner * sin_signed


NB_GRID_ROWS = DEC_SEQ // GRID_W
NB_WIN_ROWS = min(NB_ROWS_MAX, NB_GRID_ROWS)
NB_Q_ROWS = 4


def _nb_row_start(qr):
    return min(max(qr - NB_WIN_ROWS // 2, 0), NB_GRID_ROWS - NB_WIN_ROWS)


def _nb_tiles():
    tiles = []
    for q0 in range(0, NB_GRID_ROWS, NB_Q_ROWS):
        starts = [_nb_row_start(qr) for qr in range(q0, q0 + NB_Q_ROWS)]
        lo, hi = min(starts), max(starts) + NB_WIN_ROWS
        tiles.append((q0, lo - lo % 2, hi + hi % 2))
    return tiles


def _nbhd_attn_kernel(q_ref, k_ref, v_ref, kc_ref, vc_ref, pair_ref, cos_ref, sin_ref, o_ref, bias_scr):
    scale = HEAD_DIM ** -0.5
    lane = lax.broadcasted_iota(jnp.int32, (1, HEAD_DIM), 1)
    first_quarter = (lane % 64) < 32

    @pl.when(pl.program_id(1) == 0)
    def _():
        left = lane < GRID_W
        neg = jnp.full((GRID_W, 2 * GRID_W), NEG_INF, F32)
        for q0, lo, hi in _nb_tiles():
            for qr in range(q0, q0 + NB_Q_ROWS):
                r0 = _nb_row_start(qr)
                for p in range(lo // 2, hi // 2):
                    ok0 = r0 <= 2 * p < r0 + NB_WIN_ROWS
                    ok1 = r0 <= 2 * p + 1 < r0 + NB_WIN_ROWS
                    tile = neg
                    if ok0 or ok1:
                        tile = pair_ref[2 * p - qr + NB_ROWS_MAX]
                        if not ok0:
                            tile = jnp.where(left, NEG_INF, tile)
                        if not ok1:
                            tile = jnp.where(left, tile, NEG_INF)
                    bias_scr[qr * GRID_W:(qr + 1) * GRID_W, p * 2 * GRID_W:(p + 1) * 2 * GRID_W] = tile

    k = _rope(k_ref[...], cos_ref[...], sin_ref[...], first_quarter).astype(BF16)
    v = v_ref[...].astype(BF16)
    kc = kc_ref[...].astype(BF16)
    vc = vc_ref[...].astype(BF16)
    for q0, lo, hi in _nb_tiles():
        rs = slice(q0 * GRID_W, (q0 + NB_Q_ROWS) * GRID_W)
        ks = slice(lo * GRID_W, hi * GRID_W)
        q = _rope(q_ref[rs, :], cos_ref[rs, :], sin_ref[rs, :], first_quarter).astype(BF16)
        s_loc = _qkt(q, k[ks]) * scale + bias_scr[rs, ks]
        s_ctx = _qkt(q, kc) * scale
        o_ref[rs, :] = _softmax_pv([s_loc, s_ctx], [v[ks], vc]).astype(BF16)


def nbhd_attention(z, cache_k, cache_v, pair_table, cos, sin_signed, l):
    lat0 = N_CTX // DEC_SEQ
    col = lambda c: (lambda h, b: (lat0 + b, c * HEADS + h))
    tok = lambda c: pl.BlockSpec((DEC_SEQ, HEAD_DIM), col(c))
    cache = pl.BlockSpec((None, None, PAST_LEN, HEAD_DIM), lambda h, b: (b, l, 0, h))
    table = pl.BlockSpec((DEC_SEQ, HEAD_DIM), lambda h, b: (0, 0))
    pairs = pl.BlockSpec((None, 2 * NB_ROWS_MAX, GRID_W, 2 * GRID_W), lambda h, b: (h, 0, 0, 0))
    return pl.pallas_call(
        _nbhd_attn_kernel,
        grid=(HEADS, DEC_BATCH),
        in_specs=[tok(COL_BQ), tok(COL_BK), tok(COL_BV), cache, cache, pairs, table, table],
        out_specs=pl.BlockSpec((DEC_SEQ, HEAD_DIM), lambda h, b: (b, h)),
        out_shape=jax.ShapeDtypeStruct((N_LAT, WIDTH), BF16),
        scratch_shapes=[pltpu.VMEM((DEC_SEQ, DEC_SEQ), F32)],
        compiler_params=_cparams(2),
        name="nbhd_attn",
    )(z, z, z, cache_k, cache_v, pair_table, cos, sin_signed)


def nbhd_pair_table(rpb_l):
    cols = jnp.arange(GRID_W)
    c_start = jnp.clip(cols - NB_COLS // 2, 0, GRID_W - NB_COLS)
    col_ok = (cols[None, :] >= c_start[:, None]) & (cols[None, :] < c_start[:, None] + NB_COLS)
    dc_idx = jnp.clip(cols[None, :] - cols[:, None] + NB_COLS - 1, 0, 2 * NB_COLS - 2)
    t = jnp.where(col_ok[None, None], rpb_l[:, :, dc_idx], NEG_INF).astype(F32)
    neg = jnp.full((HEADS, 1, GRID_W, GRID_W), NEG_INF, F32)
    tp = jnp.concatenate([neg, t, neg], 1)
    return jnp.concatenate([tp[:, :-1], tp[:, 1:]], -1)


def rope_tables():
    ax = HEAD_DIM // 2
    pos = jnp.arange(DEC_SEQ)
    inv = ROPE_THETA ** (-jnp.arange(0, ax, 2, dtype=F32) / ax)

    def ang(p):
        a = p.astype(F32)[:, None] * inv[None, :]
        return jnp.concatenate([a, a], -1)

    a = jnp.concatenate([ang(pos // GRID_W), ang(pos % GRID_W)], -1)
    sign = jnp.where((jnp.arange(HEAD_DIM) % ax) < ax // 2, -1.0, 1.0).astype(F32)
    return jnp.cos(a), jnp.sin(a) * sign[None, :]


def _split3(x):
    hi = x.astype(BF16)
    r = x - hi.astype(F32)
    mid = r.astype(BF16)
    lo = (r - mid.astype(F32)).astype(BF16)
    return hi, mid, lo


def _hgrn_chunks(chains, st, tri_ref, pairs_ref, side_ref):
    c = HGRN_CHUNK
    st = list(st)
    qs, kk, ff, bb, vb = [], [], [], [], []
    for q, zf, v, (log2lb, log21mlb, half1mlb), d, b_scr in chains:
        z2 = zf * LOG2E
        ls = jnp.minimum(z2, 0.0) - jnp.log2(1.0 + jnp.exp2(-jnp.abs(z2)))
        x2 = log21mlb + ls
        lf = jnp.maximum(log2lb, x2) + jnp.log2(1.0 + jnp.exp2(-jnp.abs(log2lb - x2)))
        k = half1mlb * (1.0 - jnp.tanh(0.5 * zf))
        b = sum(jnp.dot(tri_ref[d], p, preferred_element_type=F32) for p in _split3(lf))
        b_scr[...] = b
        qs.append(_silu_tanh(q))
        kk.append(k)
        ff.append(1.0 - k)
        bb.append(b)
        vb.append(v.astype(BF16))

    outs = []
    for i, (_, _, _, _, d, b_scr) in enumerate(chains):
        end = 0 if d else c - 1
        b_end = b_scr[end:end + 1, :]
        outs.append(_qkt((qs[i] * jnp.exp2(bb[i])).astype(BF16), st[d].astype(BF16)))
        k_dec = (kk[i] * jnp.exp2(b_end - bb[i])).astype(BF16)
        st[d] = st[d] * jnp.exp2(b_end) + lax.dot_general(vb[i], k_dec, (((0,), (0,)), ((), ())),
                                                          preferred_element_type=F32)

    attn = []
    for i, (_, _, _, _, d, _) in enumerate(chains):
        a = jnp.zeros((c, c), F32)
        kd = kk[i]
        for delta in range(HGRN_BAND):
            if delta:
                kd3 = kd.reshape(c // SUBLANES, SUBLANES, HEAD_DIM)
                kd = ff[i] * pltpu.roll(kd3, SUBLANES - 1 if d else 1, 1).reshape(c, HEAD_DIM)
            a = jnp.where(pairs_ref[d] == delta, (qs[i] * kd).sum(-1, keepdims=True), a)
        attn.append(a)

    qb = [x.astype(BF16) for x in qs]
    kb = [x.astype(BF16) for x in kk]
    for lv, m in enumerate(HGRN_LEVELS):
        for i, (_, _, _, _, d, b_scr) in enumerate(chains):
            seam = m // 2 if d else m // 2 - 1
            g = jnp.concatenate([jnp.broadcast_to(b_scr[j * m + seam:j * m + seam + 1, :], (m, HEAD_DIM))
                                 for j in range(c // m)], 0)
            e = jnp.exp2((bb[i] - g) * side_ref[d, lv]).astype(BF16)
            attn[i] = jnp.where(pairs_ref[d] == m, _qkt(qb[i] * e, kb[i] * e), attn[i])
    for i in range(len(chains)):
        outs[i] += jnp.dot(attn[i].astype(BF16), vb[i], preferred_element_type=F32)
    return outs, st


def _hgrn_kernel(q_ref, ff_ref, fb_ref, v_ref, g_ref, lbp_ref, tri_ref, pairs_ref, side_ref, *rest, n, latent):
    of_scr, ob_scr, b_scr = rest[-3:]
    if latent:
        s0_ref, y_ref = rest[:2]
    else:
        y_ref, s_ref = rest[1:3]
    zero_init = not latent
    c = HGRN_CHUNK
    nc = n // c
    gate_f = [lbp_ref[r:r + 1, :] for r in range(3)]
    gate_b = [lbp_ref[3 + r:4 + r, :] for r in range(3)]

    def body(i, st):
        chains, dst = [], []
        for u in range(HGRN_CHUNKS_PER_TRIP):
            cf = i * HGRN_CHUNKS_PER_TRIP + u
            rf = pl.ds(pl.multiple_of(cf * c, c), c)
            rb = pl.ds(pl.multiple_of((nc - 1 - cf) * c, c), c)
            chains.append((q_ref[rf, :], ff_ref[rf, :], v_ref[rf, :], gate_f, 0, b_scr.at[2 * u]))
            chains.append((q_ref[rb, :], fb_ref[rb, :], v_ref[rb, :], gate_b, 1, b_scr.at[2 * u + 1]))
            dst += [(of_scr, rf), (ob_scr, rb)]
        outs, st = _hgrn_chunks(chains, st, tri_ref, pairs_ref, side_ref)
        for (scr, rows), o in zip(dst, outs):
            scr[rows, :] = o
        return tuple(st)

    if zero_init:
        st0 = (jnp.zeros((HEAD_DIM, HEAD_DIM), F32),) * 2
    else:
        st0 = (s0_ref[0].T, s0_ref[1].T)
    st_f, st_b = lax.fori_loop(0, nc // HGRN_CHUNKS_PER_TRIP, body, st0)
    if not latent:
        s_ref[0] = st_f.T
        s_ref[1] = st_b.T

    def finish(i, carry):
        rs = pl.ds(pl.multiple_of(i * c, c), c)
        y_ref[rs, :] = (_rms(of_scr[rs, :] + ob_scr[rs, :]) * _silu_tanh(g_ref[rs, :])).astype(BF16)
        return carry

    lax.fori_loop(0, nc, finish, 0)


def _hgrn_constants():
    c = HGRN_CHUNK
    t = jnp.arange(c)[:, None]
    s = jnp.arange(c)[None, :]
    tri_f = (s <= t)
    pairs_f = jnp.where((t // HGRN_BAND == s // HGRN_BAND) & (s <= t), t - s, -1).astype(jnp.int32)
    for m in HGRN_LEVELS:
        h = m // 2
        pairs_f = jnp.where((t // h == s // h + 1) & ((t // h) % 2 == 1), m, pairs_f)
    tri = jnp.stack([tri_f, tri_f.T]).astype(BF16)
    pairs = jnp.stack([pairs_f, pairs_f.T])
    later_half = jnp.stack([(jnp.arange(c) % m) >= m // 2 for m in HGRN_LEVELS])
    side_f = jnp.broadcast_to(jnp.where(later_half, 1.0, -1.0)[:, :, None], (len(HGRN_LEVELS), c, HEAD_DIM))
    side = jnp.stack([side_f, -side_f]).astype(F32)
    return tri, pairs, side


def hgrn(z, lbp, l, s_all=None, s0=None):
    latent = s0 is not None
    n = DEC_SEQ if latent else SEQ
    bsz = DEC_BATCH if latent else BATCH
    blk0 = N_CTX // DEC_SEQ if latent else 0
    tri, pairs, side = _hgrn_constants()
    tok = lambda c: pl.BlockSpec((n, HEAD_DIM), lambda b, h: (blk0 + b, c * HEADS + h))
    raw = pl.BlockSpec(memory_space=pl.ANY)
    in_specs = [tok(COL_CQ), tok(COL_CFF), tok(COL_CFB), tok(COL_CI), tok(COL_CG),
                pl.BlockSpec((8, HEAD_DIM), lambda b, h: (0, h)),
                pl.BlockSpec((2, HGRN_CHUNK, HGRN_CHUNK), lambda b, h: (0, 0, 0)),
                pl.BlockSpec((2, HGRN_CHUNK, HGRN_CHUNK), lambda b, h: (0, 0, 0)),
                pl.BlockSpec(side.shape, lambda b, h: (0, 0, 0, 0))]
    args = [z, z, z, z, z, lbp, tri, pairs, side]
    y_spec = pl.BlockSpec((n, HEAD_DIM), lambda b, h: (b, h))
    y_shape = jax.ShapeDtypeStruct((bsz * n, WIDTH), BF16)
    state = pl.BlockSpec((None, None, 2, None, HEAD_DIM, HEAD_DIM), lambda b, h: (b, l, 0, h, 0, 0))
    if latent:
        in_specs.append(state)
        args.append(s0)
        out_specs, out_shape, aliases = y_spec, y_shape, {}
    else:
        aliases = {len(args): 1}
        in_specs.append(raw)
        args.append(s_all)
        out_specs = [y_spec, state]
        out_shape = [y_shape, jax.ShapeDtypeStruct(s_all.shape, F32)]
    return pl.pallas_call(
        functools.partial(_hgrn_kernel, n=n, latent=latent),
        grid=(bsz, HEADS),
        in_specs=in_specs, out_specs=out_specs, out_shape=out_shape,
        scratch_shapes=[pltpu.VMEM((n, HEAD_DIM), F32)] * 2
        + [pltpu.VMEM((2 * HGRN_CHUNKS_PER_TRIP, HGRN_CHUNK, HEAD_DIM), F32)],
        input_output_aliases=aliases,
        compiler_params=_cparams(2),
        name="hgrn",
    )(*args)


def hgrn_gate_params(lb_logits):
    p = jax.nn.softmax(lb_logits.astype(F32), axis=1)
    cs = jnp.cumsum(p, axis=1)
    lb = cs - cs[:, :1]
    rows = [jnp.log(lb[0]) * LOG2E, jnp.log1p(-lb[0]) * LOG2E, 0.5 * (1.0 - lb[0]),
            jnp.log(lb[1]) * LOG2E, jnp.log1p(-lb[1]) * LOG2E, 0.5 * (1.0 - lb[1]),
            jnp.zeros_like(lb[0]), jnp.zeros_like(lb[0])]
    return jnp.stack(rows, axis=1)


def kernel(x_prompt, x_sample, cache_attn_k, cache_attn_v, state_hgrn, c, c_ctx, w_ada, b_ada, norm_g, w_in,
           a_spatial_w, a_spatial_b, nb_rpb, hgrn_lb_logits, w_up_a, w_up_b, w_up_c, w_out, w_ffn_in, w_ffn_out):
    x = jnp.concatenate([x_prompt.reshape(N_CTX, D_MODEL), x_sample.reshape(N_LAT, D_MODEL)], 0)
    cond = jnp.zeros((COND_ROWS, D_MODEL), F32).at[0].set(c_ctx).at[1:N_COND].set(c)
    mod = ada_all_layers(cond, w_ada, b_ada)
    mod = mod.reshape(DEPTH, COND_ROWS, 6, 1, D_MODEL).transpose(0, 2, 1, 3, 4)
    lbp = hgrn_gate_params(hgrn_lb_logits)
    cos, sin_signed = rope_tables()
    cache_k = cache_attn_k.reshape(DEC_BATCH, DEPTH, PAST_LEN, WIDTH)
    cache_v = cache_attn_v.reshape(DEC_BATCH, DEPTH, PAST_LEN, WIDTH)
    gain = lambda l, i: norm_g[l, i][None, :]
    w_out_b = w_out.astype(BF16)
    w_ffn_out_b = w_ffn_out.astype(BF16)
    w_up_a, w_up_b, w_up_c = (w.astype(BF16) for w in (w_up_a, w_up_b, w_up_c))

    k_all = jnp.zeros((BATCH, DEPTH, SEQ, WIDTH), F32)
    v_all = jnp.zeros((BATCH, DEPTH, SEQ, WIDTH), F32)
    s_all = jnp.zeros((BATCH, DEPTH, 2, HEADS, HEAD_DIM, HEAD_DIM), F32)
    h = prenorm(x, gain(0, 0), mod[0, 1], mod[0, 0])
    for l in range(DEPTH):
        sh1, sc1, g1, sh2, sc2, g2 = (mod[l, i] for i in range(6))
        z = matmul_layer(h, w_in, l, tm=1024, tn=1024)
        y_a = mix_a(z, a_spatial_w[l], a_spatial_b[l])
        yb_ctx, k_all, v_all = ctx_attention(z, l, k_all, v_all)
        yb_lat = nbhd_attention(z, cache_k, cache_v, nbhd_pair_table(nb_rpb[l]), cos, sin_signed, l)
        yc_ctx, s_all = hgrn(z, lbp[l], l, s_all=s_all)
        yc_lat = hgrn(z, lbp[l], l, s0=state_hgrn)
        merged = merge_branches(y_a, (yb_ctx, yb_lat), (yc_ctx, yc_lat), z, w_up_a, w_up_b, w_up_c, l)
        x, h2 = proj_resnorm(merged, w_out_b, l, x, gain(l, 1), g1, nxt=(gain(l, 2), sc2, sh2))
        act = swiglu_in(h2, w_ffn_in, l)
        nxt = None if l == DEPTH - 1 else (gain(l + 1, 0), mod[l + 1, 1], mod[l + 1, 0])
        x, h = proj_resnorm(act, w_ffn_out_b, l, x, gain(l, 3), g2, nxt=nxt, tk=D_FF // 4)

    y_prompt = x[:N_CTX].reshape(BATCH, SEQ, D_MODEL)
    y_sample = x[N_CTX:].reshape(DEC_BATCH, DEC_SEQ, D_MODEL)
    kv_shape = (BATCH, DEPTH, SEQ, HEADS, HEAD_DIM)
    return (y_prompt, y_sample, k_all.reshape(kv_shape), v_all.reshape(kv_shape), s_all)
```

```python
import functools
import math

import jax
import jax.numpy as jnp
from jax import lax
from jax.experimental import pallas as pl
from jax.experimental.pallas import tpu as pltpu

F32 = jnp.float32
BF16 = jnp.bfloat16

D_MODEL = 2048
BATCH = 32
SEQ = 256
DEPTH = 4
DEC_BATCH = 2
DEC_SEQ = 1024
PAST_LEN = 512
GRID_W = 64
HEADS = 8
HEAD_DIM = 128
WIDTH = HEADS * HEAD_DIM
A_CHUNK = 128
NB_ROWS_MAX = 8
NB_COLS = 16
ROPE_THETA = 10000.0
D_FF = -(-8 * D_MODEL // (3 * 256)) * 256
NORM_EPS = 1e-6
NEG_INF = -1e30
IN_WIDTH = 10 * WIDTH + 3 * D_MODEL

N_CTX = BATCH * SEQ
N_LAT = DEC_BATCH * DEC_SEQ
N_TOK = N_CTX + N_LAT
N_COND = 1 + DEC_BATCH
COND_ROWS = 8

COL_AU, COL_AV, COL_BQ, COL_BK, COL_BV, COL_CQ, COL_CFF, COL_CFB, COL_CI, COL_CG = range(10)
COL_GATES = 10 * WIDTH

HGRN_CHUNK = 128
HGRN_BAND = 4
HGRN_LEVELS = (8, 16, 32, 64, 128)
SUBLANES = 8
HGRN_CHUNKS_PER_TRIP = 2
HGRN_CTX_HEADS_PER_STEP = 4
HGRN_LAT_HEADS_PER_STEP = 2

LOG2E = math.log2(math.e)

VMEM_LIMIT = 48 * 1024 * 1024
VMEM_LIMIT_BIG = 56 * 1024 * 1024


def _cparams(n_axes, vmem=VMEM_LIMIT):
    return pltpu.CompilerParams(dimension_semantics=("arbitrary",) * n_axes, vmem_limit_bytes=vmem)


def _cond_row(i, tm):
    n_ctx_tiles = N_CTX // tm
    return jnp.where(i < n_ctx_tiles, 0, 1 + (i - n_ctx_tiles) // (DEC_SEQ // tm))


def _silu(x):
    return x * jax.nn.sigmoid(x)


def _sigmoid(x):
    return 0.5 * (1.0 + jnp.tanh(0.5 * x))


def _silu_tanh(x):
    hx = 0.5 * x
    return hx + hx * jnp.tanh(hx)


def _gelu_tanh(x):
    c = math.sqrt(2.0 / math.pi)
    return x * (0.5 * (1.0 + jnp.tanh(c * (x + 0.044715 * (x * x * x)))))


def _rms(x):
    return x * lax.rsqrt(jnp.mean(x * x, -1, keepdims=True) + NORM_EPS)


def _ada_kernel(c_ref, w_ref, b_ref, o_ref):
    s = _silu(c_ref[...]).astype(BF16)
    o_ref[...] = jnp.dot(s, w_ref[...].astype(BF16), preferred_element_type=F32) + b_ref[...]


def ada_all_layers(cond, w_ada, b_ada):
    tn = 1024
    n_out = 6 * D_MODEL
    return pl.pallas_call(
        _ada_kernel,
        grid=(DEPTH, n_out // tn),
        in_specs=[pl.BlockSpec((COND_ROWS, D_MODEL), lambda l, j: (0, 0)),
                  pl.BlockSpec((None, D_MODEL, tn), lambda l, j: (l, 0, j)),
                  pl.BlockSpec((None, 1, tn), lambda l, j: (l, 0, j))],
        out_specs=pl.BlockSpec((None, COND_ROWS, tn), lambda l, j: (l, 0, j)),
        out_shape=jax.ShapeDtypeStruct((DEPTH, COND_ROWS, n_out), F32),
        compiler_params=_cparams(2),
        name="ada",
    )(cond, w_ada, b_ada.reshape(DEPTH, 1, n_out))


def _prenorm_kernel(x_ref, g_ref, sc_ref, sh_ref, h_ref):
    y = _rms(x_ref[...]) * g_ref[...]
    h_ref[...] = (y * (1.0 + sc_ref[...]) + sh_ref[...]).astype(BF16)


def prenorm(x, gain, sc, sh):
    tm = 512
    row = lambda i: (_cond_row(i, tm), 0, 0)
    return pl.pallas_call(
        _prenorm_kernel,
        grid=(N_TOK // tm,),
        in_specs=[pl.BlockSpec((tm, D_MODEL), lambda i: (i, 0)),
                  pl.BlockSpec((1, D_MODEL), lambda i: (0, 0)),
                  pl.BlockSpec((None, 1, D_MODEL), row),
                  pl.BlockSpec((None, 1, D_MODEL), row)],
        out_specs=pl.BlockSpec((tm, D_MODEL), lambda i: (i, 0)),
        out_shape=jax.ShapeDtypeStruct((N_TOK, D_MODEL), BF16),
        compiler_params=_cparams(1),
        name="prenorm",
    )(x, gain, sc, sh)


def _proj_resnorm_kernel(a_ref, w_ref, x_ref, g_ref, gate_ref, *rest, nk, with_next):
    if with_next:
        g2_ref, sc_ref, sh_ref, xo_ref, h_ref = rest[:5]
    else:
        xo_ref = rest[0]
    def part():
        return jnp.dot(a_ref[...], w_ref[...], preferred_element_type=F32)

    def finish(t):
        x_new = x_ref[...] + gate_ref[...] * (_rms(t) * g_ref[...])
        xo_ref[...] = x_new
        if with_next:
            h_ref[...] = (_rms(x_new) * g2_ref[...] * (1.0 + sc_ref[...]) + sh_ref[...]).astype(BF16)

    if nk == 1:
        finish(part())
        return
    acc_ref = rest[-1]
    k = pl.program_id(1)

    @pl.when(k == 0)
    def _():
        acc_ref[...] = part()

    @pl.when(k > 0)
    def _():
        acc_ref[...] += part()

    @pl.when(k == nk - 1)
    def _():
        finish(acc_ref[...])


def proj_resnorm(a, w, l, x, gain, gate, nxt=None, tk=None):
    tm = 512
    tk = tk or a.shape[1]
    nk = a.shape[1] // tk
    row = lambda i, k: (_cond_row(i, tm), 0, 0)
    tile = pl.BlockSpec((tm, D_MODEL), lambda i, k: (i, 0))
    vec = pl.BlockSpec((1, D_MODEL), lambda i, k: (0, 0))
    mod = pl.BlockSpec((None, 1, D_MODEL), row)
    w_mode = dict(pipeline_mode=pl.Buffered(1)) if nk == 1 else {}
    in_specs = [pl.BlockSpec((tm, tk), lambda i, k: (i, k)),
                pl.BlockSpec((None, tk, D_MODEL), lambda i, k: (l, k, 0), **w_mode), tile, vec, mod]
    args = [a, w, x, gain, gate]
    out_specs = [tile]
    out_shape = [jax.ShapeDtypeStruct((N_TOK, D_MODEL), F32)]
    if nxt is not None:
        in_specs += [vec, mod, mod]
        args += list(nxt)
        out_specs.append(tile)
        out_shape.append(jax.ShapeDtypeStruct((N_TOK, D_MODEL), BF16))
    out = pl.pallas_call(
        functools.partial(_proj_resnorm_kernel, nk=nk, with_next=nxt is not None),
        grid=(N_TOK // tm, nk),
        in_specs=in_specs, out_specs=out_specs, out_shape=out_shape,
        scratch_shapes=[pltpu.VMEM((tm, D_MODEL), F32)] if nk > 1 else [],
        compiler_params=_cparams(2, VMEM_LIMIT_BIG),
        name="proj_resnorm",
    )(*args)
    return out if nxt is not None else (out[0], None)


def _mm_kernel(a_ref, w_ref, o_ref, wb_ref):
    @pl.when(pl.program_id(1) == 0)
    def _():
        wb_ref[...] = w_ref[...].astype(BF16)

    o_ref[...] = jnp.dot(a_ref[...], wb_ref[...], preferred_element_type=F32)


def matmul_layer(a, w, l, tm, tn, vmem=VMEM_LIMIT):
    m, k = a.shape
    n = w.shape[2]
    return pl.pallas_call(
        _mm_kernel,
        grid=(n // tn, m // tm),
        in_specs=[pl.BlockSpec((tm, k), lambda j, i: (i, 0)),
                  pl.BlockSpec((None, k, tn), lambda j, i: (l, 0, j))],
        out_specs=pl.BlockSpec((tm, tn), lambda j, i: (i, j)),
        out_shape=jax.ShapeDtypeStruct((m, n), F32),
        scratch_shapes=[pltpu.VMEM((k, tn), BF16)],
        compiler_params=_cparams(2, vmem),
        name="matmul",
    )(a, w)


def _merge_kernel(ya_ref, ybc_ref, ybl_ref, ycc_ref, ycl_ref, wa_ref, wb_ref, wc_ref, ga_ref, gb_ref, gc_ref,
                  o_ref, *, n_ctx_tiles):
    def run(yb_ref, yc_ref):
        acc = _sigmoid(ga_ref[...]) * jnp.dot(ya_ref[...], wa_ref[...], preferred_element_type=F32)
        acc += _sigmoid(gb_ref[...]) * jnp.dot(yb_ref[...], wb_ref[...], preferred_element_type=F32)
        acc += _sigmoid(gc_ref[...]) * jnp.dot(yc_ref[...], wc_ref[...], preferred_element_type=F32)
        o_ref[...] = acc.astype(BF16)

    pl.when(pl.program_id(0) < n_ctx_tiles)(lambda: run(ybc_ref, ycc_ref))
    pl.when(pl.program_id(0) >= n_ctx_tiles)(lambda: run(ybl_ref, ycl_ref))


def merge_branches(y_a, y_b, y_c, z, w_up_a, w_up_b, w_up_c, l):
    tm, tn = 1024, 512
    gate0 = COL_GATES // tn
    per_gate = D_MODEL // tn
    n_ctx_tiles = N_CTX // tm
    y_spec = pl.BlockSpec((tm, WIDTH), lambda i, j: (i, 0))
    ctx_spec = pl.BlockSpec((tm, WIDTH), lambda i, j: (jnp.minimum(i, n_ctx_tiles - 1), 0))
    lat_spec = pl.BlockSpec((tm, WIDTH), lambda i, j: (jnp.maximum(i - n_ctx_tiles, 0), 0))
    w_spec = pl.BlockSpec((None, WIDTH, tn), lambda i, j: (l, 0, j))
    g_spec = lambda b: pl.BlockSpec((tm, tn), lambda i, j: (i, gate0 + b * per_gate + j))
    return pl.pallas_call(
        functools.partial(_merge_kernel, n_ctx_tiles=n_ctx_tiles),
        grid=(N_TOK // tm, D_MODEL // tn),
        in_specs=[y_spec, ctx_spec, lat_spec, ctx_spec, lat_spec, w_spec, w_spec, w_spec,
                  g_spec(0), g_spec(1), g_spec(2)],
        out_specs=pl.BlockSpec((tm, tn), lambda i, j: (i, j)),
        out_shape=jax.ShapeDtypeStruct((N_TOK, D_MODEL), BF16),
        compiler_params=_cparams(2),
        name="merge",
    )(y_a, y_b[0], y_b[1], y_c[0], y_c[1], w_up_a, w_up_b, w_up_c, z, z, z)


def _swiglu_kernel(h_ref, wu_ref, wg_ref, o_ref, su_ref, sg_ref):
    @pl.when(pl.program_id(1) == 0)
    def _():
        su_ref[...] = wu_ref[...].astype(BF16)
        sg_ref[...] = wg_ref[...].astype(BF16)

    h = h_ref[...]
    up = jnp.dot(h, su_ref[...], preferred_element_type=F32)
    gate = jnp.dot(h, sg_ref[...], preferred_element_type=F32)
    o_ref[...] = (_silu(gate) * up).astype(BF16)


def swiglu_in(h, w_ffn_in, l):
    tm, tn = 2048, 512
    n_up = D_FF // tn
    return pl.pallas_call(
        _swiglu_kernel,
        grid=(n_up, N_TOK // tm),
        in_specs=[pl.BlockSpec((tm, D_MODEL), lambda j, i: (i, 0)),
                  pl.BlockSpec((None, D_MODEL, tn), lambda j, i: (l, 0, j)),
                  pl.BlockSpec((None, D_MODEL, tn), lambda j, i: (l, 0, n_up + j))],
        out_specs=pl.BlockSpec((tm, tn), lambda j, i: (i, j)),
        out_shape=jax.ShapeDtypeStruct((N_TOK, D_FF), BF16),
        scratch_shapes=[pltpu.VMEM((D_MODEL, tn), BF16)] * 2,
        compiler_params=_cparams(2, VMEM_LIMIT_BIG),
        name="swiglu_in",
    )(h, w_ffn_in, w_ffn_in)


def _mix_a_kernel(u_ref, v_ref, w_ref, bt_ref, o_ref, *, rows):
    w = w_ref[...].astype(BF16)
    for c in range(rows // A_CHUNK):
        rs = slice(c * A_CHUNK, (c + 1) * A_CHUNK)
        u = _gelu_tanh(u_ref[rs, :])
        v = _gelu_tanh(v_ref[rs, :])
        mu = jnp.mean(v, -1, keepdims=True)
        vc = v - mu
        vn = (vc * lax.rsqrt(jnp.mean(vc * vc, -1, keepdims=True) + NORM_EPS)).astype(BF16)
        for g in range(HEADS):
            cs = slice(g * HEAD_DIM, (g + 1) * HEAD_DIM)
            zg = jnp.dot(w[g], vn[:, cs], preferred_element_type=F32) + bt_ref[:, g:g + 1]
            o_ref[rs, cs] = (u[:, cs] * zg).astype(BF16)


def mix_a(z, a_w_l, a_b_l):
    rows = 512
    return pl.pallas_call(
        functools.partial(_mix_a_kernel, rows=rows),
        grid=(N_TOK // rows,),
        in_specs=[pl.BlockSpec((rows, WIDTH), lambda i: (i, COL_AU)),
                  pl.BlockSpec((rows, WIDTH), lambda i: (i, COL_AV)),
                  pl.BlockSpec((HEADS, A_CHUNK, A_CHUNK), lambda i: (0, 0, 0)),
                  pl.BlockSpec((A_CHUNK, HEADS), lambda i: (0, 0))],
        out_specs=pl.BlockSpec((rows, WIDTH), lambda i: (i, 0)),
        out_shape=jax.ShapeDtypeStruct((N_TOK, WIDTH), BF16),
        compiler_params=_cparams(1),
        name="mix_a",
    )(z, z, a_w_l, a_b_l.T)


def _softmax_pv(scores, values):
    m = scores[0].max(-1, keepdims=True)
    for s in scores[1:]:
        m = jnp.maximum(m, s.max(-1, keepdims=True))
    den = 0.0
    acc = 0.0
    for s, v in zip(scores, values):
        p = jnp.exp(s - m)
        den = den + p.sum(-1, keepdims=True)
        acc = acc + jnp.dot(p.astype(BF16), v, preferred_element_type=F32)
    return acc / den


def _qkt(q, k):
    return lax.dot_general(q, k, (((1,), (1,)), ((), ())), preferred_element_type=F32)


def _ctx_attn_kernel(q_ref, k_ref, v_ref, *rest):
    o_ref, ko_ref, vo_ref = rest[-3:]
    scale = HEAD_DIM ** -0.5
    ko_ref[...] = k_ref[...]
    vo_ref[...] = v_ref[...]
    for h in range(HEADS):
        cs = slice(h * HEAD_DIM, (h + 1) * HEAD_DIM)
        q = q_ref[:, cs].astype(BF16)
        k = k_ref[:, cs].astype(BF16)
        v = v_ref[:, cs].astype(BF16)
        o_ref[:, cs] = _softmax_pv([_qkt(q, k) * scale], [v]).astype(BF16)


def ctx_attention(z, l, k_all, v_all):
    spec = lambda col: pl.BlockSpec((SEQ, WIDTH), lambda b: (b, col))
    kv_spec = pl.BlockSpec((None, None, SEQ, WIDTH), lambda b: (b, l, 0, 0))
    kv_shape = jax.ShapeDtypeStruct((BATCH, DEPTH, SEQ, WIDTH), F32)
    raw = pl.BlockSpec(memory_space=pl.ANY)
    return pl.pallas_call(
        _ctx_attn_kernel,
        grid=(BATCH,),
        in_specs=[spec(COL_BQ), spec(COL_BK), spec(COL_BV), raw, raw],
        out_specs=[pl.BlockSpec((SEQ, WIDTH), lambda b: (b, 0)), kv_spec, kv_spec],
        out_shape=[jax.ShapeDtypeStruct((N_CTX, WIDTH), BF16), kv_shape, kv_shape],
        input_output_aliases={3: 1, 4: 2},
        compiler_params=_cparams(1),
        name="ctx_attn",
    )(z, z, z, k_all, v_all)


def _rope(x, cos, sin_signed, first_quarter):
    partner = jnp.where(first_quarter, pltpu.roll(x, HEAD_DIM - 32, 1), pltpu.roll(x, 32, 1))
    return x * cos + partner * sin_signed


NB_GRID_ROWS = DEC_SEQ // GRID_W
NB_WIN_ROWS = min(NB_ROWS_MAX, NB_GRID_ROWS)
NB_Q_ROWS = 4


def _nb_row_start(qr):
    return min(max(qr - NB_WIN_ROWS // 2, 0), NB_GRID_ROWS - NB_WIN_ROWS)


def _nb_tiles():
    tiles = []
    for q0 in range(0, NB_GRID_ROWS, NB_Q_ROWS):
        starts = [_nb_row_start(qr) for qr in range(q0, q0 + NB_Q_ROWS)]
        lo, hi = min(starts), max(starts) + NB_WIN_ROWS
        tiles.append((q0, lo - lo % 2, hi + hi % 2))
    return tiles


def _nbhd_attn_kernel(q_ref, k_ref, v_ref, kc_ref, vc_ref, pair_ref, cos_ref, sin_ref, o_ref, bias_scr):
    scale = HEAD_DIM ** -0.5
    lane = lax.broadcasted_iota(jnp.int32, (1, HEAD_DIM), 1)
    first_quarter = (lane % 64) < 32

    @pl.when(pl.program_id(1) == 0)
    def _():
        left = lane < GRID_W
        neg = jnp.full((GRID_W, 2 * GRID_W), NEG_INF, F32)
        for q0, lo, hi in _nb_tiles():
            for qr in range(q0, q0 + NB_Q_ROWS):
                r0 = _nb_row_start(qr)
                for p in range(lo // 2, hi // 2):
                    ok0 = r0 <= 2 * p < r0 + NB_WIN_ROWS
                    ok1 = r0 <= 2 * p + 1 < r0 + NB_WIN_ROWS
                    tile = neg
                    if ok0 or ok1:
                        tile = pair_ref[2 * p - qr + NB_ROWS_MAX]
                        if not ok0:
                            tile = jnp.where(left, NEG_INF, tile)
                        if not ok1:
                            tile = jnp.where(left, tile, NEG_INF)
                    bias_scr[qr * GRID_W:(qr + 1) * GRID_W, p * 2 * GRID_W:(p + 1) * 2 * GRID_W] = tile

    k = _rope(k_ref[...], cos_ref[...], sin_ref[...], first_quarter).astype(BF16)
    v = v_ref[...].astype(BF16)
    kc = kc_ref[...].astype(BF16)
    vc = vc_ref[...].astype(BF16)
    for q0, lo, hi in _nb_tiles():
        rs = slice(q0 * GRID_W, (q0 + NB_Q_ROWS) * GRID_W)
        ks = slice(lo * GRID_W, hi * GRID_W)
        q = _rope(q_ref[rs, :], cos_ref[rs, :], sin_ref[rs, :], first_quarter).astype(BF16)
        s_loc = _qkt(q, k[ks]) * scale + bias_scr[rs, ks]
        s_ctx = _qkt(q, kc) * scale
        o_ref[rs, :] = _softmax_pv([s_loc, s_ctx], [v[ks], vc]).astype(BF16)


def nbhd_attention(z, cache_k, cache_v, pair_table, cos, sin_signed, l):
    lat0 = N_CTX // DEC_SEQ
    col = lambda c: (lambda h, b: (lat0 + b, c * HEADS + h))
    tok = lambda c: pl.BlockSpec((DEC_SEQ, HEAD_DIM), col(c))
    cache = pl.BlockSpec((None, None, PAST_LEN, HEAD_DIM), lambda h, b: (b, l, 0, h))
    table = pl.BlockSpec((DEC_SEQ, HEAD_DIM), lambda h, b: (0, 0))
    pairs = pl.BlockSpec((None, 2 * NB_ROWS_MAX, GRID_W, 2 * GRID_W), lambda h, b: (h, 0, 0, 0))
    return pl.pallas_call(
        _nbhd_attn_kernel,
        grid=(HEADS, DEC_BATCH),
        in_specs=[tok(COL_BQ), tok(COL_BK), tok(COL_BV), cache, cache, pairs, table, table],
        out_specs=pl.BlockSpec((DEC_SEQ, HEAD_DIM), lambda h, b: (b, h)),
        out_shape=jax.ShapeDtypeStruct((N_LAT, WIDTH), BF16),
        scratch_shapes=[pltpu.VMEM((DEC_SEQ, DEC_SEQ), F32)],
        compiler_params=_cparams(2),
        name="nbhd_attn",
    )(z, z, z, cache_k, cache_v, pair_table, cos, sin_signed)


def nbhd_pair_tables(rpb):
    cols = jnp.arange(GRID_W)
    c_start = jnp.clip(cols - NB_COLS // 2, 0, GRID_W - NB_COLS)
    col_ok = (cols[None, :] >= c_start[:, None]) & (cols[None, :] < c_start[:, None] + NB_COLS)
    dc_idx = jnp.clip(cols[None, :] - cols[:, None] + NB_COLS - 1, 0, 2 * NB_COLS - 2)
    onehot = (dc_idx[None] == jnp.arange(2 * NB_COLS - 1)[:, None, None]).astype(F32)
    t = jnp.einsum('lhrd,dqk->lhrqk', rpb.astype(F32), onehot, precision=lax.Precision.HIGHEST)
    t = jnp.where(col_ok, t, NEG_INF)
    neg = jnp.full((DEPTH, HEADS, 1, GRID_W, GRID_W), NEG_INF, F32)
    tp = jnp.concatenate([neg, t, neg], 2)
    return jnp.concatenate([tp[:, :, :-1], tp[:, :, 1:]], -1)


def rope_tables():
    ax = HEAD_DIM // 2
    pos = jnp.arange(DEC_SEQ)
    inv = ROPE_THETA ** (-jnp.arange(0, ax, 2, dtype=F32) / ax)

    def ang(p):
        a = p.astype(F32)[:, None] * inv[None, :]
        return jnp.concatenate([a, a], -1)

    a = jnp.concatenate([ang(pos // GRID_W), ang(pos % GRID_W)], -1)
    sign = jnp.where((jnp.arange(HEAD_DIM) % ax) < ax // 2, -1.0, 1.0).astype(F32)
    return jnp.cos(a), jnp.sin(a) * sign[None, :]


def _split3(x):
    hi = x.astype(BF16)
    r = x - hi.astype(F32)
    mid = r.astype(BF16)
    lo = (r - mid.astype(F32)).astype(BF16)
    return hi, mid, lo


def _hgrn_chunks(chains, st, tri_ref, pairs_ref, side_ref):
    c = HGRN_CHUNK
    st = list(st)
    qs, kk, ff, bb, vb = [], [], [], [], []
    for q, zf, v, (log2lb, log21mlb, half1mlb), d, b_scr, _ in chains:
        z2 = zf * LOG2E
        ls = jnp.minimum(z2, 0.0) - jnp.log2(1.0 + jnp.exp2(-jnp.abs(z2)))
        x2 = log21mlb + ls
        lf = jnp.maximum(log2lb, x2) + jnp.log2(1.0 + jnp.exp2(-jnp.abs(log2lb - x2)))
        k = half1mlb * (1.0 - jnp.tanh(0.5 * zf))
        b = sum(jnp.dot(tri_ref[d], p, preferred_element_type=F32) for p in _split3(lf))
        b_scr[...] = b
        qs.append(_silu_tanh(q))
        kk.append(k)
        ff.append(1.0 - k)
        bb.append(b)
        vb.append(v.astype(BF16))

    outs = []
    for i, (_, _, _, _, d, b_scr, s) in enumerate(chains):
        end = 0 if d else c - 1
        b_end = b_scr[end:end + 1, :]
        outs.append(_qkt((qs[i] * jnp.exp2(bb[i])).astype(BF16), st[s].astype(BF16)))
        k_dec = (kk[i] * jnp.exp2(b_end - bb[i])).astype(BF16)
        st[s] = st[s] * jnp.exp2(b_end) + lax.dot_general(vb[i], k_dec, (((0,), (0,)), ((), ())),
                                                          preferred_element_type=F32)

    attn = []
    for i, (_, _, _, _, d, _, _) in enumerate(chains):
        a = jnp.zeros((c, c), F32)
        kd = kk[i]
        for delta in range(HGRN_BAND):
            if delta:
                kd3 = kd.reshape(c // SUBLANES, SUBLANES, HEAD_DIM)
                kd = ff[i] * pltpu.roll(kd3, SUBLANES - 1 if d else 1, 1).reshape(c, HEAD_DIM)
            a = jnp.where(pairs_ref[d] == delta, (qs[i] * kd).sum(-1, keepdims=True), a)
        attn.append(a)

    qb = [x.astype(BF16) for x in qs]
    kb = [x.astype(BF16) for x in kk]
    for lv, m in enumerate(HGRN_LEVELS):
        for i, (_, _, _, _, d, b_scr, _) in enumerate(chains):
            seam = m // 2 if d else m // 2 - 1
            g = jnp.concatenate([jnp.broadcast_to(b_scr[j * m + seam:j * m + seam + 1, :], (m, HEAD_DIM))
                                 for j in range(c // m)], 0)
            e = jnp.exp2((bb[i] - g) * side_ref[d, lv]).astype(BF16)
            attn[i] = jnp.where(pairs_ref[d] == m, _qkt(qb[i] * e, kb[i] * e), attn[i])
    for i in range(len(chains)):
        outs[i] += jnp.dot(attn[i].astype(BF16), vb[i], preferred_element_type=F32)
    return outs, st


def _hgrn_kernel(q_ref, ff_ref, fb_ref, v_ref, g_ref, lbp_ref, tri_ref, pairs_ref, side_ref, *rest,
                 n, heads, latent):
    of_scr, ob_scr, b_scr = rest[-3:]
    if latent:
        s0_ref, y_ref = rest[:2]
    else:
        y_ref, s_ref = rest[1:3]
    c = HGRN_CHUNK
    nc = n // c
    lanes = [slice(h * HEAD_DIM, (h + 1) * HEAD_DIM) for h in range(heads)]

    def body(i, st):
        chains, dst = [], []
        for u in range(HGRN_CHUNKS_PER_TRIP):
            cf = i * HGRN_CHUNKS_PER_TRIP + u
            rf = pl.ds(pl.multiple_of(cf * c, c), c)
            rb = pl.ds(pl.multiple_of((nc - 1 - cf) * c, c), c)
            for h, ls in enumerate(lanes):
                gate_f = [lbp_ref[r:r + 1, ls] for r in range(3)]
                gate_b = [lbp_ref[3 + r:4 + r, ls] for r in range(3)]
                slot = 2 * (u * heads + h)
                chains.append((q_ref[rf, ls], ff_ref[rf, ls], v_ref[rf, ls], gate_f, 0, b_scr.at[slot], 2 * h))
                chains.append((q_ref[rb, ls], fb_ref[rb, ls], v_ref[rb, ls], gate_b, 1, b_scr.at[slot + 1],
                               2 * h + 1))
                dst += [(of_scr, rf, ls), (ob_scr, rb, ls)]
        outs, st = _hgrn_chunks(chains, st, tri_ref, pairs_ref, side_ref)
        for (scr, rows, ls), o in zip(dst, outs):
            scr[rows, ls] = o
        return tuple(st)

    if latent:
        st0 = tuple(s0_ref[d, h].T for h in range(heads) for d in range(2))
    else:
        st0 = (jnp.zeros((HEAD_DIM, HEAD_DIM), F32),) * (2 * heads)
    st = lax.fori_loop(0, nc // HGRN_CHUNKS_PER_TRIP, body, st0)
    if not latent:
        for h in range(heads):
            for d in range(2):
                s_ref[d, h] = st[2 * h + d].T

    def finish(i, carry):
        rs = pl.ds(pl.multiple_of(i * c, c), c)
        for ls in lanes:
            y_ref[rs, ls] = (_rms(of_scr[rs, ls] + ob_scr[rs, ls]) * _silu_tanh(g_ref[rs, ls])).astype(BF16)
        return carry

    lax.fori_loop(0, nc, finish, 0)


def _hgrn_constants():
    c = HGRN_CHUNK
    t = jnp.arange(c)[:, None]
    s = jnp.arange(c)[None, :]
    tri_f = (s <= t)
    pairs_f = jnp.where((t // HGRN_BAND == s // HGRN_BAND) & (s <= t), t - s, -1).astype(jnp.int32)
    for m in HGRN_LEVELS:
        h = m // 2
        pairs_f = jnp.where((t // h == s // h + 1) & ((t // h) % 2 == 1), m, pairs_f)
    tri = jnp.stack([tri_f, tri_f.T]).astype(BF16)
    pairs = jnp.stack([pairs_f, pairs_f.T])
    later_half = jnp.stack([(jnp.arange(c) % m) >= m // 2 for m in HGRN_LEVELS])
    side_f = jnp.broadcast_to(jnp.where(later_half, 1.0, -1.0)[:, :, None], (len(HGRN_LEVELS), c, HEAD_DIM))
    side = jnp.stack([side_f, -side_f]).astype(F32)
    return tri, pairs, side


def hgrn(z, lbp, l, s_all=None, s0=None):
    latent = s0 is not None
    n = DEC_SEQ if latent else SEQ
    bsz = DEC_BATCH if latent else BATCH
    blk0 = N_CTX // DEC_SEQ if latent else 0
    hb = HGRN_LAT_HEADS_PER_STEP if latent else HGRN_CTX_HEADS_PER_STEP
    wb = hb * HEAD_DIM
    tri, pairs, side = _hgrn_constants()
    tok = lambda c: pl.BlockSpec((n, wb), lambda b, h: (blk0 + b, c * (HEADS // hb) + h))
    raw = pl.BlockSpec(memory_space=pl.ANY)
    in_specs = [tok(COL_CQ), tok(COL_CFF), tok(COL_CFB), tok(COL_CI), tok(COL_CG),
                pl.BlockSpec((8, wb), lambda b, h: (0, h)),
                pl.BlockSpec((2, HGRN_CHUNK, HGRN_CHUNK), lambda b, h: (0, 0, 0)),
                pl.BlockSpec((2, HGRN_CHUNK, HGRN_CHUNK), lambda b, h: (0, 0, 0)),
                pl.BlockSpec(side.shape, lambda b, h: (0, 0, 0, 0))]
    args = [z, z, z, z, z, lbp, tri, pairs, side]
    y_spec = pl.BlockSpec((n, wb), lambda b, h: (b, h))
    y_shape = jax.ShapeDtypeStruct((bsz * n, WIDTH), BF16)
    state = pl.BlockSpec((None, None, 2, hb, HEAD_DIM, HEAD_DIM), lambda b, h: (b, l, 0, h, 0, 0))
    if latent:
        in_specs.append(state)
        args.append(s0)
        out_specs, out_shape, aliases = y_spec, y_shape, {}
    else:
        aliases = {len(args): 1}
        in_specs.append(raw)
        args.append(s_all)
        out_specs = [y_spec, state]
        out_shape = [y_shape, jax.ShapeDtypeStruct(s_all.shape, F32)]
    return pl.pallas_call(
        functools.partial(_hgrn_kernel, n=n, heads=hb, latent=latent),
        grid=(bsz, HEADS // hb),
        in_specs=in_specs, out_specs=out_specs, out_shape=out_shape,
        scratch_shapes=[pltpu.VMEM((n, wb), F32)] * 2
        + [pltpu.VMEM((2 * HGRN_CHUNKS_PER_TRIP * hb, HGRN_CHUNK, HEAD_DIM), F32)],
        input_output_aliases=aliases,
        compiler_params=_cparams(2),
        name="hgrn",
    )(*args)


def hgrn_gate_params(lb_logits):
    p = jax.nn.softmax(lb_logits.astype(F32), axis=1)
    cs = jnp.cumsum(p, axis=1)
    lb = cs - cs[:, :1]
    rows = [jnp.log(lb[0]) * LOG2E, jnp.log1p(-lb[0]) * LOG2E, 0.5 * (1.0 - lb[0]),
            jnp.log(lb[1]) * LOG2E, jnp.log1p(-lb[1]) * LOG2E, 0.5 * (1.0 - lb[1]),
            jnp.zeros_like(lb[0]), jnp.zeros_like(lb[0])]
    return jnp.stack(rows, axis=1)


def kernel(x_prompt, x_sample, cache_attn_k, cache_attn_v, state_hgrn, c, c_ctx, w_ada, b_ada, norm_g, w_in,
           a_spatial_w, a_spatial_b, nb_rpb, hgrn_lb_logits, w_up_a, w_up_b, w_up_c, w_out, w_ffn_in, w_ffn_out):
    x = jnp.concatenate([x_prompt.reshape(N_CTX, D_MODEL), x_sample.reshape(N_LAT, D_MODEL)], 0)
    cond = jnp.zeros((COND_ROWS, D_MODEL), F32).at[0].set(c_ctx).at[1:N_COND].set(c)
    mod = ada_all_layers(cond, w_ada, b_ada)
    mod = mod.reshape(DEPTH, COND_ROWS, 6, 1, D_MODEL).transpose(0, 2, 1, 3, 4)
    lbp = hgrn_gate_params(hgrn_lb_logits)
    cos, sin_signed = rope_tables()
    pair_tables = nbhd_pair_tables(nb_rpb)
    cache_k = cache_attn_k.reshape(DEC_BATCH, DEPTH, PAST_LEN, WIDTH)
    cache_v = cache_attn_v.reshape(DEC_BATCH, DEPTH, PAST_LEN, WIDTH)
    gain = lambda l, i: norm_g[l, i][None, :]
    w_out_b = w_out.astype(BF16)
    w_ffn_out_b = w_ffn_out.astype(BF16)
    w_up_a, w_up_b, w_up_c = (w.astype(BF16) for w in (w_up_a, w_up_b, w_up_c))

    k_all = jnp.zeros((BATCH, DEPTH, SEQ, WIDTH), F32)
    v_all = jnp.zeros((BATCH, DEPTH, SEQ, WIDTH), F32)
    s_all = jnp.zeros((BATCH, DEPTH, 2, HEADS, HEAD_DIM, HEAD_DIM), F32)
    h = prenorm(x, gain(0, 0), mod[0, 1], mod[0, 0])
    for l in range(DEPTH):
        sh1, sc1, g1, sh2, sc2, g2 = (mod[l, i] for i in range(6))
        z = matmul_layer(h, w_in, l, tm=1024, tn=1024)
        y_a = mix_a(z, a_spatial_w[l], a_spatial_b[l])
        yb_ctx, k_all, v_all = ctx_attention(z, l, k_all, v_all)
        yb_lat = nbhd_attention(z, cache_k, cache_v, pair_tables[l], cos, sin_signed, l)
        yc_ctx, s_all = hgrn(z, lbp[l], l, s_all=s_all)
        yc_lat = hgrn(z, lbp[l], l, s0=state_hgrn)
        merged = merge_branches(y_a, (yb_ctx, yb_lat), (yc_ctx, yc_lat), z, w_up_a, w_up_b, w_up_c, l)
        x, h2 = proj_resnorm(merged, w_out_b, l, x, gain(l, 1), g1, nxt=(gain(l, 2), sc2, sh2))
        act = swiglu_in(h2, w_ffn_in, l)
        nxt = None if l == DEPTH - 1 else (gain(l + 1, 0), mod[l + 1, 1], mod[l + 1, 0])
        x, h = proj_resnorm(act, w_ffn_out_b, l, x, gain(l, 3), g2, nxt=nxt, tk=D_FF // 4)

    y_prompt = x[:N_CTX].reshape(BATCH, SEQ, D_MODEL)
    y_sample = x[N_CTX:].reshape(DEC_BATCH, DEC_SEQ, D_MODEL)
    kv_shape = (BATCH, DEPTH, SEQ, HEADS, HEAD_DIM)
    return (y_prompt, y_sample, k_all.reshape(kv_shape), v_all.reshape(kv_shape), s_all)
```

```python
import functools
import math

import jax
import jax.numpy as jnp
from jax import lax
from jax.experimental import pallas as pl
from jax.experimental.pallas import tpu as pltpu

F32 = jnp.float32
BF16 = jnp.bfloat16

D_MODEL = 2048
BATCH = 32
SEQ = 256
DEPTH = 4
DEC_BATCH = 2
DEC_SEQ = 1024
PAST_LEN = 512
GRID_W = 64
HEADS = 8
HEAD_DIM = 128
WIDTH = HEADS * HEAD_DIM
A_CHUNK = 128
NB_ROWS_MAX = 8
NB_COLS = 16
ROPE_THETA = 10000.0
D_FF = -(-8 * D_MODEL // (3 * 256)) * 256
NORM_EPS = 1e-6
NEG_INF = -1e30
IN_WIDTH = 10 * WIDTH + 3 * D_MODEL

N_CTX = BATCH * SEQ
N_LAT = DEC_BATCH * DEC_SEQ
N_TOK = N_CTX + N_LAT
N_COND = 1 + DEC_BATCH
COND_ROWS = 8

COL_AU, COL_AV, COL_BQ, COL_BK, COL_BV, COL_CQ, COL_CFF, COL_CFB, COL_CI, COL_CG = range(10)
COL_GATES = 10 * WIDTH

HGRN_CHUNK = 128
HGRN_BAND = 4
HGRN_LEVELS = (8, 16, 32, 64, 128)
SUBLANES = 8
HGRN_CHUNKS_PER_TRIP = 2
HGRN_CTX_HEADS_PER_STEP = 4
HGRN_LAT_HEADS_PER_STEP = 2

LOG2E = math.log2(math.e)

VMEM_LIMIT = 48 * 1024 * 1024
VMEM_LIMIT_BIG = 56 * 1024 * 1024


def _cparams(n_axes, vmem=VMEM_LIMIT):
    return pltpu.CompilerParams(dimension_semantics=("arbitrary",) * n_axes, vmem_limit_bytes=vmem)


def _cond_row(i, tm):
    n_ctx_tiles = N_CTX // tm
    return jnp.where(i < n_ctx_tiles, 0, 1 + (i - n_ctx_tiles) // (DEC_SEQ // tm))


def _silu(x):
    return x * jax.nn.sigmoid(x)


def _sigmoid(x):
    return 0.5 * (1.0 + jnp.tanh(0.5 * x))


def _silu_tanh(x):
    hx = 0.5 * x
    return hx + hx * jnp.tanh(hx)


def _gelu_tanh(x):
    c = math.sqrt(2.0 / math.pi)
    return x * (0.5 * (1.0 + jnp.tanh(c * (x + 0.044715 * (x * x * x)))))


def _rms(x):
    return x * lax.rsqrt(jnp.mean(x * x, -1, keepdims=True) + NORM_EPS)


def _ada_kernel(c_ref, w_ref, b_ref, o_ref):
    s = _silu(c_ref[...]).astype(BF16)
    o_ref[...] = jnp.dot(s, w_ref[...].astype(BF16), preferred_element_type=F32) + b_ref[...]


def ada_all_layers(cond, w_ada, b_ada):
    tn = 1024
    n_out = 6 * D_MODEL
    return pl.pallas_call(
        _ada_kernel,
        grid=(DEPTH, n_out // tn),
        in_specs=[pl.BlockSpec((COND_ROWS, D_MODEL), lambda l, j: (0, 0)),
                  pl.BlockSpec((None, D_MODEL, tn), lambda l, j: (l, 0, j)),
                  pl.BlockSpec((None, 1, tn), lambda l, j: (l, 0, j))],
        out_specs=pl.BlockSpec((None, COND_ROWS, tn), lambda l, j: (l, 0, j)),
        out_shape=jax.ShapeDtypeStruct((DEPTH, COND_ROWS, n_out), F32),
        compiler_params=_cparams(2),
        name="ada",
    )(cond, w_ada, b_ada.reshape(DEPTH, 1, n_out))


def _path_specs(tm):
    n_ctx_tiles = N_CTX // tm
    ctx = pl.BlockSpec((tm, D_MODEL), lambda i, *_: (jnp.minimum(i, n_ctx_tiles - 1), 0))
    lat = pl.BlockSpec((tm, D_MODEL), lambda i, *_: (jnp.maximum(i - n_ctx_tiles, 0), 0))
    return ctx, lat


def _prenorm_kernel(xc_ref, xl_ref, g_ref, sc_ref, sh_ref, h_ref, *, n_ctx_tiles):
    x = jnp.where(pl.program_id(0) < n_ctx_tiles, xc_ref[...], xl_ref[...])
    h_ref[...] = (_rms(x) * (g_ref[...] * (1.0 + sc_ref[...])) + sh_ref[...]).astype(BF16)


def prenorm(x_ctx, x_lat, gain, sc, sh):
    tm = 512
    row = lambda i: (_cond_row(i, tm), 0, 0)
    ctx, lat = _path_specs(tm)
    return pl.pallas_call(
        functools.partial(_prenorm_kernel, n_ctx_tiles=N_CTX // tm),
        grid=(N_TOK // tm,),
        in_specs=[ctx, lat,
                  pl.BlockSpec((1, D_MODEL), lambda i: (0, 0)),
                  pl.BlockSpec((None, 1, D_MODEL), row),
                  pl.BlockSpec((None, 1, D_MODEL), row)],
        out_specs=pl.BlockSpec((tm, D_MODEL), lambda i: (i, 0)),
        out_shape=jax.ShapeDtypeStruct((N_TOK, D_MODEL), BF16),
        compiler_params=_cparams(1),
        name="prenorm",
    )(x_ctx, x_lat, gain, sc, sh)


def _proj_resnorm_kernel(a_ref, w_ref, *rest, nk, n_ctx_tiles, split_in, split_out, with_next):
    rest = list(rest)
    x_refs = [rest.pop(0) for _ in range(1 + split_in)]
    g_ref, gate_ref = rest.pop(0), rest.pop(0)
    if with_next:
        g2_ref, sc_ref, sh_ref = rest.pop(0), rest.pop(0), rest.pop(0)
    xo_refs = [rest.pop(0) for _ in range(1 + split_out)]
    if with_next:
        h_ref = rest.pop(0)
    is_ctx = pl.program_id(0) < n_ctx_tiles

    def part():
        return jnp.dot(a_ref[...], w_ref[...], preferred_element_type=F32)

    def finish(t):
        x = jnp.where(is_ctx, x_refs[0][...], x_refs[1][...]) if split_in else x_refs[0][...]
        x_new = x + _rms(t) * (gate_ref[...] * g_ref[...])
        if split_out:
            @pl.when(is_ctx)
            def _():
                xo_refs[0][...] = x_new

            @pl.when(jnp.logical_not(is_ctx))
            def _():
                xo_refs[1][...] = x_new
        else:
            xo_refs[0][...] = x_new
        if with_next:
            h_ref[...] = (_rms(x_new) * (g2_ref[...] * (1.0 + sc_ref[...])) + sh_ref[...]).astype(BF16)

    if nk == 1:
        finish(part())
        return
    acc_ref = rest[-1]
    k = pl.program_id(1)

    @pl.when(k == 0)
    def _():
        acc_ref[...] = part()

    @pl.when(k > 0)
    def _():
        acc_ref[...] += part()

    @pl.when(k == nk - 1)
    def _():
        finish(acc_ref[...])


def proj_resnorm(a, w, l, x, gain, gate, nxt=None, tk=None, split_out=False):
    tm = 512
    tk = tk or a.shape[1]
    nk = a.shape[1] // tk
    split_in = isinstance(x, tuple)
    row = lambda i, k: (_cond_row(i, tm), 0, 0)
    tile = pl.BlockSpec((tm, D_MODEL), lambda i, k: (i, 0))
    vec = pl.BlockSpec((1, D_MODEL), lambda i, k: (0, 0))
    mod = pl.BlockSpec((None, 1, D_MODEL), row)
    w_mode = dict(pipeline_mode=pl.Buffered(1)) if nk == 1 else {}
    in_specs = [pl.BlockSpec((tm, tk), lambda i, k: (i, k)),
                pl.BlockSpec((None, tk, D_MODEL), lambda i, k: (l, k, 0), **w_mode)]
    in_specs += list(_path_specs(tm)) if split_in else [tile]
    in_specs += [vec, mod]
    args = [a, w, *(x if split_in else (x,)), gain, gate]
    if nxt is not None:
        in_specs += [vec, mod, mod]
        args += list(nxt)
    if split_out:
        out_specs = list(_path_specs(tm))
        out_shape = [jax.ShapeDtypeStruct((N_CTX, D_MODEL), F32), jax.ShapeDtypeStruct((N_LAT, D_MODEL), F32)]
    else:
        out_specs = [tile]
        out_shape = [jax.ShapeDtypeStruct((N_TOK, D_MODEL), F32)]
    if nxt is not None:
        out_specs.append(tile)
        out_shape.append(jax.ShapeDtypeStruct((N_TOK, D_MODEL), BF16))
    out = pl.pallas_call(
        functools.partial(_proj_resnorm_kernel, nk=nk, n_ctx_tiles=N_CTX // tm, split_in=split_in,
                          split_out=split_out, with_next=nxt is not None),
        grid=(N_TOK // tm, nk),
        in_specs=in_specs, out_specs=out_specs, out_shape=out_shape,
        scratch_shapes=[pltpu.VMEM((tm, D_MODEL), F32)] if nk > 1 else [],
        compiler_params=_cparams(2, VMEM_LIMIT_BIG),
        name="proj_resnorm",
    )(*args)
    n_x = 1 + split_out
    x_new = tuple(out[:n_x]) if split_out else out[0]
    return x_new, (out[n_x] if nxt is not None else None)


def _mm_kernel(a_ref, w_ref, o_ref, wb_ref):
    @pl.when(pl.program_id(1) == 0)
    def _():
        wb_ref[...] = w_ref[...].astype(BF16)

    o_ref[...] = jnp.dot(a_ref[...], wb_ref[...], preferred_element_type=F32)


def matmul_layer(a, w, l, tm, tn, vmem=VMEM_LIMIT):
    m, k = a.shape
    n = w.shape[2]
    return pl.pallas_call(
        _mm_kernel,
        grid=(n // tn, m // tm),
        in_specs=[pl.BlockSpec((tm, k), lambda j, i: (i, 0)),
                  pl.BlockSpec((None, k, tn), lambda j, i: (l, 0, j))],
        out_specs=pl.BlockSpec((tm, tn), lambda j, i: (i, j)),
        out_shape=jax.ShapeDtypeStruct((m, n), F32),
        scratch_shapes=[pltpu.VMEM((k, tn), BF16)],
        compiler_params=_cparams(2, vmem),
        name="matmul",
    )(a, w)


def _merge_kernel(ya_ref, ybc_ref, ybl_ref, ycc_ref, ycl_ref, wa_ref, wb_ref, wc_ref, ga_ref, gb_ref, gc_ref,
                  o_ref, *, n_ctx_tiles):
    def run(yb_ref, yc_ref):
        acc = _sigmoid(ga_ref[...]) * jnp.dot(ya_ref[...], wa_ref[...], preferred_element_type=F32)
        acc += _sigmoid(gb_ref[...]) * jnp.dot(yb_ref[...], wb_ref[...], preferred_element_type=F32)
        acc += _sigmoid(gc_ref[...]) * jnp.dot(yc_ref[...], wc_ref[...], preferred_element_type=F32)
        o_ref[...] = acc.astype(BF16)

    pl.when(pl.program_id(0) < n_ctx_tiles)(lambda: run(ybc_ref, ycc_ref))
    pl.when(pl.program_id(0) >= n_ctx_tiles)(lambda: run(ybl_ref, ycl_ref))


def merge_branches(y_a, y_b, y_c, z, w_up_a, w_up_b, w_up_c, l):
    tm, tn = 1024, 512
    gate0 = COL_GATES // tn
    per_gate = D_MODEL // tn
    n_ctx_tiles = N_CTX // tm
    y_spec = pl.BlockSpec((tm, WIDTH), lambda i, j: (i, 0))
    ctx_spec = pl.BlockSpec((tm, WIDTH), lambda i, j: (jnp.minimum(i, n_ctx_tiles - 1), 0))
    lat_spec = pl.BlockSpec((tm, WIDTH), lambda i, j: (jnp.maximum(i - n_ctx_tiles, 0), 0))
    w_spec = pl.BlockSpec((None, WIDTH, tn), lambda i, j: (l, 0, j))
    g_spec = lambda b: pl.BlockSpec((tm, tn), lambda i, j: (i, gate0 + b * per_gate + j))
    return pl.pallas_call(
        functools.partial(_merge_kernel, n_ctx_tiles=n_ctx_tiles),
        grid=(N_TOK // tm, D_MODEL // tn),
        in_specs=[y_spec, ctx_spec, lat_spec, ctx_spec, lat_spec, w_spec, w_spec, w_spec,
                  g_spec(0), g_spec(1), g_spec(2)],
        out_specs=pl.BlockSpec((tm, tn), lambda i, j: (i, j)),
        out_shape=jax.ShapeDtypeStruct((N_TOK, D_MODEL), BF16),
        compiler_params=_cparams(2),
        name="merge",
    )(y_a, y_b[0], y_b[1], y_c[0], y_c[1], w_up_a, w_up_b, w_up_c, z, z, z)


def _swiglu_kernel(h_ref, wu_ref, wg_ref, o_ref, su_ref, sg_ref):
    @pl.when(pl.program_id(1) == 0)
    def _():
        su_ref[...] = wu_ref[...].astype(BF16)
        sg_ref[...] = wg_ref[...].astype(BF16)

    h = h_ref[...]
    up = jnp.dot(h, su_ref[...], preferred_element_type=F32)
    gate = jnp.dot(h, sg_ref[...], preferred_element_type=F32)
    o_ref[...] = (_silu(gate) * up).astype(BF16)


def swiglu_in(h, w_ffn_in, l):
    tm, tn = 2048, 512
    n_up = D_FF // tn
    return pl.pallas_call(
        _swiglu_kernel,
        grid=(n_up, N_TOK // tm),
        in_specs=[pl.BlockSpec((tm, D_MODEL), lambda j, i: (i, 0)),
                  pl.BlockSpec((None, D_MODEL, tn), lambda j, i: (l, 0, j)),
                  pl.BlockSpec((None, D_MODEL, tn), lambda j, i: (l, 0, n_up + j))],
        out_specs=pl.BlockSpec((tm, tn), lambda j, i: (i, j)),
        out_shape=jax.ShapeDtypeStruct((N_TOK, D_FF), BF16),
        scratch_shapes=[pltpu.VMEM((D_MODEL, tn), BF16)] * 2,
        compiler_params=_cparams(2, VMEM_LIMIT_BIG),
        name="swiglu_in",
    )(h, w_ffn_in, w_ffn_in)


def _mix_a_kernel(u_ref, v_ref, w_ref, bt_ref, o_ref, *, rows):
    w = w_ref[...].astype(BF16)
    for c in range(rows // A_CHUNK):
        rs = slice(c * A_CHUNK, (c + 1) * A_CHUNK)
        u = _gelu_tanh(u_ref[rs, :])
        v = _gelu_tanh(v_ref[rs, :])
        mu = jnp.mean(v, -1, keepdims=True)
        vc = v - mu
        vn = (vc * lax.rsqrt(jnp.mean(vc * vc, -1, keepdims=True) + NORM_EPS)).astype(BF16)
        for g in range(HEADS):
            cs = slice(g * HEAD_DIM, (g + 1) * HEAD_DIM)
            zg = jnp.dot(w[g], vn[:, cs], preferred_element_type=F32) + bt_ref[:, g:g + 1]
            o_ref[rs, cs] = (u[:, cs] * zg).astype(BF16)


def mix_a(z, a_w_l, a_b_l):
    rows = 512
    return pl.pallas_call(
        functools.partial(_mix_a_kernel, rows=rows),
        grid=(N_TOK // rows,),
        in_specs=[pl.BlockSpec((rows, WIDTH), lambda i: (i, COL_AU)),
                  pl.BlockSpec((rows, WIDTH), lambda i: (i, COL_AV)),
                  pl.BlockSpec((HEADS, A_CHUNK, A_CHUNK), lambda i: (0, 0, 0)),
                  pl.BlockSpec((A_CHUNK, HEADS), lambda i: (0, 0))],
        out_specs=pl.BlockSpec((rows, WIDTH), lambda i: (i, 0)),
        out_shape=jax.ShapeDtypeStruct((N_TOK, WIDTH), BF16),
        compiler_params=_cparams(1),
        name="mix_a",
    )(z, z, a_w_l, a_b_l.T)


def _softmax_pv(scores, values):
    m = scores[0].max(-1, keepdims=True)
    for s in scores[1:]:
        m = jnp.maximum(m, s.max(-1, keepdims=True))
    den = 0.0
    acc = 0.0
    for s, v in zip(scores, values):
        p = jnp.exp(s - m)
        den = den + p.sum(-1, keepdims=True)
        acc = acc + jnp.dot(p.astype(BF16), v, preferred_element_type=F32)
    return acc / den


def _qkt(q, k):
    return lax.dot_general(q, k, (((1,), (1,)), ((), ())), preferred_element_type=F32)


def _ctx_attn_kernel(q_ref, k_ref, v_ref, *rest):
    o_ref, ko_ref, vo_ref = rest[-3:]
    scale = HEAD_DIM ** -0.5
    ko_ref[...] = k_ref[...]
    vo_ref[...] = v_ref[...]
    for h in range(HEADS):
        cs = slice(h * HEAD_DIM, (h + 1) * HEAD_DIM)
        q = q_ref[:, cs].astype(BF16)
        k = k_ref[:, cs].astype(BF16)
        v = v_ref[:, cs].astype(BF16)
        o_ref[:, cs] = _softmax_pv([_qkt(q, k) * scale], [v]).astype(BF16)


def ctx_attention(z, l, k_all, v_all):
    spec = lambda col: pl.BlockSpec((SEQ, WIDTH), lambda b: (b, col))
    kv_spec = pl.BlockSpec((None, None, SEQ, WIDTH), lambda b: (b, l, 0, 0))
    kv_shape = jax.ShapeDtypeStruct((BATCH, DEPTH, SEQ, WIDTH), F32)
    raw = pl.BlockSpec(memory_space=pl.ANY)
    return pl.pallas_call(
        _ctx_attn_kernel,
        grid=(BATCH,),
        in_specs=[spec(COL_BQ), spec(COL_BK), spec(COL_BV), raw, raw],
        out_specs=[pl.BlockSpec((SEQ, WIDTH), lambda b: (b, 0)), kv_spec, kv_spec],
        out_shape=[jax.ShapeDtypeStruct((N_CTX, WIDTH), BF16), kv_shape, kv_shape],
        input_output_aliases={3: 1, 4: 2},
        compiler_params=_cparams(1),
        name="ctx_attn",
    )(z, z, z, k_all, v_all)


def _rope(x, cos, sin_signed, first_quarter):
    partner = jnp.where(first_quarter, pltpu.roll(x, HEAD_DIM - 32, 1), pltpu.roll(x, 32, 1))
    return x * cos + partner * sin_signed


NB_GRID_ROWS = DEC_SEQ // GRID_W
NB_WIN_ROWS = min(NB_ROWS_MAX, NB_GRID_ROWS)
NB_Q_ROWS = 4


def _nb_row_start(qr):
    return min(max(qr - NB_WIN_ROWS // 2, 0), NB_GRID_ROWS - NB_WIN_ROWS)


def _nb_tiles():
    tiles = []
    for q0 in range(0, NB_GRID_ROWS, NB_Q_ROWS):
        starts = [_nb_row_start(qr) for qr in range(q0, q0 + NB_Q_ROWS)]
        lo, hi = min(starts), max(starts) + NB_WIN_ROWS
        tiles.append((q0, lo - lo % 2, hi + hi % 2))
    return tiles


def _nbhd_attn_kernel(q_ref, k_ref, v_ref, kc_ref, vc_ref, pair_ref, cos_ref, sin_ref, o_ref, bias_scr):
    scale = HEAD_DIM ** -0.5
    lane = lax.broadcasted_iota(jnp.int32, (1, HEAD_DIM), 1)
    first_quarter = (lane % 64) < 32

    @pl.when(pl.program_id(1) == 0)
    def _():
        left = lane < GRID_W
        neg = jnp.full((GRID_W, 2 * GRID_W), NEG_INF, F32)
        for q0, lo, hi in _nb_tiles():
            for qr in range(q0, q0 + NB_Q_ROWS):
                r0 = _nb_row_start(qr)
                for p in range(lo // 2, hi // 2):
                    ok0 = r0 <= 2 * p < r0 + NB_WIN_ROWS
                    ok1 = r0 <= 2 * p + 1 < r0 + NB_WIN_ROWS
                    tile = neg
                    if ok0 or ok1:
                        tile = pair_ref[2 * p - qr + NB_ROWS_MAX]
                        if not ok0:
                            tile = jnp.where(left, NEG_INF, tile)
                        if not ok1:
                            tile = jnp.where(left, tile, NEG_INF)
                    bias_scr[qr * GRID_W:(qr + 1) * GRID_W, p * 2 * GRID_W:(p + 1) * 2 * GRID_W] = tile

    k = _rope(k_ref[...], cos_ref[...], sin_ref[...], first_quarter).astype(BF16)
    v = v_ref[...].astype(BF16)
    kc = kc_ref[...].astype(BF16)
    vc = vc_ref[...].astype(BF16)
    for q0, lo, hi in _nb_tiles():
        rs = slice(q0 * GRID_W, (q0 + NB_Q_ROWS) * GRID_W)
        ks = slice(lo * GRID_W, hi * GRID_W)
        q = _rope(q_ref[rs, :], cos_ref[rs, :], sin_ref[rs, :], first_quarter).astype(BF16)
        s_loc = _qkt(q, k[ks]) * scale + bias_scr[rs, ks]
        s_ctx = _qkt(q, kc) * scale
        o_ref[rs, :] = _softmax_pv([s_loc, s_ctx], [v[ks], vc]).astype(BF16)


def nbhd_attention(z, cache_k, cache_v, pair_table, cos, sin_signed, l):
    lat0 = N_CTX // DEC_SEQ
    col = lambda c: (lambda h, b: (lat0 + b, c * HEADS + h))
    tok = lambda c: pl.BlockSpec((DEC_SEQ, HEAD_DIM), col(c))
    cache = pl.BlockSpec((None, None, PAST_LEN, HEAD_DIM), lambda h, b: (b, l, 0, h))
    table = pl.BlockSpec((DEC_SEQ, HEAD_DIM), lambda h, b: (0, 0))
    pairs = pl.BlockSpec((None, 2 * NB_ROWS_MAX, GRID_W, 2 * GRID_W), lambda h, b: (h, 0, 0, 0))
    return pl.pallas_call(
        _nbhd_attn_kernel,
        grid=(HEADS, DEC_BATCH),
        in_specs=[tok(COL_BQ), tok(COL_BK), tok(COL_BV), cache, cache, pairs, table, table],
        out_specs=pl.BlockSpec((DEC_SEQ, HEAD_DIM), lambda h, b: (b, h)),
        out_shape=jax.ShapeDtypeStruct((N_LAT, WIDTH), BF16),
        scratch_shapes=[pltpu.VMEM((DEC_SEQ, DEC_SEQ), F32)],
        compiler_params=_cparams(2),
        name="nbhd_attn",
    )(z, z, z, cache_k, cache_v, pair_table, cos, sin_signed)


def nbhd_pair_tables(rpb):
    cols = jnp.arange(GRID_W)
    c_start = jnp.clip(cols - NB_COLS // 2, 0, GRID_W - NB_COLS)
    col_ok = (cols[None, :] >= c_start[:, None]) & (cols[None, :] < c_start[:, None] + NB_COLS)
    dc_idx = jnp.clip(cols[None, :] - cols[:, None] + NB_COLS - 1, 0, 2 * NB_COLS - 2)
    onehot = (dc_idx[None] == jnp.arange(2 * NB_COLS - 1)[:, None, None]).astype(F32)
    t = jnp.einsum('lhrd,dqk->lhrqk', rpb.astype(F32), onehot, precision=lax.Precision.HIGHEST)
    t = jnp.where(col_ok, t, NEG_INF)
    neg = jnp.full((DEPTH, HEADS, 1, GRID_W, GRID_W), NEG_INF, F32)
    tp = jnp.concatenate([neg, t, neg], 2)
    return jnp.concatenate([tp[:, :, :-1], tp[:, :, 1:]], -1)


def rope_tables():
    ax = HEAD_DIM // 2
    pos = jnp.arange(DEC_SEQ)
    inv = ROPE_THETA ** (-jnp.arange(0, ax, 2, dtype=F32) / ax)

    def ang(p):
        a = p.astype(F32)[:, None] * inv[None, :]
        return jnp.concatenate([a, a], -1)

    a = jnp.concatenate([ang(pos // GRID_W), ang(pos % GRID_W)], -1)
    sign = jnp.where((jnp.arange(HEAD_DIM) % ax) < ax // 2, -1.0, 1.0).astype(F32)
    return jnp.cos(a), jnp.sin(a) * sign[None, :]


def _split3(x):
    hi = x.astype(BF16)
    r = x - hi.astype(F32)
    mid = r.astype(BF16)
    lo = (r - mid.astype(F32)).astype(BF16)
    return hi, mid, lo


def _hgrn_chunks(chains, st, tri_ref, pairs_ref, side_ref):
    c = HGRN_CHUNK
    st = list(st)
    qs, kk, ff, bb, vb = [], [], [], [], []
    for q, zf, v, (log2lb, log21mlb, half1mlb), d, b_scr, _ in chains:
        z2 = zf * LOG2E
        ls = jnp.minimum(z2, 0.0) - jnp.log2(1.0 + jnp.exp2(-jnp.abs(z2)))
        x2 = log21mlb + ls
        lf = jnp.maximum(log2lb, x2) + jnp.log2(1.0 + jnp.exp2(-jnp.abs(log2lb - x2)))
        k = half1mlb * (1.0 - jnp.tanh(0.5 * zf))
        b = sum(jnp.dot(tri_ref[d], p, preferred_element_type=F32) for p in _split3(lf))
        b_scr[...] = b
        qs.append(_silu_tanh(q))
        kk.append(k)
        ff.append(1.0 - k)
        bb.append(b)
        vb.append(v.astype(BF16))

    outs = []
    for i, (_, _, _, _, d, b_scr, s) in enumerate(chains):
        end = 0 if d else c - 1
        b_end = b_scr[end:end + 1, :]
        outs.append(_qkt((qs[i] * jnp.exp2(bb[i])).astype(BF16), st[s].astype(BF16)))
        k_dec = (kk[i] * jnp.exp2(b_end - bb[i])).astype(BF16)
        st[s] = st[s] * jnp.exp2(b_end) + lax.dot_general(vb[i], k_dec, (((0,), (0,)), ((), ())),
                                                          preferred_element_type=F32)

    attn = []
    for i, (_, _, _, _, d, _, _) in enumerate(chains):
        a = jnp.zeros((c, c), F32)
        kd = kk[i]
        for delta in range(HGRN_BAND):
            if delta:
                kd3 = kd.reshape(c // SUBLANES, SUBLANES, HEAD_DIM)
                kd = ff[i] * pltpu.roll(kd3, SUBLANES - 1 if d else 1, 1).reshape(c, HEAD_DIM)
            a = jnp.where(pairs_ref[d] == delta, (qs[i] * kd).sum(-1, keepdims=True), a)
        attn.append(a)

    qb = [x.astype(BF16) for x in qs]
    kb = [x.astype(BF16) for x in kk]
    for lv, m in enumerate(HGRN_LEVELS):
        for i, (_, _, _, _, d, b_scr, _) in enumerate(chains):
            seam = m // 2 if d else m // 2 - 1
            g = jnp.concatenate([jnp.broadcast_to(b_scr[j * m + seam:j * m + seam + 1, :], (m, HEAD_DIM))
                                 for j in range(c // m)], 0)
            e = jnp.exp2((bb[i] - g) * side_ref[d, lv]).astype(BF16)
            attn[i] = jnp.where(pairs_ref[d] == m, _qkt(qb[i] * e, kb[i] * e), attn[i])
    for i in range(len(chains)):
        outs[i] += jnp.dot(attn[i].astype(BF16), vb[i], preferred_element_type=F32)
    return outs, st


def _hgrn_kernel(q_ref, ff_ref, fb_ref, v_ref, g_ref, lbp_ref, tri_ref, pairs_ref, side_ref, *rest,
                 n, heads, latent):
    of_scr, ob_scr, b_scr = rest[-3:]
    if latent:
        s0_ref, y_ref = rest[:2]
    else:
        y_ref, s_ref = rest[1:3]
    c = HGRN_CHUNK
    nc = n // c
    lanes = [slice(h * HEAD_DIM, (h + 1) * HEAD_DIM) for h in range(heads)]

    def body(i, st):
        chains, dst = [], []
        for u in range(HGRN_CHUNKS_PER_TRIP):
            cf = i * HGRN_CHUNKS_PER_TRIP + u
            rf = pl.ds(pl.multiple_of(cf * c, c), c)
            rb = pl.ds(pl.multiple_of((nc - 1 - cf) * c, c), c)
            for h, ls in enumerate(lanes):
                gate_f = [lbp_ref[r:r + 1, ls] for r in range(3)]
                gate_b = [lbp_ref[3 + r:4 + r, ls] for r in range(3)]
                slot = 2 * (u * heads + h)
                chains.append((q_ref[rf, ls], ff_ref[rf, ls], v_ref[rf, ls], gate_f, 0, b_scr.at[slot], 2 * h))
                chains.append((q_ref[rb, ls], fb_ref[rb, ls], v_ref[rb, ls], gate_b, 1, b_scr.at[slot + 1],
                               2 * h + 1))
                dst += [(of_scr, rf, ls), (ob_scr, rb, ls)]
        outs, st = _hgrn_chunks(chains, st, tri_ref, pairs_ref, side_ref)
        for (scr, rows, ls), o in zip(dst, outs):
            scr[rows, ls] = o
        return tuple(st)

    if latent:
        st0 = tuple(s0_ref[d, h].T for h in range(heads) for d in range(2))
    else:
        st0 = (jnp.zeros((HEAD_DIM, HEAD_DIM), F32),) * (2 * heads)
    st = lax.fori_loop(0, nc // HGRN_CHUNKS_PER_TRIP, body, st0)
    if not latent:
        for h in range(heads):
            for d in range(2):
                s_ref[d, h] = st[2 * h + d].T

    def finish(i, carry):
        rs = pl.ds(pl.multiple_of(i * c, c), c)
        for ls in lanes:
            y_ref[rs, ls] = (_rms(of_scr[rs, ls] + ob_scr[rs, ls]) * _silu_tanh(g_ref[rs, ls])).astype(BF16)
        return carry

    lax.fori_loop(0, nc, finish, 0)


def _hgrn_constants():
    c = HGRN_CHUNK
    t = jnp.arange(c)[:, None]
    s = jnp.arange(c)[None, :]
    tri_f = (s <= t)
    pairs_f = jnp.where((t // HGRN_BAND == s // HGRN_BAND) & (s <= t), t - s, -1).astype(jnp.int32)
    for m in HGRN_LEVELS:
        h = m // 2
        pairs_f = jnp.where((t // h == s // h + 1) & ((t // h) % 2 == 1), m, pairs_f)
    tri = jnp.stack([tri_f, tri_f.T]).astype(BF16)
    pairs = jnp.stack([pairs_f, pairs_f.T])
    later_half = jnp.stack([(jnp.arange(c) % m) >= m // 2 for m in HGRN_LEVELS])
    side_f = jnp.broadcast_to(jnp.where(later_half, 1.0, -1.0)[:, :, None], (len(HGRN_LEVELS), c, HEAD_DIM))
    side = jnp.stack([side_f, -side_f]).astype(F32)
    return tri, pairs, side


def hgrn(z, lbp, l, s_all=None, s0=None):
    latent = s0 is not None
    n = DEC_SEQ if latent else SEQ
    bsz = DEC_BATCH if latent else BATCH
    blk0 = N_CTX // DEC_SEQ if latent else 0
    hb = HGRN_LAT_HEADS_PER_STEP if latent else HGRN_CTX_HEADS_PER_STEP
    wb = hb * HEAD_DIM
    tri, pairs, side = _hgrn_constants()
    tok = lambda c: pl.BlockSpec((n, wb), lambda b, h: (blk0 + b, c * (HEADS // hb) + h))
    raw = pl.BlockSpec(memory_space=pl.ANY)
    in_specs = [tok(COL_CQ), tok(COL_CFF), tok(COL_CFB), tok(COL_CI), tok(COL_CG),
                pl.BlockSpec((8, wb), lambda b, h: (0, h)),
                pl.BlockSpec((2, HGRN_CHUNK, HGRN_CHUNK), lambda b, h: (0, 0, 0)),
                pl.BlockSpec((2, HGRN_CHUNK, HGRN_CHUNK), lambda b, h: (0, 0, 0)),
                pl.BlockSpec(side.shape, lambda b, h: (0, 0, 0, 0))]
    args = [z, z, z, z, z, lbp, tri, pairs, side]
    y_spec = pl.BlockSpec((n, wb), lambda b, h: (b, h))
    y_shape = jax.ShapeDtypeStruct((bsz * n, WIDTH), BF16)
    state = pl.BlockSpec((None, None, 2, hb, HEAD_DIM, HEAD_DIM), lambda b, h: (b, l, 0, h, 0, 0))
    if latent:
        in_specs.append(state)
        args.append(s0)
        out_specs, out_shape, aliases = y_spec, y_shape, {}
    else:
        aliases = {len(args): 1}
        in_specs.append(raw)
        args.append(s_all)
        out_specs = [y_spec, state]
        out_shape = [y_shape, jax.ShapeDtypeStruct(s_all.shape, F32)]
    return pl.pallas_call(
        functools.partial(_hgrn_kernel, n=n, heads=hb, latent=latent),
        grid=(bsz, HEADS // hb),
        in_specs=in_specs, out_specs=out_specs, out_shape=out_shape,
        scratch_shapes=[pltpu.VMEM((n, wb), F32)] * 2
        + [pltpu.VMEM((2 * HGRN_CHUNKS_PER_TRIP * hb, HGRN_CHUNK, HEAD_DIM), F32)],
        input_output_aliases=aliases,
        compiler_params=_cparams(2),
        name="hgrn",
    )(*args)


def hgrn_gate_params(lb_logits):
    p = jax.nn.softmax(lb_logits.astype(F32), axis=1)
    cs = jnp.cumsum(p, axis=1)
    lb = cs - cs[:, :1]
    rows = [jnp.log(lb[0]) * LOG2E, jnp.log1p(-lb[0]) * LOG2E, 0.5 * (1.0 - lb[0]),
            jnp.log(lb[1]) * LOG2E, jnp.log1p(-lb[1]) * LOG2E, 0.5 * (1.0 - lb[1]),
            jnp.zeros_like(lb[0]), jnp.zeros_like(lb[0])]
    return jnp.stack(rows, axis=1)


def kernel(x_prompt, x_sample, cache_attn_k, cache_attn_v, state_hgrn, c, c_ctx, w_ada, b_ada, norm_g, w_in,
           a_spatial_w, a_spatial_b, nb_rpb, hgrn_lb_logits, w_up_a, w_up_b, w_up_c, w_out, w_ffn_in, w_ffn_out):
    x = (x_prompt.reshape(N_CTX, D_MODEL), x_sample.reshape(N_LAT, D_MODEL))
    cond = jnp.zeros((COND_ROWS, D_MODEL), F32).at[0].set(c_ctx).at[1:N_COND].set(c)
    mod = ada_all_layers(cond, w_ada, b_ada)
    mod = mod.reshape(DEPTH, COND_ROWS, 6, 1, D_MODEL).transpose(0, 2, 1, 3, 4)
    lbp = hgrn_gate_params(hgrn_lb_logits)
    cos, sin_signed = rope_tables()
    pair_tables = nbhd_pair_tables(nb_rpb)
    cache_k = cache_attn_k.reshape(DEC_BATCH, DEPTH, PAST_LEN, WIDTH)
    cache_v = cache_attn_v.reshape(DEC_BATCH, DEPTH, PAST_LEN, WIDTH)
    gain = lambda l, i: norm_g[l, i][None, :]
    w_out_b = w_out.astype(BF16)
    w_ffn_out_b = w_ffn_out.astype(BF16)
    w_up_a, w_up_b, w_up_c = (w.astype(BF16) for w in (w_up_a, w_up_b, w_up_c))

    k_all = jnp.zeros((BATCH, DEPTH, SEQ, WIDTH), F32)
    v_all = jnp.zeros((BATCH, DEPTH, SEQ, WIDTH), F32)
    s_all = jnp.zeros((BATCH, DEPTH, 2, HEADS, HEAD_DIM, HEAD_DIM), F32)
    h = prenorm(*x, gain(0, 0), mod[0, 1], mod[0, 0])
    for l in range(DEPTH):
        sh1, sc1, g1, sh2, sc2, g2 = (mod[l, i] for i in range(6))
        z = matmul_layer(h, w_in, l, tm=1024, tn=1024)
        y_a = mix_a(z, a_spatial_w[l], a_spatial_b[l])
        yb_ctx, k_all, v_all = ctx_attention(z, l, k_all, v_all)
        yb_lat = nbhd_attention(z, cache_k, cache_v, pair_tables[l], cos, sin_signed, l)
        yc_ctx, s_all = hgrn(z, lbp[l], l, s_all=s_all)
        yc_lat = hgrn(z, lbp[l], l, s0=state_hgrn)
        merged = merge_branches(y_a, (yb_ctx, yb_lat), (yc_ctx, yc_lat), z, w_up_a, w_up_b, w_up_c, l)
        x, h2 = proj_resnorm(merged, w_out_b, l, x, gain(l, 1), g1, nxt=(gain(l, 2), sc2, sh2))
        act = swiglu_in(h2, w_ffn_in, l)
        nxt = None if l == DEPTH - 1 else (gain(l + 1, 0), mod[l + 1, 1], mod[l + 1, 0])
        x, h = proj_resnorm(act, w_ffn_out_b, l, x, gain(l, 3), g2, nxt=nxt, tk=D_FF // 4,
                            split_out=l == DEPTH - 1)

    y_prompt = x[0].reshape(BATCH, SEQ, D_MODEL)
    y_sample = x[1].reshape(DEC_BATCH, DEC_SEQ, D_MODEL)
    kv_shape = (BATCH, DEPTH, SEQ, HEADS, HEAD_DIM)
    return (y_prompt, y_sample, k_all.reshape(kv_shape), v_all.reshape(kv_shape), s_all)
```

```python
import functools
import math

import jax
import jax.numpy as jnp
from jax import lax
from jax.experimental import pallas as pl
from jax.experimental.pallas import tpu as pltpu

F32 = jnp.float32
BF16 = jnp.bfloat16

D_MODEL = 2048
BATCH = 32
SEQ = 256
DEPTH = 4
DEC_BATCH = 2
DEC_SEQ = 1024
PAST_LEN = 512
GRID_W = 64
HEADS = 8
HEAD_DIM = 128
WIDTH = HEADS * HEAD_DIM
A_CHUNK = 128
NB_ROWS_MAX = 8
NB_COLS = 16
ROPE_THETA = 10000.0
D_FF = -(-8 * D_MODEL // (3 * 256)) * 256
NORM_EPS = 1e-6
NEG_INF = -1e30
IN_WIDTH = 10 * WIDTH + 3 * D_MODEL

N_CTX = BATCH * SEQ
N_LAT = DEC_BATCH * DEC_SEQ
N_TOK = N_CTX + N_LAT
N_COND = 1 + DEC_BATCH
COND_ROWS = 8

COL_AU, COL_AV, COL_BQ, COL_BK, COL_BV, COL_CQ, COL_CFF, COL_CFB, COL_CI, COL_CG = range(10)
COL_GATES = 10 * WIDTH

HGRN_CHUNK = 128
HGRN_BAND = 4
HGRN_LEVELS = (8, 16, 32, 64, 128)
SUBLANES = 8
HGRN_CHUNKS_PER_TRIP = 2
HGRN_CTX_HEADS_PER_STEP = 4
HGRN_LAT_HEADS_PER_STEP = 2

LOG2E = math.log2(math.e)

V7X_VMEM_BYTES = 64 * 1024 * 1024
VMEM_LIMIT = V7X_VMEM_BYTES * 3 // 4
VMEM_LIMIT_BIG = V7X_VMEM_BYTES * 7 // 8


def _cparams(n_axes, vmem=VMEM_LIMIT):
    return pltpu.CompilerParams(dimension_semantics=("arbitrary",) * n_axes, vmem_limit_bytes=vmem)


def _cond_row(i, tm):
    n_ctx_tiles = N_CTX // tm
    return jnp.where(i < n_ctx_tiles, 0, 1 + (i - n_ctx_tiles) // (DEC_SEQ // tm))


def _sigmoid(x):
    return 0.5 * (1.0 + jnp.tanh(0.5 * x))


def _silu(x):
    hx = 0.5 * x
    return hx + hx * jnp.tanh(hx)


def _gelu_tanh(x):
    c = math.sqrt(2.0 / math.pi)
    return x * (0.5 * (1.0 + jnp.tanh(c * (x + 0.044715 * (x * x * x)))))


def _rms(x):
    return x * lax.rsqrt(jnp.mean(x * x, -1, keepdims=True) + NORM_EPS)


def _ada_kernel(c_ref, w_ref, b_ref, o_ref):
    s = _silu(c_ref[...]).astype(BF16)
    o_ref[...] = jnp.dot(s, w_ref[...].astype(BF16), preferred_element_type=F32) + b_ref[...]


def ada_all_layers(cond, w_ada, b_ada):
    tn = 1024
    n_out = 6 * D_MODEL
    return pl.pallas_call(
        _ada_kernel,
        grid=(DEPTH, n_out // tn),
        in_specs=[pl.BlockSpec((COND_ROWS, D_MODEL), lambda l, j: (0, 0)),
                  pl.BlockSpec((None, D_MODEL, tn), lambda l, j: (l, 0, j)),
                  pl.BlockSpec((None, 1, tn), lambda l, j: (l, 0, j))],
        out_specs=pl.BlockSpec((None, COND_ROWS, tn), lambda l, j: (l, 0, j)),
        out_shape=jax.ShapeDtypeStruct((DEPTH, COND_ROWS, n_out), F32),
        compiler_params=_cparams(2),
        name="ada",
    )(cond, w_ada, b_ada.reshape(DEPTH, 1, n_out))


def _path_specs(tm, width=D_MODEL):
    n_ctx_tiles = N_CTX // tm
    ctx = pl.BlockSpec((tm, width), lambda i, *_: (jnp.minimum(i, n_ctx_tiles - 1), 0))
    lat = pl.BlockSpec((tm, width), lambda i, *_: (jnp.maximum(i - n_ctx_tiles, 0), 0))
    return ctx, lat


def _prenorm_kernel(xc_ref, xl_ref, g_ref, sc_ref, sh_ref, h_ref, *, n_ctx_tiles):
    x = jnp.where(pl.program_id(0) < n_ctx_tiles, xc_ref[...], xl_ref[...])
    h_ref[...] = (_rms(x) * (g_ref[...] * (1.0 + sc_ref[...])) + sh_ref[...]).astype(BF16)


def prenorm(x_ctx, x_lat, gain, sc, sh):
    tm = 512
    row = lambda i: (_cond_row(i, tm), 0, 0)
    ctx, lat = _path_specs(tm)
    return pl.pallas_call(
        functools.partial(_prenorm_kernel, n_ctx_tiles=N_CTX // tm),
        grid=(N_TOK // tm,),
        in_specs=[ctx, lat,
                  pl.BlockSpec((1, D_MODEL), lambda i: (0, 0)),
                  pl.BlockSpec((None, 1, D_MODEL), row),
                  pl.BlockSpec((None, 1, D_MODEL), row)],
        out_specs=pl.BlockSpec((tm, D_MODEL), lambda i: (i, 0)),
        out_shape=jax.ShapeDtypeStruct((N_TOK, D_MODEL), BF16),
        compiler_params=_cparams(1),
        name="prenorm",
    )(x_ctx, x_lat, gain, sc, sh)


def _proj_resnorm_kernel(a_ref, w_ref, *rest, nk, n_ctx_tiles, split_in, split_out, with_next):
    rest = list(rest)
    x_refs = [rest.pop(0) for _ in range(1 + split_in)]
    g_ref, gate_ref = rest.pop(0), rest.pop(0)
    if with_next:
        g2_ref, sc_ref, sh_ref = rest.pop(0), rest.pop(0), rest.pop(0)
    xo_refs = [rest.pop(0) for _ in range(1 + split_out)]
    if with_next:
        h_ref = rest.pop(0)
    is_ctx = pl.program_id(0) < n_ctx_tiles

    def part():
        return jnp.dot(a_ref[...], w_ref[...], preferred_element_type=F32)

    def finish(t):
        x = jnp.where(is_ctx, x_refs[0][...], x_refs[1][...]) if split_in else x_refs[0][...]
        x_new = x + _rms(t) * (gate_ref[...] * g_ref[...])
        if split_out:
            @pl.when(is_ctx)
            def _():
                xo_refs[0][...] = x_new

            @pl.when(jnp.logical_not(is_ctx))
            def _():
                xo_refs[1][...] = x_new
        else:
            xo_refs[0][...] = x_new
        if with_next:
            h_ref[...] = (_rms(x_new) * (g2_ref[...] * (1.0 + sc_ref[...])) + sh_ref[...]).astype(BF16)

    if nk == 1:
        finish(part())
        return
    acc_ref = rest[-1]
    k = pl.program_id(1)

    @pl.when(k == 0)
    def _():
        acc_ref[...] = part()

    @pl.when(k > 0)
    def _():
        acc_ref[...] += part()

    @pl.when(k == nk - 1)
    def _():
        finish(acc_ref[...])


def proj_resnorm(a, w, l, x, gain, gate, nxt=None, tk=None, split_out=False):
    tm = 512
    tk = tk or a.shape[1]
    nk = a.shape[1] // tk
    split_in = isinstance(x, tuple)
    row = lambda i, k: (_cond_row(i, tm), 0, 0)
    tile = pl.BlockSpec((tm, D_MODEL), lambda i, k: (i, 0))
    vec = pl.BlockSpec((1, D_MODEL), lambda i, k: (0, 0))
    mod = pl.BlockSpec((None, 1, D_MODEL), row)
    w_mode = dict(pipeline_mode=pl.Buffered(1)) if nk == 1 else {}
    in_specs = [pl.BlockSpec((tm, tk), lambda i, k: (i, k)),
                pl.BlockSpec((None, tk, D_MODEL), lambda i, k: (l, k, 0), **w_mode)]
    in_specs += list(_path_specs(tm)) if split_in else [tile]
    in_specs += [vec, mod]
    args = [a, w, *(x if split_in else (x,)), gain, gate]
    if nxt is not None:
        in_specs += [vec, mod, mod]
        args += list(nxt)
    if split_out:
        out_specs = list(_path_specs(tm))
        out_shape = [jax.ShapeDtypeStruct((N_CTX, D_MODEL), F32), jax.ShapeDtypeStruct((N_LAT, D_MODEL), F32)]
    else:
        out_specs = [tile]
        out_shape = [jax.ShapeDtypeStruct((N_TOK, D_MODEL), F32)]
    if nxt is not None:
        out_specs.append(tile)
        out_shape.append(jax.ShapeDtypeStruct((N_TOK, D_MODEL), BF16))
    out = pl.pallas_call(
        functools.partial(_proj_resnorm_kernel, nk=nk, n_ctx_tiles=N_CTX // tm, split_in=split_in,
                          split_out=split_out, with_next=nxt is not None),
        grid=(N_TOK // tm, nk),
        in_specs=in_specs, out_specs=out_specs, out_shape=out_shape,
        scratch_shapes=[pltpu.VMEM((tm, D_MODEL), F32)] if nk > 1 else [],
        compiler_params=_cparams(2, VMEM_LIMIT_BIG),
        name="proj_resnorm",
    )(*args)
    n_x = 1 + split_out
    x_new = tuple(out[:n_x]) if split_out else out[0]
    return x_new, (out[n_x] if nxt is not None else None)


def _mm_kernel(a_ref, w_ref, o_ref, wb_ref):
    @pl.when(pl.program_id(1) == 0)
    def _():
        wb_ref[...] = w_ref[...].astype(BF16)

    o_ref[...] = jnp.dot(a_ref[...], wb_ref[...], preferred_element_type=F32)


def matmul_layer(a, w, l, tm, tn, vmem=VMEM_LIMIT):
    m, k = a.shape
    n = w.shape[2]
    return pl.pallas_call(
        _mm_kernel,
        grid=(n // tn, m // tm),
        in_specs=[pl.BlockSpec((tm, k), lambda j, i: (i, 0)),
                  pl.BlockSpec((None, k, tn), lambda j, i: (l, 0, j))],
        out_specs=pl.BlockSpec((tm, tn), lambda j, i: (i, j)),
        out_shape=jax.ShapeDtypeStruct((m, n), F32),
        scratch_shapes=[pltpu.VMEM((k, tn), BF16)],
        compiler_params=_cparams(2, vmem),
        name="matmul",
    )(a, w)


def _merge_kernel(ya_ref, ybc_ref, ybl_ref, ycc_ref, ycl_ref, wa_ref, wb_ref, wc_ref, ga_ref, gb_ref, gc_ref,
                  o_ref, *, n_ctx_tiles):
    def run(yb_ref, yc_ref):
        acc = _sigmoid(ga_ref[...]) * jnp.dot(ya_ref[...], wa_ref[...], preferred_element_type=F32)
        acc += _sigmoid(gb_ref[...]) * jnp.dot(yb_ref[...], wb_ref[...], preferred_element_type=F32)
        acc += _sigmoid(gc_ref[...]) * jnp.dot(yc_ref[...], wc_ref[...], preferred_element_type=F32)
        o_ref[...] = acc.astype(BF16)

    pl.when(pl.program_id(0) < n_ctx_tiles)(lambda: run(ybc_ref, ycc_ref))
    pl.when(pl.program_id(0) >= n_ctx_tiles)(lambda: run(ybl_ref, ycl_ref))


def merge_branches(y_a, y_b, y_c, z, w_up_a, w_up_b, w_up_c, l):
    tm, tn = 1024, 512
    gate0 = COL_GATES // tn
    per_gate = D_MODEL // tn
    y_spec = pl.BlockSpec((tm, WIDTH), lambda i, j: (i, 0))
    ctx_spec, lat_spec = _path_specs(tm, WIDTH)
    w_spec = pl.BlockSpec((None, WIDTH, tn), lambda i, j: (l, 0, j))
    g_spec = lambda b: pl.BlockSpec((tm, tn), lambda i, j: (i, gate0 + b * per_gate + j))
    return pl.pallas_call(
        functools.partial(_merge_kernel, n_ctx_tiles=N_CTX // tm),
        grid=(N_TOK // tm, D_MODEL // tn),
        in_specs=[y_spec, ctx_spec, lat_spec, ctx_spec, lat_spec, w_spec, w_spec, w_spec,
                  g_spec(0), g_spec(1), g_spec(2)],
        out_specs=pl.BlockSpec((tm, tn), lambda i, j: (i, j)),
        out_shape=jax.ShapeDtypeStruct((N_TOK, D_MODEL), BF16),
        compiler_params=_cparams(2),
        name="merge",
    )(y_a, y_b[0], y_b[1], y_c[0], y_c[1], w_up_a, w_up_b, w_up_c, z, z, z)


def _swiglu_kernel(h_ref, wu_ref, wg_ref, o_ref, su_ref, sg_ref):
    @pl.when(pl.program_id(1) == 0)
    def _():
        su_ref[...] = wu_ref[...].astype(BF16)
        sg_ref[...] = wg_ref[...].astype(BF16)

    h = h_ref[...]
    up = jnp.dot(h, su_ref[...], preferred_element_type=F32)
    gate = jnp.dot(h, sg_ref[...], preferred_element_type=F32)
    o_ref[...] = (_silu(gate) * up).astype(BF16)


def swiglu_in(h, w_ffn_in, l):
    tm, tn = 2048, 512
    n_up = D_FF // tn
    return pl.pallas_call(
        _swiglu_kernel,
        grid=(n_up, N_TOK // tm),
        in_specs=[pl.BlockSpec((tm, D_MODEL), lambda j, i: (i, 0)),
                  pl.BlockSpec((None, D_MODEL, tn), lambda j, i: (l, 0, j)),
                  pl.BlockSpec((None, D_MODEL, tn), lambda j, i: (l, 0, n_up + j))],
        out_specs=pl.BlockSpec((tm, tn), lambda j, i: (i, j)),
        out_shape=jax.ShapeDtypeStruct((N_TOK, D_FF), BF16),
        scratch_shapes=[pltpu.VMEM((D_MODEL, tn), BF16)] * 2,
        compiler_params=_cparams(2, VMEM_LIMIT_BIG),
        name="swiglu_in",
    )(h, w_ffn_in, w_ffn_in)


def _mix_a_kernel(u_ref, v_ref, w_ref, bt_ref, o_ref, *, rows):
    w = w_ref[...].astype(BF16)
    for c in range(rows // A_CHUNK):
        rs = slice(c * A_CHUNK, (c + 1) * A_CHUNK)
        u = _gelu_tanh(u_ref[rs, :])
        v = _gelu_tanh(v_ref[rs, :])
        mu = jnp.mean(v, -1, keepdims=True)
        vc = v - mu
        vn = (vc * lax.rsqrt(jnp.mean(vc * vc, -1, keepdims=True) + NORM_EPS)).astype(BF16)
        for g in range(HEADS):
            cs = slice(g * HEAD_DIM, (g + 1) * HEAD_DIM)
            zg = jnp.dot(w[g], vn[:, cs], preferred_element_type=F32) + bt_ref[:, g:g + 1]
            o_ref[rs, cs] = (u[:, cs] * zg).astype(BF16)


def mix_a(z, a_w_l, a_b_l):
    rows = 512
    return pl.pallas_call(
        functools.partial(_mix_a_kernel, rows=rows),
        grid=(N_TOK // rows,),
        in_specs=[pl.BlockSpec((rows, WIDTH), lambda i: (i, COL_AU)),
                  pl.BlockSpec((rows, WIDTH), lambda i: (i, COL_AV)),
                  pl.BlockSpec((HEADS, A_CHUNK, A_CHUNK), lambda i: (0, 0, 0)),
                  pl.BlockSpec((A_CHUNK, HEADS), lambda i: (0, 0))],
        out_specs=pl.BlockSpec((rows, WIDTH), lambda i: (i, 0)),
        out_shape=jax.ShapeDtypeStruct((N_TOK, WIDTH), BF16),
        compiler_params=_cparams(1),
        name="mix_a",
    )(z, z, a_w_l, a_b_l.T)


def _softmax_pv(scores, values):
    m = scores[0].max(-1, keepdims=True)
    for s in scores[1:]:
        m = jnp.maximum(m, s.max(-1, keepdims=True))
    den = 0.0
    acc = 0.0
    for s, v in zip(scores, values):
        p = jnp.exp(s - m)
        den = den + p.sum(-1, keepdims=True)
        acc = acc + jnp.dot(p.astype(BF16), v, preferred_element_type=F32)
    return acc / den


def _qkt(q, k):
    return lax.dot_general(q, k, (((1,), (1,)), ((), ())), preferred_element_type=F32)


def _ctx_attn_kernel(q_ref, k_ref, v_ref, *rest):
    o_ref, ko_ref, vo_ref = rest[-3:]
    scale = HEAD_DIM ** -0.5
    ko_ref[...] = k_ref[...]
    vo_ref[...] = v_ref[...]
    for h in range(HEADS):
        cs = slice(h * HEAD_DIM, (h + 1) * HEAD_DIM)
        q = q_ref[:, cs].astype(BF16)
        k = k_ref[:, cs].astype(BF16)
        v = v_ref[:, cs].astype(BF16)
        o_ref[:, cs] = _softmax_pv([_qkt(q, k) * scale], [v]).astype(BF16)


def ctx_attention(z, l, k_all, v_all):
    spec = lambda col: pl.BlockSpec((SEQ, WIDTH), lambda b: (b, col))
    kv_spec = pl.BlockSpec((None, None, SEQ, WIDTH), lambda b: (b, l, 0, 0))
    kv_shape = jax.ShapeDtypeStruct((BATCH, DEPTH, SEQ, WIDTH), F32)
    raw = pl.BlockSpec(memory_space=pl.ANY)
    return pl.pallas_call(
        _ctx_attn_kernel,
        grid=(BATCH,),
        in_specs=[spec(COL_BQ), spec(COL_BK), spec(COL_BV), raw, raw],
        out_specs=[pl.BlockSpec((SEQ, WIDTH), lambda b: (b, 0)), kv_spec, kv_spec],
        out_shape=[jax.ShapeDtypeStruct((N_CTX, WIDTH), BF16), kv_shape, kv_shape],
        input_output_aliases={3: 1, 4: 2},
        compiler_params=_cparams(1),
        name="ctx_attn",
    )(z, z, z, k_all, v_all)


ROPE_AXIS = HEAD_DIM // 2
ROPE_PAIR = ROPE_AXIS // 2


def _rope(x, cos, sin_signed, first_of_pair):
    partner = jnp.where(first_of_pair, pltpu.roll(x, HEAD_DIM - ROPE_PAIR, 1), pltpu.roll(x, ROPE_PAIR, 1))
    return x * cos + partner * sin_signed


NB_GRID_ROWS = DEC_SEQ // GRID_W
NB_WIN_ROWS = min(NB_ROWS_MAX, NB_GRID_ROWS)
NB_Q_ROWS = 4


def _nb_row_start(qr):
    return min(max(qr - NB_WIN_ROWS // 2, 0), NB_GRID_ROWS - NB_WIN_ROWS)


def _nb_tiles():
    tiles = []
    for q0 in range(0, NB_GRID_ROWS, NB_Q_ROWS):
        starts = [_nb_row_start(qr) for qr in range(q0, q0 + NB_Q_ROWS)]
        lo, hi = min(starts), max(starts) + NB_WIN_ROWS
        tiles.append((q0, lo - lo % 2, hi + hi % 2))
    return tiles


def _nbhd_attn_kernel(q_ref, k_ref, v_ref, kc_ref, vc_ref, pair_ref, cos_ref, sin_ref, o_ref, bias_scr):
    scale = HEAD_DIM ** -0.5
    lane = lax.broadcasted_iota(jnp.int32, (1, HEAD_DIM), 1)
    first_quarter = (lane % ROPE_AXIS) < ROPE_PAIR

    @pl.when(pl.program_id(1) == 0)
    def _():
        left = lane < GRID_W
        neg = jnp.full((GRID_W, 2 * GRID_W), NEG_INF, F32)
        for q0, lo, hi in _nb_tiles():
            for qr in range(q0, q0 + NB_Q_ROWS):
                r0 = _nb_row_start(qr)
                for p in range(lo // 2, hi // 2):
                    ok0 = r0 <= 2 * p < r0 + NB_WIN_ROWS
                    ok1 = r0 <= 2 * p + 1 < r0 + NB_WIN_ROWS
                    tile = neg
                    if ok0 or ok1:
                        tile = pair_ref[2 * p - qr + NB_ROWS_MAX]
                        if not ok0:
                            tile = jnp.where(left, NEG_INF, tile)
                        if not ok1:
                            tile = jnp.where(left, tile, NEG_INF)
                    bias_scr[qr * GRID_W:(qr + 1) * GRID_W, p * 2 * GRID_W:(p + 1) * 2 * GRID_W] = tile

    k = _rope(k_ref[...], cos_ref[...], sin_ref[...], first_quarter).astype(BF16)
    v = v_ref[...].astype(BF16)
    kc = kc_ref[...].astype(BF16)
    vc = vc_ref[...].astype(BF16)
    for q0, lo, hi in _nb_tiles():
        rs = slice(q0 * GRID_W, (q0 + NB_Q_ROWS) * GRID_W)
        ks = slice(lo * GRID_W, hi * GRID_W)
        q = _rope(q_ref[rs, :], cos_ref[rs, :], sin_ref[rs, :], first_quarter).astype(BF16)
        s_loc = _qkt(q, k[ks]) * scale + bias_scr[rs, ks]
        s_ctx = _qkt(q, kc) * scale
        o_ref[rs, :] = _softmax_pv([s_loc, s_ctx], [v[ks], vc]).astype(BF16)


def nbhd_attention(z, cache_k, cache_v, pair_table, cos, sin_signed, l):
    lat0 = N_CTX // DEC_SEQ
    col = lambda c: (lambda h, b: (lat0 + b, c * HEADS + h))
    tok = lambda c: pl.BlockSpec((DEC_SEQ, HEAD_DIM), col(c))
    cache = pl.BlockSpec((None, None, PAST_LEN, HEAD_DIM), lambda h, b: (b, l, 0, h))
    table = pl.BlockSpec((DEC_SEQ, HEAD_DIM), lambda h, b: (0, 0))
    pairs = pl.BlockSpec((None, 2 * NB_ROWS_MAX, GRID_W, 2 * GRID_W), lambda h, b: (h, 0, 0, 0))
    return pl.pallas_call(
        _nbhd_attn_kernel,
        grid=(HEADS, DEC_BATCH),
        in_specs=[tok(COL_BQ), tok(COL_BK), tok(COL_BV), cache, cache, pairs, table, table],
        out_specs=pl.BlockSpec((DEC_SEQ, HEAD_DIM), lambda h, b: (b, h)),
        out_shape=jax.ShapeDtypeStruct((N_LAT, WIDTH), BF16),
        scratch_shapes=[pltpu.VMEM((DEC_SEQ, DEC_SEQ), F32)],
        compiler_params=_cparams(2),
        name="nbhd_attn",
    )(z, z, z, cache_k, cache_v, pair_table, cos, sin_signed)


def nbhd_pair_tables(rpb):
    cols = jnp.arange(GRID_W)
    c_start = jnp.clip(cols - NB_COLS // 2, 0, GRID_W - NB_COLS)
    col_ok = (cols[None, :] >= c_start[:, None]) & (cols[None, :] < c_start[:, None] + NB_COLS)
    dc_idx = jnp.clip(cols[None, :] - cols[:, None] + NB_COLS - 1, 0, 2 * NB_COLS - 2)
    onehot = (dc_idx[None] == jnp.arange(2 * NB_COLS - 1)[:, None, None]).astype(F32)
    t = jnp.einsum('lhrd,dqk->lhrqk', rpb.astype(F32), onehot, precision=lax.Precision.HIGHEST)
    t = jnp.where(col_ok, t, NEG_INF)
    neg = jnp.full((DEPTH, HEADS, 1, GRID_W, GRID_W), NEG_INF, F32)
    tp = jnp.concatenate([neg, t, neg], 2)
    return jnp.concatenate([tp[:, :, :-1], tp[:, :, 1:]], -1)


def rope_tables():
    ax = HEAD_DIM // 2
    pos = jnp.arange(DEC_SEQ)
    inv = ROPE_THETA ** (-jnp.arange(0, ax, 2, dtype=F32) / ax)

    def ang(p):
        a = p.astype(F32)[:, None] * inv[None, :]
        return jnp.concatenate([a, a], -1)

    a = jnp.concatenate([ang(pos // GRID_W), ang(pos % GRID_W)], -1)
    sign = jnp.where((jnp.arange(HEAD_DIM) % ax) < ax // 2, -1.0, 1.0).astype(F32)
    return jnp.cos(a), jnp.sin(a) * sign[None, :]


def _split3(x):
    hi = x.astype(BF16)
    r = x - hi.astype(F32)
    mid = r.astype(BF16)
    lo = (r - mid.astype(F32)).astype(BF16)
    return hi, mid, lo


def _hgrn_chunks(chains, st, tri_ref, pairs_ref, side_ref):
    c = HGRN_CHUNK
    st = list(st)
    qs, kk, ff, bb, vb = [], [], [], [], []
    for q, zf, v, (log2lb, log21mlb, half1mlb), d, b_scr, _ in chains:
        z2 = zf * LOG2E
        ls = jnp.minimum(z2, 0.0) - jnp.log2(1.0 + jnp.exp2(-jnp.abs(z2)))
        x2 = log21mlb + ls
        lf = jnp.maximum(log2lb, x2) + jnp.log2(1.0 + jnp.exp2(-jnp.abs(log2lb - x2)))
        k = half1mlb * (1.0 - jnp.tanh(0.5 * zf))
        b = sum(jnp.dot(tri_ref[d], p, preferred_element_type=F32) for p in _split3(lf))
        b_scr[...] = b
        qs.append(_silu(q))
        kk.append(k)
        ff.append(1.0 - k)
        bb.append(b)
        vb.append(v.astype(BF16))

    outs = []
    for i, (_, _, _, _, d, b_scr, s) in enumerate(chains):
        end = 0 if d else c - 1
        b_end = b_scr[end:end + 1, :]
        k_dec = (kk[i] * jnp.exp2(b_end - bb[i])).astype(BF16)
        st_add = lax.dot_general(vb[i], k_dec, (((0,), (0,)), ((), ())), preferred_element_type=F32)
        if st[s] is None:
            outs.append(None)
            st[s] = st_add
        else:
            outs.append(_qkt((qs[i] * jnp.exp2(bb[i])).astype(BF16), st[s].astype(BF16)))
            st[s] = st[s] * jnp.exp2(b_end) + st_add

    attn = []
    for i, (_, _, _, _, d, _, _) in enumerate(chains):
        a = jnp.zeros((c, c), F32)
        kd = kk[i]
        for delta in range(HGRN_BAND):
            if delta:
                kd3 = kd.reshape(c // SUBLANES, SUBLANES, HEAD_DIM)
                kd = ff[i] * pltpu.roll(kd3, SUBLANES - 1 if d else 1, 1).reshape(c, HEAD_DIM)
            a = jnp.where(pairs_ref[d] == delta, (qs[i] * kd).sum(-1, keepdims=True), a)
        attn.append(a)

    qb = [x.astype(BF16) for x in qs]
    kb = [x.astype(BF16) for x in kk]
    for lv, m in enumerate(HGRN_LEVELS):
        for i, (_, _, _, _, d, b_scr, _) in enumerate(chains):
            seam = m // 2 if d else m // 2 - 1
            g = jnp.concatenate([jnp.broadcast_to(b_scr[j * m + seam:j * m + seam + 1, :], (m, HEAD_DIM))
                                 for j in range(c // m)], 0)
            e = jnp.exp2((bb[i] - g) * side_ref[d, lv]).astype(BF16)
            attn[i] = jnp.where(pairs_ref[d] == m, _qkt(qb[i] * e, kb[i] * e), attn[i])
    for i in range(len(chains)):
        o_intra = jnp.dot(attn[i].astype(BF16), vb[i], preferred_element_type=F32)
        outs[i] = o_intra if outs[i] is None else outs[i] + o_intra
    return outs, st


def _hgrn_kernel(q_ref, ff_ref, fb_ref, v_ref, g_ref, lbp_ref, tri_ref, pairs_ref, side_ref, *rest,
                 n, heads, latent):
    of_scr, ob_scr, b_scr = rest[-3:]
    if latent:
        s0_ref, y_ref = rest[:2]
    else:
        y_ref, s_ref = rest[1:3]
    c = HGRN_CHUNK
    nc = n // c
    lanes = [slice(h * HEAD_DIM, (h + 1) * HEAD_DIM) for h in range(heads)]

    def body(i, st):
        chains, dst = [], []
        for u in range(HGRN_CHUNKS_PER_TRIP):
            cf = i * HGRN_CHUNKS_PER_TRIP + u
            start = (lambda r: r) if isinstance(cf, int) else (lambda r: pl.multiple_of(r, c))
            rf = pl.ds(start(cf * c), c)
            rb = pl.ds(start((nc - 1 - cf) * c), c)
            for h, ls in enumerate(lanes):
                gate_f = [lbp_ref[r:r + 1, ls] for r in range(3)]
                gate_b = [lbp_ref[3 + r:4 + r, ls] for r in range(3)]
                slot = 2 * (u * heads + h)
                chains.append((q_ref[rf, ls], ff_ref[rf, ls], v_ref[rf, ls], gate_f, 0, b_scr.at[slot], 2 * h))
                chains.append((q_ref[rb, ls], fb_ref[rb, ls], v_ref[rb, ls], gate_b, 1, b_scr.at[slot + 1],
                               2 * h + 1))
                dst += [(of_scr, rf, ls), (ob_scr, rb, ls)]
        outs, st = _hgrn_chunks(chains, st, tri_ref, pairs_ref, side_ref)
        for (scr, rows, ls), o in zip(dst, outs):
            scr[rows, ls] = o
        return tuple(st)

    trips = nc // HGRN_CHUNKS_PER_TRIP
    if latent:
        st = lax.fori_loop(0, trips, body, tuple(s0_ref[d, h].T for h in range(heads) for d in range(2)))
    else:
        st = body(0, (None,) * (2 * heads))
        if trips > 1:
            st = lax.fori_loop(1, trips, body, st)
    if not latent:
        for h in range(heads):
            for d in range(2):
                s_ref[d, h] = st[2 * h + d].T

    def finish(i, carry):
        rs = pl.ds(pl.multiple_of(i * c, c), c)
        for ls in lanes:
            y_ref[rs, ls] = (_rms(of_scr[rs, ls] + ob_scr[rs, ls]) * _silu(g_ref[rs, ls])).astype(BF16)
        return carry

    lax.fori_loop(0, nc, finish, 0)


def _hgrn_constants():
    c = HGRN_CHUNK
    t = jnp.arange(c)[:, None]
    s = jnp.arange(c)[None, :]
    tri_f = (s <= t)
    pairs_f = jnp.where((t // HGRN_BAND == s // HGRN_BAND) & (s <= t), t - s, -1).astype(jnp.int32)
    for m in HGRN_LEVELS:
        h = m // 2
        pairs_f = jnp.where((t // h == s // h + 1) & ((t // h) % 2 == 1), m, pairs_f)
    tri = jnp.stack([tri_f, tri_f.T]).astype(BF16)
    pairs = jnp.stack([pairs_f, pairs_f.T])
    later_half = jnp.stack([(jnp.arange(c) % m) >= m // 2 for m in HGRN_LEVELS])
    side_f = jnp.broadcast_to(jnp.where(later_half, 1.0, -1.0)[:, :, None], (len(HGRN_LEVELS), c, HEAD_DIM))
    side = jnp.stack([side_f, -side_f]).astype(F32)
    return tri, pairs, side


def hgrn(z, lbp, l, s_all=None, s0=None):
    latent = s0 is not None
    n = DEC_SEQ if latent else SEQ
    bsz = DEC_BATCH if latent else BATCH
    blk0 = N_CTX // DEC_SEQ if latent else 0
    hb = HGRN_LAT_HEADS_PER_STEP if latent else HGRN_CTX_HEADS_PER_STEP
    wb = hb * HEAD_DIM
    tri, pairs, side = _hgrn_constants()
    tok = lambda c: pl.BlockSpec((n, wb), lambda b, h: (blk0 + b, c * (HEADS // hb) + h))
    raw = pl.BlockSpec(memory_space=pl.ANY)
    in_specs = [tok(COL_CQ), tok(COL_CFF), tok(COL_CFB), tok(COL_CI), tok(COL_CG),
                pl.BlockSpec((8, wb), lambda b, h: (0, h)),
                pl.BlockSpec((2, HGRN_CHUNK, HGRN_CHUNK), lambda b, h: (0, 0, 0)),
                pl.BlockSpec((2, HGRN_CHUNK, HGRN_CHUNK), lambda b, h: (0, 0, 0)),
                pl.BlockSpec(side.shape, lambda b, h: (0, 0, 0, 0))]
    args = [z, z, z, z, z, lbp, tri, pairs, side]
    y_spec = pl.BlockSpec((n, wb), lambda b, h: (b, h))
    y_shape = jax.ShapeDtypeStruct((bsz * n, WIDTH), BF16)
    state = pl.BlockSpec((None, None, 2, hb, HEAD_DIM, HEAD_DIM), lambda b, h: (b, l, 0, h, 0, 0))
    if latent:
        in_specs.append(state)
        args.append(s0)
        out_specs, out_shape, aliases = y_spec, y_shape, {}
    else:
        aliases = {len(args): 1}
        in_specs.append(raw)
        args.append(s_all)
        out_specs = [y_spec, state]
        out_shape = [y_shape, jax.ShapeDtypeStruct(s_all.shape, F32)]
    return pl.pallas_call(
        functools.partial(_hgrn_kernel, n=n, heads=hb, latent=latent),
        grid=(bsz, HEADS // hb),
        in_specs=in_specs, out_specs=out_specs, out_shape=out_shape,
        scratch_shapes=[pltpu.VMEM((n, wb), F32)] * 2
        + [pltpu.VMEM((2 * HGRN_CHUNKS_PER_TRIP * hb, HGRN_CHUNK, HEAD_DIM), F32)],
        input_output_aliases=aliases,
        compiler_params=_cparams(2),
        name="hgrn",
    )(*args)


def hgrn_gate_params(lb_logits):
    p = jax.nn.softmax(lb_logits.astype(F32), axis=1)
    cs = jnp.cumsum(p, axis=1)
    lb = cs - cs[:, :1]
    rows = [jnp.log(lb[0]) * LOG2E, jnp.log1p(-lb[0]) * LOG2E, 0.5 * (1.0 - lb[0]),
            jnp.log(lb[1]) * LOG2E, jnp.log1p(-lb[1]) * LOG2E, 0.5 * (1.0 - lb[1]),
            jnp.zeros_like(lb[0]), jnp.zeros_like(lb[0])]
    return jnp.stack(rows, axis=1)


def kernel(x_prompt, x_sample, cache_attn_k, cache_attn_v, state_hgrn, c, c_ctx, w_ada, b_ada, norm_g, w_in,
           a_spatial_w, a_spatial_b, nb_rpb, hgrn_lb_logits, w_up_a, w_up_b, w_up_c, w_out, w_ffn_in, w_ffn_out):
    x = (x_prompt.reshape(N_CTX, D_MODEL), x_sample.reshape(N_LAT, D_MODEL))
    cond = jnp.zeros((COND_ROWS, D_MODEL), F32).at[0].set(c_ctx).at[1:N_COND].set(c)
    mod = ada_all_layers(cond, w_ada, b_ada)
    mod = mod.reshape(DEPTH, COND_ROWS, 6, 1, D_MODEL).transpose(0, 2, 1, 3, 4)
    lbp = hgrn_gate_params(hgrn_lb_logits)
    cos, sin_signed = rope_tables()
    pair_tables = nbhd_pair_tables(nb_rpb)
    cache_k = cache_attn_k.reshape(DEC_BATCH, DEPTH, PAST_LEN, WIDTH)
    cache_v = cache_attn_v.reshape(DEC_BATCH, DEPTH, PAST_LEN, WIDTH)
    gain = lambda l, i: norm_g[l, i][None, :]
    w_out_b = w_out.astype(BF16)
    w_ffn_out_b = w_ffn_out.astype(BF16)
    w_up_a, w_up_b, w_up_c = (w.astype(BF16) for w in (w_up_a, w_up_b, w_up_c))

    k_all = jnp.zeros((BATCH, DEPTH, SEQ, WIDTH), F32)
    v_all = jnp.zeros((BATCH, DEPTH, SEQ, WIDTH), F32)
    s_all = jnp.zeros((BATCH, DEPTH, 2, HEADS, HEAD_DIM, HEAD_DIM), F32)
    h = prenorm(*x, gain(0, 0), mod[0, 1], mod[0, 0])
    for l in range(DEPTH):
        sh1, sc1, g1, sh2, sc2, g2 = (mod[l, i] for i in range(6))
        z = matmul_layer(h, w_in, l, tm=1024, tn=1024)
        y_a = mix_a(z, a_spatial_w[l], a_spatial_b[l])
        yb_ctx, k_all, v_all = ctx_attention(z, l, k_all, v_all)
        yb_lat = nbhd_attention(z, cache_k, cache_v, pair_tables[l], cos, sin_signed, l)
        yc_ctx, s_all = hgrn(z, lbp[l], l, s_all=s_all)
        yc_lat = hgrn(z, lbp[l], l, s0=state_hgrn)
        merged = merge_branches(y_a, (yb_ctx, yb_lat), (yc_ctx, yc_lat), z, w_up_a, w_up_b, w_up_c, l)
        x, h2 = proj_resnorm(merged, w_out_b, l, x, gain(l, 1), g1, nxt=(gain(l, 2), sc2, sh2))
        act = swiglu_in(h2, w_ffn_in, l)
        nxt = None if l == DEPTH - 1 else (gain(l + 1, 0), mod[l + 1, 1], mod[l + 1, 0])
        x, h = proj_resnorm(act, w_ffn_out_b, l, x, gain(l, 3), g2, nxt=nxt, tk=D_FF // 4,
                            split_out=l == DEPTH - 1)

    y_prompt = x[0].reshape(BATCH, SEQ, D_MODEL)
    y_sample = x[1].reshape(DEC_BATCH, DEC_SEQ, D_MODEL)
    kv_shape = (BATCH, DEPTH, SEQ, HEADS, HEAD_DIM)
    return (y_prompt, y_sample, k_all.reshape(kv_shape), v_all.reshape(kv_shape), s_all)
```

```python
import functools
import math

import jax
import jax.numpy as jnp
from jax import lax
from jax.experimental import pallas as pl
from jax.experimental.pallas import tpu as pltpu

F32 = jnp.float32
BF16 = jnp.bfloat16

D_MODEL = 2048
BATCH = 32
SEQ = 256
DEPTH = 4
DEC_BATCH = 2
DEC_SEQ = 1024
PAST_LEN = 512
GRID_W = 64
HEADS = 8
HEAD_DIM = 128
WIDTH = HEADS * HEAD_DIM
A_CHUNK = 128
NB_ROWS_MAX = 8
NB_COLS = 16
ROPE_THETA = 10000.0
D_FF = -(-8 * D_MODEL // (3 * 256)) * 256
NORM_EPS = 1e-6
NEG_INF = -1e30
IN_WIDTH = 10 * WIDTH + 3 * D_MODEL

N_CTX = BATCH * SEQ
N_LAT = DEC_BATCH * DEC_SEQ
N_TOK = N_CTX + N_LAT
N_COND = 1 + DEC_BATCH
COND_ROWS = 8

COL_AU, COL_AV, COL_BQ, COL_BK, COL_BV, COL_CQ, COL_CFF, COL_CFB, COL_CI, COL_CG = range(10)
COL_GATES = 10 * WIDTH

HGRN_CHUNK = 128
HGRN_BAND = 4
HGRN_LEVELS = (8, 16, 32, 64, 128)
SUBLANES = 8
HGRN_CHUNKS_PER_TRIP = 2
HGRN_CTX_HEADS_PER_STEP = 4
HGRN_LAT_HEADS_PER_STEP = 2

LOG2E = math.log2(math.e)

MXU_TILE = 256
V7X_VMEM_BYTES = 64 * 1024 * 1024
VMEM_LIMIT = V7X_VMEM_BYTES * 3 // 4
VMEM_LIMIT_BIG = V7X_VMEM_BYTES * 7 // 8


def _cparams(n_axes, vmem=VMEM_LIMIT):
    return pltpu.CompilerParams(dimension_semantics=("arbitrary",) * n_axes, vmem_limit_bytes=vmem)


def _cond_row(i, tm):
    n_ctx_tiles = N_CTX // tm
    return jnp.where(i < n_ctx_tiles, 0, 1 + (i - n_ctx_tiles) // (DEC_SEQ // tm))


def _sigmoid(x):
    return 0.5 * (1.0 + jnp.tanh(0.5 * x))


def _silu(x):
    hx = 0.5 * x
    return hx + hx * jnp.tanh(hx)


def _gelu_tanh(x):
    c = math.sqrt(2.0 / math.pi)
    return x * (0.5 * (1.0 + jnp.tanh(c * (x + 0.044715 * (x * x * x)))))


def _rms(x):
    return x * lax.rsqrt(jnp.mean(x * x, -1, keepdims=True) + NORM_EPS)


def _ada_kernel(c_ref, w_ref, b_ref, o_ref):
    s = _silu(c_ref[...]).astype(BF16)
    o_ref[...] = jnp.dot(s, w_ref[...].astype(BF16), preferred_element_type=F32) + b_ref[...]


def ada_all_layers(cond, w_ada, b_ada):
    tn = 1024
    n_out = 6 * D_MODEL
    return pl.pallas_call(
        _ada_kernel,
        grid=(DEPTH, n_out // tn),
        in_specs=[pl.BlockSpec((COND_ROWS, D_MODEL), lambda l, j: (0, 0)),
                  pl.BlockSpec((None, D_MODEL, tn), lambda l, j: (l, 0, j)),
                  pl.BlockSpec((None, 1, tn), lambda l, j: (l, 0, j))],
        out_specs=pl.BlockSpec((None, COND_ROWS, tn), lambda l, j: (l, 0, j)),
        out_shape=jax.ShapeDtypeStruct((DEPTH, COND_ROWS, n_out), F32),
        compiler_params=_cparams(2),
        name="ada",
    )(cond, w_ada, b_ada.reshape(DEPTH, 1, n_out))


def _path_specs(tm, width=D_MODEL):
    n_ctx_tiles = N_CTX // tm
    ctx = pl.BlockSpec((tm, width), lambda i, *_: (jnp.minimum(i, n_ctx_tiles - 1), 0))
    lat = pl.BlockSpec((tm, width), lambda i, *_: (jnp.maximum(i - n_ctx_tiles, 0), 0))
    return ctx, lat


def _prenorm_kernel(xc_ref, xl_ref, g_ref, sc_ref, sh_ref, h_ref, *, n_ctx_tiles):
    x = jnp.where(pl.program_id(0) < n_ctx_tiles, xc_ref[...], xl_ref[...])
    h_ref[...] = (_rms(x) * (g_ref[...] * (1.0 + sc_ref[...])) + sh_ref[...]).astype(BF16)


def prenorm(x_ctx, x_lat, gain, sc, sh):
    tm = 512
    row = lambda i: (_cond_row(i, tm), 0, 0)
    ctx, lat = _path_specs(tm)
    return pl.pallas_call(
        functools.partial(_prenorm_kernel, n_ctx_tiles=N_CTX // tm),
        grid=(N_TOK // tm,),
        in_specs=[ctx, lat,
                  pl.BlockSpec((1, D_MODEL), lambda i: (0, 0)),
                  pl.BlockSpec((None, 1, D_MODEL), row),
                  pl.BlockSpec((None, 1, D_MODEL), row)],
        out_specs=pl.BlockSpec((tm, D_MODEL), lambda i: (i, 0)),
        out_shape=jax.ShapeDtypeStruct((N_TOK, D_MODEL), BF16),
        compiler_params=_cparams(1),
        name="prenorm",
    )(x_ctx, x_lat, gain, sc, sh)


def _proj_resnorm_kernel(a_ref, w_ref, *rest, nk, n_ctx_tiles, split_in, split_out, with_next):
    rest = list(rest)
    x_refs = [rest.pop(0) for _ in range(1 + split_in)]
    g_ref, gate_ref = rest.pop(0), rest.pop(0)
    if with_next:
        g2_ref, sc_ref, sh_ref = rest.pop(0), rest.pop(0), rest.pop(0)
    xo_refs = [rest.pop(0) for _ in range(1 + split_out)]
    if with_next:
        h_ref = rest.pop(0)
    is_ctx = pl.program_id(0) < n_ctx_tiles

    def part():
        return jnp.dot(a_ref[...], w_ref[...], preferred_element_type=F32)

    def finish(t):
        x = jnp.where(is_ctx, x_refs[0][...], x_refs[1][...]) if split_in else x_refs[0][...]
        x_new = x + _rms(t) * (gate_ref[...] * g_ref[...])
        if split_out:
            @pl.when(is_ctx)
            def _():
                xo_refs[0][...] = x_new

            @pl.when(jnp.logical_not(is_ctx))
            def _():
                xo_refs[1][...] = x_new
        else:
            xo_refs[0][...] = x_new
        if with_next:
            h_ref[...] = (_rms(x_new) * (g2_ref[...] * (1.0 + sc_ref[...])) + sh_ref[...]).astype(BF16)

    if nk == 1:
        finish(part())
        return
    acc_ref = rest[-1]
    k = pl.program_id(1)

    @pl.when(k == 0)
    def _():
        acc_ref[...] = part()

    @pl.when(k > 0)
    def _():
        acc_ref[...] += part()

    @pl.when(k == nk - 1)
    def _():
        finish(acc_ref[...])


def proj_resnorm(a, w, l, x, gain, gate, nxt=None, tk=None, split_out=False):
    tm = 512
    tk = tk or a.shape[1]
    nk = a.shape[1] // tk
    split_in = isinstance(x, tuple)
    row = lambda i, k: (_cond_row(i, tm), 0, 0)
    tile = pl.BlockSpec((tm, D_MODEL), lambda i, k: (i, 0))
    vec = pl.BlockSpec((1, D_MODEL), lambda i, k: (0, 0))
    mod = pl.BlockSpec((None, 1, D_MODEL), row)
    w_mode = dict(pipeline_mode=pl.Buffered(1)) if nk == 1 else {}
    in_specs = [pl.BlockSpec((tm, tk), lambda i, k: (i, k)),
                pl.BlockSpec((None, tk, D_MODEL), lambda i, k: (l, k, 0), **w_mode)]
    in_specs += list(_path_specs(tm)) if split_in else [tile]
    in_specs += [vec, mod]
    args = [a, w, *(x if split_in else (x,)), gain, gate]
    if nxt is not None:
        in_specs += [vec, mod, mod]
        args += list(nxt)
    if split_out:
        out_specs = list(_path_specs(tm))
        out_shape = [jax.ShapeDtypeStruct((N_CTX, D_MODEL), F32), jax.ShapeDtypeStruct((N_LAT, D_MODEL), F32)]
    else:
        out_specs = [tile]
        out_shape = [jax.ShapeDtypeStruct((N_TOK, D_MODEL), F32)]
    if nxt is not None:
        out_specs.append(tile)
        out_shape.append(jax.ShapeDtypeStruct((N_TOK, D_MODEL), BF16))
    out = pl.pallas_call(
        functools.partial(_proj_resnorm_kernel, nk=nk, n_ctx_tiles=N_CTX // tm, split_in=split_in,
                          split_out=split_out, with_next=nxt is not None),
        grid=(N_TOK // tm, nk),
        in_specs=in_specs, out_specs=out_specs, out_shape=out_shape,
        scratch_shapes=[pltpu.VMEM((tm, D_MODEL), F32)] if nk > 1 else [],
        compiler_params=_cparams(2, VMEM_LIMIT_BIG),
        name="proj_resnorm",
    )(*args)
    n_x = 1 + split_out
    x_new = tuple(out[:n_x]) if split_out else out[0]
    return x_new, (out[n_x] if nxt is not None else None)


def _mm_kernel(a_ref, w_ref, o_ref, wb_ref):
    @pl.when(pl.program_id(1) == 0)
    def _():
        wb_ref[...] = w_ref[...].astype(BF16)

    o_ref[...] = jnp.dot(a_ref[...], wb_ref[...], preferred_element_type=F32)


def matmul_layer(a, w, l, tm, tn, vmem=VMEM_LIMIT):
    m, k = a.shape
    n = w.shape[2]
    return pl.pallas_call(
        _mm_kernel,
        grid=(n // tn, m // tm),
        in_specs=[pl.BlockSpec((tm, k), lambda j, i: (i, 0)),
                  pl.BlockSpec((None, k, tn), lambda j, i: (l, 0, j))],
        out_specs=pl.BlockSpec((tm, tn), lambda j, i: (i, j)),
        out_shape=jax.ShapeDtypeStruct((m, n), F32),
        scratch_shapes=[pltpu.VMEM((k, tn), BF16)],
        compiler_params=_cparams(2, vmem),
        name="matmul",
    )(a, w)


def _merge_kernel(ya_ref, ybc_ref, ybl_ref, ycc_ref, ycl_ref, wa_ref, wb_ref, wc_ref, ga_ref, gb_ref, gc_ref,
                  o_ref, *, n_ctx_tiles):
    def run(yb_ref, yc_ref):
        acc = _sigmoid(ga_ref[...]) * jnp.dot(ya_ref[...], wa_ref[...], preferred_element_type=F32)
        acc += _sigmoid(gb_ref[...]) * jnp.dot(yb_ref[...], wb_ref[...], preferred_element_type=F32)
        acc += _sigmoid(gc_ref[...]) * jnp.dot(yc_ref[...], wc_ref[...], preferred_element_type=F32)
        o_ref[...] = acc.astype(BF16)

    pl.when(pl.program_id(0) < n_ctx_tiles)(lambda: run(ybc_ref, ycc_ref))
    pl.when(pl.program_id(0) >= n_ctx_tiles)(lambda: run(ybl_ref, ycl_ref))


def merge_branches(y_a, y_b, y_c, z, w_up_a, w_up_b, w_up_c, l):
    tm, tn = 1024, 512
    gate0 = COL_GATES // tn
    per_gate = D_MODEL // tn
    y_spec = pl.BlockSpec((tm, WIDTH), lambda i, j: (i, 0))
    ctx_spec, lat_spec = _path_specs(tm, WIDTH)
    w_spec = pl.BlockSpec((None, WIDTH, tn), lambda i, j: (l, 0, j))
    g_spec = lambda b: pl.BlockSpec((tm, tn), lambda i, j: (i, gate0 + b * per_gate + j))
    return pl.pallas_call(
        functools.partial(_merge_kernel, n_ctx_tiles=N_CTX // tm),
        grid=(N_TOK // tm, D_MODEL // tn),
        in_specs=[y_spec, ctx_spec, lat_spec, ctx_spec, lat_spec, w_spec, w_spec, w_spec,
                  g_spec(0), g_spec(1), g_spec(2)],
        out_specs=pl.BlockSpec((tm, tn), lambda i, j: (i, j)),
        out_shape=jax.ShapeDtypeStruct((N_TOK, D_MODEL), BF16),
        compiler_params=_cparams(2),
        name="merge",
    )(y_a, y_b[0], y_b[1], y_c[0], y_c[1], w_up_a, w_up_b, w_up_c, z, z, z)


def _swiglu_kernel(h_ref, wu_ref, wg_ref, o_ref, su_ref, sg_ref):
    @pl.when(pl.program_id(1) == 0)
    def _():
        su_ref[...] = wu_ref[...].astype(BF16)
        sg_ref[...] = wg_ref[...].astype(BF16)

    h = h_ref[...]
    for c in range(su_ref.shape[1] // MXU_TILE):
        cs = slice(c * MXU_TILE, (c + 1) * MXU_TILE)
        up = jnp.dot(h, su_ref[:, cs], preferred_element_type=F32)
        gate = jnp.dot(h, sg_ref[:, cs], preferred_element_type=F32)
        o_ref[:, cs] = (_silu(gate) * up).astype(BF16)


def swiglu_in(h, w_ffn_in, l):
    tm, tn = 2048, 512
    n_up = D_FF // tn
    return pl.pallas_call(
        _swiglu_kernel,
        grid=(n_up, N_TOK // tm),
        in_specs=[pl.BlockSpec((tm, D_MODEL), lambda j, i: (i, 0)),
                  pl.BlockSpec((None, D_MODEL, tn), lambda j, i: (l, 0, j)),
                  pl.BlockSpec((None, D_MODEL, tn), lambda j, i: (l, 0, n_up + j))],
        out_specs=pl.BlockSpec((tm, tn), lambda j, i: (i, j)),
        out_shape=jax.ShapeDtypeStruct((N_TOK, D_FF), BF16),
        scratch_shapes=[pltpu.VMEM((D_MODEL, tn), BF16)] * 2,
        compiler_params=_cparams(2, VMEM_LIMIT_BIG),
        name="swiglu_in",
    )(h, w_ffn_in, w_ffn_in)


def _mix_a_kernel(u_ref, v_ref, w_ref, bt_ref, o_ref, *, rows):
    w = w_ref[...].astype(BF16)
    for c in range(rows // A_CHUNK):
        rs = slice(c * A_CHUNK, (c + 1) * A_CHUNK)
        u = _gelu_tanh(u_ref[rs, :])
        v = _gelu_tanh(v_ref[rs, :])
        mu = jnp.mean(v, -1, keepdims=True)
        vc = v - mu
        vn = (vc * lax.rsqrt(jnp.mean(vc * vc, -1, keepdims=True) + NORM_EPS)).astype(BF16)
        for g in range(HEADS):
            cs = slice(g * HEAD_DIM, (g + 1) * HEAD_DIM)
            zg = jnp.dot(w[g], vn[:, cs], preferred_element_type=F32) + bt_ref[:, g:g + 1]
            o_ref[rs, cs] = (u[:, cs] * zg).astype(BF16)


def mix_a(z, a_w_l, a_b_l):
    rows = 512
    return pl.pallas_call(
        functools.partial(_mix_a_kernel, rows=rows),
        grid=(N_TOK // rows,),
        in_specs=[pl.BlockSpec((rows, WIDTH), lambda i: (i, COL_AU)),
                  pl.BlockSpec((rows, WIDTH), lambda i: (i, COL_AV)),
                  pl.BlockSpec((HEADS, A_CHUNK, A_CHUNK), lambda i: (0, 0, 0)),
                  pl.BlockSpec((A_CHUNK, HEADS), lambda i: (0, 0))],
        out_specs=pl.BlockSpec((rows, WIDTH), lambda i: (i, 0)),
        out_shape=jax.ShapeDtypeStruct((N_TOK, WIDTH), BF16),
        compiler_params=_cparams(1),
        name="mix_a",
    )(z, z, a_w_l, a_b_l.T)


def _softmax_pv(scores, values):
    m = scores[0].max(-1, keepdims=True)
    for s in scores[1:]:
        m = jnp.maximum(m, s.max(-1, keepdims=True))
    den = 0.0
    acc = 0.0
    for s, v in zip(scores, values):
        p = jnp.exp(s - m)
        den = den + p.sum(-1, keepdims=True)
        acc = acc + jnp.dot(p.astype(BF16), v, preferred_element_type=F32)
    return acc / den


def _qkt(q, k):
    return lax.dot_general(q, k, (((1,), (1,)), ((), ())), preferred_element_type=F32)


def _ctx_attn_kernel(q_ref, k_ref, v_ref, *rest):
    o_ref, ko_ref, vo_ref = rest[-3:]
    scale = HEAD_DIM ** -0.5
    ko_ref[...] = k_ref[...]
    vo_ref[...] = v_ref[...]
    for h in range(HEADS):
        cs = slice(h * HEAD_DIM, (h + 1) * HEAD_DIM)
        q = q_ref[:, cs].astype(BF16)
        k = k_ref[:, cs].astype(BF16)
        v = v_ref[:, cs].astype(BF16)
        o_ref[:, cs] = _softmax_pv([_qkt(q, k) * scale], [v]).astype(BF16)


def ctx_attention(z, l, k_all, v_all):
    spec = lambda col: pl.BlockSpec((SEQ, WIDTH), lambda b: (b, col))
    kv_spec = pl.BlockSpec((None, None, SEQ, WIDTH), lambda b: (b, l, 0, 0))
    kv_shape = jax.ShapeDtypeStruct((BATCH, DEPTH, SEQ, WIDTH), F32)
    raw = pl.BlockSpec(memory_space=pl.ANY)
    return pl.pallas_call(
        _ctx_attn_kernel,
        grid=(BATCH,),
        in_specs=[spec(COL_BQ), spec(COL_BK), spec(COL_BV), raw, raw],
        out_specs=[pl.BlockSpec((SEQ, WIDTH), lambda b: (b, 0)), kv_spec, kv_spec],
        out_shape=[jax.ShapeDtypeStruct((N_CTX, WIDTH), BF16), kv_shape, kv_shape],
        input_output_aliases={3: 1, 4: 2},
        compiler_params=_cparams(1),
        name="ctx_attn",
    )(z, z, z, k_all, v_all)


ROPE_AXIS = HEAD_DIM // 2
ROPE_PAIR = ROPE_AXIS // 2


def _rope(x, cos, sin_signed, first_of_pair):
    partner = jnp.where(first_of_pair, pltpu.roll(x, HEAD_DIM - ROPE_PAIR, 1), pltpu.roll(x, ROPE_PAIR, 1))
    return x * cos + partner * sin_signed


NB_GRID_ROWS = DEC_SEQ // GRID_W
NB_WIN_ROWS = min(NB_ROWS_MAX, NB_GRID_ROWS)
NB_Q_ROWS = 4


def _nb_row_start(qr):
    return min(max(qr - NB_WIN_ROWS // 2, 0), NB_GRID_ROWS - NB_WIN_ROWS)


def _nb_tiles():
    tiles = []
    for q0 in range(0, NB_GRID_ROWS, NB_Q_ROWS):
        starts = [_nb_row_start(qr) for qr in range(q0, q0 + NB_Q_ROWS)]
        lo, hi = min(starts), max(starts) + NB_WIN_ROWS
        tiles.append((q0, lo - lo % 2, hi + hi % 2))
    return tiles


def _nbhd_attn_kernel(q_ref, k_ref, v_ref, kc_ref, vc_ref, pair_ref, cos_ref, sin_ref, o_ref, bias_scr):
    scale = HEAD_DIM ** -0.5
    lane = lax.broadcasted_iota(jnp.int32, (1, HEAD_DIM), 1)
    first_quarter = (lane % ROPE_AXIS) < ROPE_PAIR

    @pl.when(pl.program_id(1) == 0)
    def _():
        left = lane < GRID_W
        neg = jnp.full((GRID_W, 2 * GRID_W), NEG_INF, F32)
        for q0, lo, hi in _nb_tiles():
            for qr in range(q0, q0 + NB_Q_ROWS):
                r0 = _nb_row_start(qr)
                for p in range(lo // 2, hi // 2):
                    ok0 = r0 <= 2 * p < r0 + NB_WIN_ROWS
                    ok1 = r0 <= 2 * p + 1 < r0 + NB_WIN_ROWS
                    tile = neg
                    if ok0 or ok1:
                        tile = pair_ref[2 * p - qr + NB_ROWS_MAX]
                        if not ok0:
                            tile = jnp.where(left, NEG_INF, tile)
                        if not ok1:
                            tile = jnp.where(left, tile, NEG_INF)
                    bias_scr[qr * GRID_W:(qr + 1) * GRID_W, p * 2 * GRID_W:(p + 1) * 2 * GRID_W] = tile

    k = _rope(k_ref[...], cos_ref[...], sin_ref[...], first_quarter).astype(BF16)
    v = v_ref[...].astype(BF16)
    kc = kc_ref[...].astype(BF16)
    vc = vc_ref[...].astype(BF16)
    for q0, lo, hi in _nb_tiles():
        rs = slice(q0 * GRID_W, (q0 + NB_Q_ROWS) * GRID_W)
        ks = slice(lo * GRID_W, hi * GRID_W)
        q = _rope(q_ref[rs, :], cos_ref[rs, :], sin_ref[rs, :], first_quarter).astype(BF16)
        s_loc = _qkt(q, k[ks]) * scale + bias_scr[rs, ks]
        s_ctx = _qkt(q, kc) * scale
        o_ref[rs, :] = _softmax_pv([s_loc, s_ctx], [v[ks], vc]).astype(BF16)


def nbhd_attention(z, cache_k, cache_v, pair_table, cos, sin_signed, l):
    lat0 = N_CTX // DEC_SEQ
    col = lambda c: (lambda h, b: (lat0 + b, c * HEADS + h))
    tok = lambda c: pl.BlockSpec((DEC_SEQ, HEAD_DIM), col(c))
    cache = pl.BlockSpec((None, None, PAST_LEN, HEAD_DIM), lambda h, b: (b, l, 0, h))
    table = pl.BlockSpec((DEC_SEQ, HEAD_DIM), lambda h, b: (0, 0))
    pairs = pl.BlockSpec((None, 2 * NB_ROWS_MAX, GRID_W, 2 * GRID_W), lambda h, b: (h, 0, 0, 0))
    return pl.pallas_call(
        _nbhd_attn_kernel,
        grid=(HEADS, DEC_BATCH),
        in_specs=[tok(COL_BQ), tok(COL_BK), tok(COL_BV), cache, cache, pairs, table, table],
        out_specs=pl.BlockSpec((DEC_SEQ, HEAD_DIM), lambda h, b: (b, h)),
        out_shape=jax.ShapeDtypeStruct((N_LAT, WIDTH), BF16),
        scratch_shapes=[pltpu.VMEM((DEC_SEQ, DEC_SEQ), F32)],
        compiler_params=_cparams(2),
        name="nbhd_attn",
    )(z, z, z, cache_k, cache_v, pair_table, cos, sin_signed)


def nbhd_pair_tables(rpb):
    cols = jnp.arange(GRID_W)
    c_start = jnp.clip(cols - NB_COLS // 2, 0, GRID_W - NB_COLS)
    col_ok = (cols[None, :] >= c_start[:, None]) & (cols[None, :] < c_start[:, None] + NB_COLS)
    dc_idx = jnp.clip(cols[None, :] - cols[:, None] + NB_COLS - 1, 0, 2 * NB_COLS - 2)
    onehot = (dc_idx[None] == jnp.arange(2 * NB_COLS - 1)[:, None, None]).astype(F32)
    t = jnp.einsum('lhrd,dqk->lhrqk', rpb.astype(F32), onehot, precision=lax.Precision.HIGHEST)
    t = jnp.where(col_ok, t, NEG_INF)
    neg = jnp.full((DEPTH, HEADS, 1, GRID_W, GRID_W), NEG_INF, F32)
    tp = jnp.concatenate([neg, t, neg], 2)
    return jnp.concatenate([tp[:, :, :-1], tp[:, :, 1:]], -1)


def rope_tables():
    ax = HEAD_DIM // 2
    pos = jnp.arange(DEC_SEQ)
    inv = ROPE_THETA ** (-jnp.arange(0, ax, 2, dtype=F32) / ax)

    def ang(p):
        a = p.astype(F32)[:, None] * inv[None, :]
        return jnp.concatenate([a, a], -1)

    a = jnp.concatenate([ang(pos // GRID_W), ang(pos % GRID_W)], -1)
    sign = jnp.where((jnp.arange(HEAD_DIM) % ax) < ax // 2, -1.0, 1.0).astype(F32)
    return jnp.cos(a), jnp.sin(a) * sign[None, :]


def _split3(x):
    hi = x.astype(BF16)
    r = x - hi.astype(F32)
    mid = r.astype(BF16)
    lo = (r - mid.astype(F32)).astype(BF16)
    return hi, mid, lo


def _hgrn_chunks(chains, st, tri_ref, pairs_ref, side_ref):
    c = HGRN_CHUNK
    st = list(st)
    qs, kk, ff, bb, vb = [], [], [], [], []
    for q, zf, v, (log2lb, log21mlb, half1mlb), d, b_scr, _ in chains:
        z2 = zf * LOG2E
        ls = jnp.minimum(z2, 0.0) - jnp.log2(1.0 + jnp.exp2(-jnp.abs(z2)))
        x2 = log21mlb + ls
        lf = jnp.maximum(log2lb, x2) + jnp.log2(1.0 + jnp.exp2(-jnp.abs(log2lb - x2)))
        k = half1mlb * (1.0 - jnp.tanh(0.5 * zf))
        b = sum(jnp.dot(tri_ref[d], p, preferred_element_type=F32) for p in _split3(lf))
        b_scr[...] = b
        qs.append(_silu(q))
        kk.append(k)
        ff.append(1.0 - k)
        bb.append(b)
        vb.append(v.astype(BF16))

    outs = []
    for i, (_, _, _, _, d, b_scr, s) in enumerate(chains):
        end = 0 if d else c - 1
        b_end = b_scr[end:end + 1, :]
        k_dec = (kk[i] * jnp.exp2(b_end - bb[i])).astype(BF16)
        st_add = lax.dot_general(vb[i], k_dec, (((0,), (0,)), ((), ())), preferred_element_type=F32)
        if st[s] is None:
            outs.append(None)
            st[s] = st_add
        else:
            outs.append(_qkt((qs[i] * jnp.exp2(bb[i])).astype(BF16), st[s].astype(BF16)))
            st[s] = st[s] * jnp.exp2(b_end) + st_add

    attn = []
    for i, (_, _, _, _, d, _, _) in enumerate(chains):
        a = jnp.zeros((c, c), F32)
        kd = kk[i]
        for delta in range(HGRN_BAND):
            if delta:
                kd3 = kd.reshape(c // SUBLANES, SUBLANES, HEAD_DIM)
                kd = ff[i] * pltpu.roll(kd3, SUBLANES - 1 if d else 1, 1).reshape(c, HEAD_DIM)
            a = jnp.where(pairs_ref[d] == delta, (qs[i] * kd).sum(-1, keepdims=True), a)
        attn.append(a)

    qb = [x.astype(BF16) for x in qs]
    kb = [x.astype(BF16) for x in kk]
    for lv, m in enumerate(HGRN_LEVELS):
        for i, (_, _, _, _, d, b_scr, _) in enumerate(chains):
            seam = m // 2 if d else m // 2 - 1
            g = jnp.concatenate([jnp.broadcast_to(b_scr[j * m + seam:j * m + seam + 1, :], (m, HEAD_DIM))
                                 for j in range(c // m)], 0)
            e = jnp.exp2((bb[i] - g) * side_ref[d, lv]).astype(BF16)
            attn[i] = jnp.where(pairs_ref[d] == m, _qkt(qb[i] * e, kb[i] * e), attn[i])
    for i in range(len(chains)):
        o_intra = jnp.dot(attn[i].astype(BF16), vb[i], preferred_element_type=F32)
        outs[i] = o_intra if outs[i] is None else outs[i] + o_intra
    return outs, st


def _hgrn_kernel(q_ref, ff_ref, fb_ref, v_ref, g_ref, lbp_ref, tri_ref, pairs_ref, side_ref, *rest,
                 n, heads, latent):
    of_scr, ob_scr, b_scr = rest[-3:]
    if latent:
        s0_ref, y_ref = rest[:2]
    else:
        y_ref, s_ref = rest[1:3]
    c = HGRN_CHUNK
    nc = n // c
    lanes = [slice(h * HEAD_DIM, (h + 1) * HEAD_DIM) for h in range(heads)]

    def body(i, st):
        chains, dst = [], []
        for u in range(HGRN_CHUNKS_PER_TRIP):
            cf = i * HGRN_CHUNKS_PER_TRIP + u
            start = (lambda r: r) if isinstance(cf, int) else (lambda r: pl.multiple_of(r, c))
            rf = pl.ds(start(cf * c), c)
            rb = pl.ds(start((nc - 1 - cf) * c), c)
            for h, ls in enumerate(lanes):
                gate_f = [lbp_ref[r:r + 1, ls] for r in range(3)]
                gate_b = [lbp_ref[3 + r:4 + r, ls] for r in range(3)]
                slot = 2 * (u * heads + h)
                chains.append((q_ref[rf, ls], ff_ref[rf, ls], v_ref[rf, ls], gate_f, 0, b_scr.at[slot], 2 * h))
                chains.append((q_ref[rb, ls], fb_ref[rb, ls], v_ref[rb, ls], gate_b, 1, b_scr.at[slot + 1],
                               2 * h + 1))
                dst += [(of_scr, rf, ls), (ob_scr, rb, ls)]
        outs, st = _hgrn_chunks(chains, st, tri_ref, pairs_ref, side_ref)
        for (scr, rows, ls), o in zip(dst, outs):
            scr[rows, ls] = o
        return tuple(st)

    trips = nc // HGRN_CHUNKS_PER_TRIP
    if latent:
        st = lax.fori_loop(0, trips, body, tuple(s0_ref[d, h].T for h in range(heads) for d in range(2)))
    else:
        st = body(0, (None,) * (2 * heads))
        if trips > 1:
            st = lax.fori_loop(1, trips, body, st)
    if not latent:
        for h in range(heads):
            for d in range(2):
                s_ref[d, h] = st[2 * h + d].T

    def finish(i, carry):
        rs = pl.ds(pl.multiple_of(i * c, c), c)
        for ls in lanes:
            y_ref[rs, ls] = (_rms(of_scr[rs, ls] + ob_scr[rs, ls]) * _silu(g_ref[rs, ls])).astype(BF16)
        return carry

    lax.fori_loop(0, nc, finish, 0)


def _hgrn_constants():
    c = HGRN_CHUNK
    t = jnp.arange(c)[:, None]
    s = jnp.arange(c)[None, :]
    tri_f = (s <= t)
    pairs_f = jnp.where((t // HGRN_BAND == s // HGRN_BAND) & (s <= t), t - s, -1).astype(jnp.int32)
    for m in HGRN_LEVELS:
        h = m // 2
        pairs_f = jnp.where((t // h == s // h + 1) & ((t // h) % 2 == 1), m, pairs_f)
    tri = jnp.stack([tri_f, tri_f.T]).astype(BF16)
    pairs = jnp.stack([pairs_f, pairs_f.T])
    later_half = jnp.stack([(jnp.arange(c) % m) >= m // 2 for m in HGRN_LEVELS])
    side_f = jnp.broadcast_to(jnp.where(later_half, 1.0, -1.0)[:, :, None], (len(HGRN_LEVELS), c, HEAD_DIM))
    side = jnp.stack([side_f, -side_f]).astype(F32)
    return tri, pairs, side


def hgrn(z, lbp, l, s_all=None, s0=None):
    latent = s0 is not None
    n = DEC_SEQ if latent else SEQ
    bsz = DEC_BATCH if latent else BATCH
    blk0 = N_CTX // DEC_SEQ if latent else 0
    hb = HGRN_LAT_HEADS_PER_STEP if latent else HGRN_CTX_HEADS_PER_STEP
    wb = hb * HEAD_DIM
    tri, pairs, side = _hgrn_constants()
    tok = lambda c: pl.BlockSpec((n, wb), lambda b, h: (blk0 + b, c * (HEADS // hb) + h))
    raw = pl.BlockSpec(memory_space=pl.ANY)
    in_specs = [tok(COL_CQ), tok(COL_CFF), tok(COL_CFB), tok(COL_CI), tok(COL_CG),
                pl.BlockSpec((8, wb), lambda b, h: (0, h)),
                pl.BlockSpec((2, HGRN_CHUNK, HGRN_CHUNK), lambda b, h: (0, 0, 0)),
                pl.BlockSpec((2, HGRN_CHUNK, HGRN_CHUNK), lambda b, h: (0, 0, 0)),
                pl.BlockSpec(side.shape, lambda b, h: (0, 0, 0, 0))]
    args = [z, z, z, z, z, lbp, tri, pairs, side]
    y_spec = pl.BlockSpec((n, wb), lambda b, h: (b, h))
    y_shape = jax.ShapeDtypeStruct((bsz * n, WIDTH), BF16)
    state = pl.BlockSpec((None, None, 2, hb, HEAD_DIM, HEAD_DIM), lambda b, h: (b, l, 0, h, 0, 0))
    if latent:
        in_specs.append(state)
        args.append(s0)
        out_specs, out_shape, aliases = y_spec, y_shape, {}
    else:
        aliases = {len(args): 1}
        in_specs.append(raw)
        args.append(s_all)
        out_specs = [y_spec, state]
        out_shape = [y_shape, jax.ShapeDtypeStruct(s_all.shape, F32)]
    return pl.pallas_call(
        functools.partial(_hgrn_kernel, n=n, heads=hb, latent=latent),
        grid=(bsz, HEADS // hb),
        in_specs=in_specs, out_specs=out_specs, out_shape=out_shape,
        scratch_shapes=[pltpu.VMEM((n, wb), F32)] * 2
        + [pltpu.VMEM((2 * HGRN_CHUNKS_PER_TRIP * hb, HGRN_CHUNK, HEAD_DIM), F32)],
        input_output_aliases=aliases,
        compiler_params=_cparams(2),
        name="hgrn",
    )(*args)


def hgrn_gate_params(lb_logits):
    p = jax.nn.softmax(lb_logits.astype(F32), axis=1)
    cs = jnp.cumsum(p, axis=1)
    lb = cs - cs[:, :1]
    rows = [jnp.log(lb[0]) * LOG2E, jnp.log1p(-lb[0]) * LOG2E, 0.5 * (1.0 - lb[0]),
            jnp.log(lb[1]) * LOG2E, jnp.log1p(-lb[1]) * LOG2E, 0.5 * (1.0 - lb[1]),
            jnp.zeros_like(lb[0]), jnp.zeros_like(lb[0])]
    return jnp.stack(rows, axis=1)


def kernel(x_prompt, x_sample, cache_attn_k, cache_attn_v, state_hgrn, c, c_ctx, w_ada, b_ada, norm_g, w_in,
           a_spatial_w, a_spatial_b, nb_rpb, hgrn_lb_logits, w_up_a, w_up_b, w_up_c, w_out, w_ffn_in, w_ffn_out):
    x = (x_prompt.reshape(N_CTX, D_MODEL), x_sample.reshape(N_LAT, D_MODEL))
    cond = jnp.zeros((COND_ROWS, D_MODEL), F32).at[0].set(c_ctx).at[1:N_COND].set(c)
    mod = ada_all_layers(cond, w_ada, b_ada)
    mod = mod.reshape(DEPTH, COND_ROWS, 6, 1, D_MODEL).transpose(0, 2, 1, 3, 4)
    lbp = hgrn_gate_params(hgrn_lb_logits)
    cos, sin_signed = rope_tables()
    pair_tables = nbhd_pair_tables(nb_rpb)
    cache_k = cache_attn_k.reshape(DEC_BATCH, DEPTH, PAST_LEN, WIDTH)
    cache_v = cache_attn_v.reshape(DEC_BATCH, DEPTH, PAST_LEN, WIDTH)
    gain = lambda l, i: norm_g[l, i][None, :]
    w_out_b = w_out.astype(BF16)
    w_ffn_out_b = w_ffn_out.astype(BF16)
    w_up_a, w_up_b, w_up_c = (w.astype(BF16) for w in (w_up_a, w_up_b, w_up_c))

    k_all = jnp.zeros((BATCH, DEPTH, SEQ, WIDTH), F32)
    v_all = jnp.zeros((BATCH, DEPTH, SEQ, WIDTH), F32)
    s_all = jnp.zeros((BATCH, DEPTH, 2, HEADS, HEAD_DIM, HEAD_DIM), F32)
    h = prenorm(*x, gain(0, 0), mod[0, 1], mod[0, 0])
    for l in range(DEPTH):
        sh1, sc1, g1, sh2, sc2, g2 = (mod[l, i] for i in range(6))
        z = matmul_layer(h, w_in, l, tm=1024, tn=1024)
        y_a = mix_a(z, a_spatial_w[l], a_spatial_b[l])
        yb_ctx, k_all, v_all = ctx_attention(z, l, k_all, v_all)
        yb_lat = nbhd_attention(z, cache_k, cache_v, pair_tables[l], cos, sin_signed, l)
        yc_ctx, s_all = hgrn(z, lbp[l], l, s_all=s_all)
        yc_lat = hgrn(z, lbp[l], l, s0=state_hgrn)
        merged = merge_branches(y_a, (yb_ctx, yb_lat), (yc_ctx, yc_lat), z, w_up_a, w_up_b, w_up_c, l)
        x, h2 = proj_resnorm(merged, w_out_b, l, x, gain(l, 1), g1, nxt=(gain(l, 2), sc2, sh2))
        act = swiglu_in(h2, w_ffn_in, l)
        nxt = None if l == DEPTH - 1 else (gain(l + 1, 0), mod[l + 1, 1], mod[l + 1, 0])
        x, h = proj_resnorm(act, w_ffn_out_b, l, x, gain(l, 3), g2, nxt=nxt, tk=D_FF // 4,
                            split_out=l == DEPTH - 1)

    y_prompt = x[0].reshape(BATCH, SEQ, D_MODEL)
    y_sample = x[1].reshape(DEC_BATCH, DEC_SEQ, D_MODEL)
    kv_shape = (BATCH, DEPTH, SEQ, HEADS, HEAD_DIM)
    return (y_prompt, y_sample, k_all.reshape(kv_shape), v_all.reshape(kv_shape), s_all)
```

```python
import functools
import math

import jax
import jax.numpy as jnp
from jax import lax
from jax.experimental import pallas as pl
from jax.experimental.pallas import tpu as pltpu

F32 = jnp.float32
BF16 = jnp.bfloat16

D_MODEL = 2048
BATCH = 32
SEQ = 256
DEPTH = 4
DEC_BATCH = 2
DEC_SEQ = 1024
PAST_LEN = 512
GRID_W = 64
HEADS = 8
HEAD_DIM = 128
WIDTH = HEADS * HEAD_DIM
A_CHUNK = 128
NB_ROWS_MAX = 8
NB_COLS = 16
ROPE_THETA = 10000.0
D_FF = -(-8 * D_MODEL // (3 * 256)) * 256
NORM_EPS = 1e-6
NEG_INF = -1e30
IN_WIDTH = 10 * WIDTH + 3 * D_MODEL

N_CTX = BATCH * SEQ
N_LAT = DEC_BATCH * DEC_SEQ
N_TOK = N_CTX + N_LAT
N_COND = 1 + DEC_BATCH
COND_ROWS = 8

COL_AU, COL_AV, COL_BQ, COL_BK, COL_BV, COL_CQ, COL_CFF, COL_CFB, COL_CI, COL_CG = range(10)
COL_GATES = 10 * WIDTH

HGRN_CHUNK = 128
HGRN_BAND = 4
HGRN_LEVELS = (8, 16, 32, 64, 128)
SUBLANES = 8
HGRN_CHUNKS_PER_TRIP = 2
HGRN_CTX_HEADS_PER_STEP = 4
HGRN_LAT_HEADS_PER_STEP = 2

LOG2E = math.log2(math.e)

MXU_TILE = 256
V7X_VMEM_BYTES = 64 * 1024 * 1024
VMEM_LIMIT = V7X_VMEM_BYTES * 3 // 4
VMEM_LIMIT_BIG = V7X_VMEM_BYTES * 7 // 8


def _cparams(n_axes, vmem=VMEM_LIMIT):
    return pltpu.CompilerParams(dimension_semantics=("arbitrary",) * n_axes, vmem_limit_bytes=vmem)


def _cond_row(i, tm):
    n_ctx_tiles = N_CTX // tm
    return jnp.where(i < n_ctx_tiles, 0, 1 + (i - n_ctx_tiles) // (DEC_SEQ // tm))


def _sigmoid(x):
    return 0.5 * (1.0 + jnp.tanh(0.5 * x))


def _silu(x):
    hx = 0.5 * x
    return hx + hx * jnp.tanh(hx)


def _gelu_tanh(x):
    c = math.sqrt(2.0 / math.pi)
    return x * (0.5 * (1.0 + jnp.tanh(c * (x + 0.044715 * (x * x * x)))))


def _rms(x):
    return x * lax.rsqrt(jnp.mean(x * x, -1, keepdims=True) + NORM_EPS)


def _ada_kernel(c_ref, w_ref, b_ref, o_ref):
    s = _silu(c_ref[...]).astype(BF16)
    o_ref[...] = jnp.dot(s, w_ref[...].astype(BF16), preferred_element_type=F32) + b_ref[...]


def ada_all_layers(cond, w_ada, b_ada):
    tn = 1024
    n_out = 6 * D_MODEL
    return pl.pallas_call(
        _ada_kernel,
        grid=(DEPTH, n_out // tn),
        in_specs=[pl.BlockSpec((COND_ROWS, D_MODEL), lambda l, j: (0, 0)),
                  pl.BlockSpec((None, D_MODEL, tn), lambda l, j: (l, 0, j)),
                  pl.BlockSpec((None, 1, tn), lambda l, j: (l, 0, j))],
        out_specs=pl.BlockSpec((None, COND_ROWS, tn), lambda l, j: (l, 0, j)),
        out_shape=jax.ShapeDtypeStruct((DEPTH, COND_ROWS, n_out), F32),
        compiler_params=_cparams(2),
        name="ada",
    )(cond, w_ada, b_ada.reshape(DEPTH, 1, n_out))


def _path_specs(tm, width=D_MODEL):
    n_ctx_tiles = N_CTX // tm
    ctx = pl.BlockSpec((tm, width), lambda i, *_: (jnp.minimum(i, n_ctx_tiles - 1), 0))
    lat = pl.BlockSpec((tm, width), lambda i, *_: (jnp.maximum(i - n_ctx_tiles, 0), 0))
    return ctx, lat


def _prenorm_kernel(xc_ref, xl_ref, g_ref, sc_ref, sh_ref, h_ref, *, n_ctx_tiles):
    x = jnp.where(pl.program_id(0) < n_ctx_tiles, xc_ref[...], xl_ref[...])
    h_ref[...] = (_rms(x) * (g_ref[...] * (1.0 + sc_ref[...])) + sh_ref[...]).astype(BF16)


def prenorm(x_ctx, x_lat, gain, sc, sh):
    tm = 512
    row = lambda i: (_cond_row(i, tm), 0, 0)
    ctx, lat = _path_specs(tm)
    return pl.pallas_call(
        functools.partial(_prenorm_kernel, n_ctx_tiles=N_CTX // tm),
        grid=(N_TOK // tm,),
        in_specs=[ctx, lat,
                  pl.BlockSpec((1, D_MODEL), lambda i: (0, 0)),
                  pl.BlockSpec((None, 1, D_MODEL), row),
                  pl.BlockSpec((None, 1, D_MODEL), row)],
        out_specs=pl.BlockSpec((tm, D_MODEL), lambda i: (i, 0)),
        out_shape=jax.ShapeDtypeStruct((N_TOK, D_MODEL), BF16),
        compiler_params=_cparams(1),
        name="prenorm",
    )(x_ctx, x_lat, gain, sc, sh)


def _proj_resnorm_kernel(a_ref, w_ref, *rest, nk, n_ctx_tiles, split_in, split_out, with_next):
    rest = list(rest)
    x_refs = [rest.pop(0) for _ in range(1 + split_in)]
    g_ref, gate_ref = rest.pop(0), rest.pop(0)
    if with_next:
        g2_ref, sc_ref, sh_ref = rest.pop(0), rest.pop(0), rest.pop(0)
    xo_refs = [rest.pop(0) for _ in range(1 + split_out)]
    if with_next:
        h_ref = rest.pop(0)
    is_ctx = pl.program_id(0) < n_ctx_tiles

    def part():
        return jnp.dot(a_ref[...], w_ref[...], preferred_element_type=F32)

    def finish(t):
        x = jnp.where(is_ctx, x_refs[0][...], x_refs[1][...]) if split_in else x_refs[0][...]
        x_new = x + _rms(t) * (gate_ref[...] * g_ref[...])
        if split_out:
            @pl.when(is_ctx)
            def _():
                xo_refs[0][...] = x_new

            @pl.when(jnp.logical_not(is_ctx))
            def _():
                xo_refs[1][...] = x_new
        else:
            xo_refs[0][...] = x_new
        if with_next:
            h_ref[...] = (_rms(x_new) * (g2_ref[...] * (1.0 + sc_ref[...])) + sh_ref[...]).astype(BF16)

    if nk == 1:
        finish(part())
        return
    acc_ref = rest[-1]
    k = pl.program_id(1)

    @pl.when(k == 0)
    def _():
        acc_ref[...] = part()

    @pl.when(k > 0)
    def _():
        acc_ref[...] += part()

    @pl.when(k == nk - 1)
    def _():
        finish(acc_ref[...])


def proj_resnorm(a, w, l, x, gain, gate, nxt=None, tk=None, split_out=False):
    tm = 512
    tk = tk or a.shape[1]
    nk = a.shape[1] // tk
    split_in = isinstance(x, tuple)
    row = lambda i, k: (_cond_row(i, tm), 0, 0)
    tile = pl.BlockSpec((tm, D_MODEL), lambda i, k: (i, 0))
    vec = pl.BlockSpec((1, D_MODEL), lambda i, k: (0, 0))
    mod = pl.BlockSpec((None, 1, D_MODEL), row)
    w_mode = dict(pipeline_mode=pl.Buffered(1)) if nk == 1 else {}
    in_specs = [pl.BlockSpec((tm, tk), lambda i, k: (i, k)),
                pl.BlockSpec((None, tk, D_MODEL), lambda i, k: (l, k, 0), **w_mode)]
    in_specs += list(_path_specs(tm)) if split_in else [tile]
    in_specs += [vec, mod]
    args = [a, w, *(x if split_in else (x,)), gain, gate]
    if nxt is not None:
        in_specs += [vec, mod, mod]
        args += list(nxt)
    if split_out:
        out_specs = list(_path_specs(tm))
        out_shape = [jax.ShapeDtypeStruct((N_CTX, D_MODEL), F32), jax.ShapeDtypeStruct((N_LAT, D_MODEL), F32)]
    else:
        out_specs = [tile]
        out_shape = [jax.ShapeDtypeStruct((N_TOK, D_MODEL), F32)]
    if nxt is not None:
        out_specs.append(tile)
        out_shape.append(jax.ShapeDtypeStruct((N_TOK, D_MODEL), BF16))
    out = pl.pallas_call(
        functools.partial(_proj_resnorm_kernel, nk=nk, n_ctx_tiles=N_CTX // tm, split_in=split_in,
                          split_out=split_out, with_next=nxt is not None),
        grid=(N_TOK // tm, nk),
        in_specs=in_specs, out_specs=out_specs, out_shape=out_shape,
        scratch_shapes=[pltpu.VMEM((tm, D_MODEL), F32)] if nk > 1 else [],
        compiler_params=_cparams(2, VMEM_LIMIT_BIG),
        name="proj_resnorm",
    )(*args)
    n_x = 1 + split_out
    x_new = tuple(out[:n_x]) if split_out else out[0]
    return x_new, (out[n_x] if nxt is not None else None)


def _mm_kernel(a_ref, w_ref, o_ref, wb_ref):
    @pl.when(pl.program_id(1) == 0)
    def _():
        wb_ref[...] = w_ref[...].astype(BF16)

    o_ref[...] = jnp.dot(a_ref[...], wb_ref[...], preferred_element_type=F32)


def matmul_layer(a, w, l, tm, tn, vmem=VMEM_LIMIT):
    m, k = a.shape
    n = w.shape[2]
    return pl.pallas_call(
        _mm_kernel,
        grid=(n // tn, m // tm),
        in_specs=[pl.BlockSpec((tm, k), lambda j, i: (i, 0)),
                  pl.BlockSpec((None, k, tn), lambda j, i: (l, 0, j))],
        out_specs=pl.BlockSpec((tm, tn), lambda j, i: (i, j)),
        out_shape=jax.ShapeDtypeStruct((m, n), F32),
        scratch_shapes=[pltpu.VMEM((k, tn), BF16)],
        compiler_params=_cparams(2, vmem),
        name="matmul",
    )(a, w)


def _merge_kernel(ya_ref, ybc_ref, ybl_ref, ycc_ref, ycl_ref, wa_ref, wb_ref, wc_ref, ga_ref, gb_ref, gc_ref,
                  o_ref, *, n_ctx_tiles):
    def run(yb_ref, yc_ref):
        acc = _sigmoid(ga_ref[...]) * jnp.dot(ya_ref[...], wa_ref[...], preferred_element_type=F32)
        acc += _sigmoid(gb_ref[...]) * jnp.dot(yb_ref[...], wb_ref[...], preferred_element_type=F32)
        acc += _sigmoid(gc_ref[...]) * jnp.dot(yc_ref[...], wc_ref[...], preferred_element_type=F32)
        o_ref[...] = acc.astype(BF16)

    pl.when(pl.program_id(0) < n_ctx_tiles)(lambda: run(ybc_ref, ycc_ref))
    pl.when(pl.program_id(0) >= n_ctx_tiles)(lambda: run(ybl_ref, ycl_ref))


def merge_branches(y_a, y_b, y_c, z, w_up_a, w_up_b, w_up_c, l):
    tm, tn = 512, 2048
    gate0 = COL_GATES // tn
    per_gate = D_MODEL // tn
    y_spec = pl.BlockSpec((tm, WIDTH), lambda i, j: (i, 0))
    ctx_spec, lat_spec = _path_specs(tm, WIDTH)
    w_mode = dict(pipeline_mode=pl.Buffered(1)) if tn == D_MODEL else {}
    w_spec = pl.BlockSpec((None, WIDTH, tn), lambda i, j: (l, 0, j), **w_mode)
    g_spec = lambda b: pl.BlockSpec((tm, tn), lambda i, j: (i, gate0 + b * per_gate + j))
    return pl.pallas_call(
        functools.partial(_merge_kernel, n_ctx_tiles=N_CTX // tm),
        grid=(N_TOK // tm, D_MODEL // tn),
        in_specs=[y_spec, ctx_spec, lat_spec, ctx_spec, lat_spec, w_spec, w_spec, w_spec,
                  g_spec(0), g_spec(1), g_spec(2)],
        out_specs=pl.BlockSpec((tm, tn), lambda i, j: (i, j)),
        out_shape=jax.ShapeDtypeStruct((N_TOK, D_MODEL), BF16),
        compiler_params=_cparams(2, VMEM_LIMIT_BIG),
        name="merge",
    )(y_a, y_b[0], y_b[1], y_c[0], y_c[1], w_up_a, w_up_b, w_up_c, z, z, z)


def _swiglu_kernel(h_ref, wu_ref, wg_ref, o_ref, su_ref, sg_ref):
    @pl.when(pl.program_id(1) == 0)
    def _():
        su_ref[...] = wu_ref[...].astype(BF16)
        sg_ref[...] = wg_ref[...].astype(BF16)

    h = h_ref[...]
    for c in range(su_ref.shape[1] // MXU_TILE):
        cs = slice(c * MXU_TILE, (c + 1) * MXU_TILE)
        up = jnp.dot(h, su_ref[:, cs], preferred_element_type=F32)
        gate = jnp.dot(h, sg_ref[:, cs], preferred_element_type=F32)
        o_ref[:, cs] = (_silu(gate) * up).astype(BF16)


def swiglu_in(h, w_ffn_in, l):
    tm, tn = 2048, 512
    n_up = D_FF // tn
    return pl.pallas_call(
        _swiglu_kernel,
        grid=(n_up, N_TOK // tm),
        in_specs=[pl.BlockSpec((tm, D_MODEL), lambda j, i: (i, 0)),
                  pl.BlockSpec((None, D_MODEL, tn), lambda j, i: (l, 0, j)),
                  pl.BlockSpec((None, D_MODEL, tn), lambda j, i: (l, 0, n_up + j))],
        out_specs=pl.BlockSpec((tm, tn), lambda j, i: (i, j)),
        out_shape=jax.ShapeDtypeStruct((N_TOK, D_FF), BF16),
        scratch_shapes=[pltpu.VMEM((D_MODEL, tn), BF16)] * 2,
        compiler_params=_cparams(2, VMEM_LIMIT_BIG),
        name="swiglu_in",
    )(h, w_ffn_in, w_ffn_in)


def _mix_a_kernel(u_ref, v_ref, w_ref, bt_ref, o_ref, *, rows):
    w = w_ref[...].astype(BF16)
    for c in range(rows // A_CHUNK):
        rs = slice(c * A_CHUNK, (c + 1) * A_CHUNK)
        u = _gelu_tanh(u_ref[rs, :])
        v = _gelu_tanh(v_ref[rs, :])
        mu = jnp.mean(v, -1, keepdims=True)
        vc = v - mu
        vn = (vc * lax.rsqrt(jnp.mean(vc * vc, -1, keepdims=True) + NORM_EPS)).astype(BF16)
        for g in range(HEADS):
            cs = slice(g * HEAD_DIM, (g + 1) * HEAD_DIM)
            zg = jnp.dot(w[g], vn[:, cs], preferred_element_type=F32) + bt_ref[:, g:g + 1]
            o_ref[rs, cs] = (u[:, cs] * zg).astype(BF16)


def mix_a(z, a_w_l, a_b_l):
    rows = 512
    return pl.pallas_call(
        functools.partial(_mix_a_kernel, rows=rows),
        grid=(N_TOK // rows,),
        in_specs=[pl.BlockSpec((rows, WIDTH), lambda i: (i, COL_AU)),
                  pl.BlockSpec((rows, WIDTH), lambda i: (i, COL_AV)),
                  pl.BlockSpec((HEADS, A_CHUNK, A_CHUNK), lambda i: (0, 0, 0)),
                  pl.BlockSpec((A_CHUNK, HEADS), lambda i: (0, 0))],
        out_specs=pl.BlockSpec((rows, WIDTH), lambda i: (i, 0)),
        out_shape=jax.ShapeDtypeStruct((N_TOK, WIDTH), BF16),
        compiler_params=_cparams(1),
        name="mix_a",
    )(z, z, a_w_l, a_b_l.T)


def _softmax_pv(scores, values):
    m = scores[0].max(-1, keepdims=True)
    for s in scores[1:]:
        m = jnp.maximum(m, s.max(-1, keepdims=True))
    den = 0.0
    acc = 0.0
    for s, v in zip(scores, values):
        p = jnp.exp(s - m)
        den = den + p.sum(-1, keepdims=True)
        acc = acc + jnp.dot(p.astype(BF16), v, preferred_element_type=F32)
    return acc / den


def _qkt(q, k):
    return lax.dot_general(q, k, (((1,), (1,)), ((), ())), preferred_element_type=F32)


def _ctx_attn_kernel(q_ref, k_ref, v_ref, *rest):
    o_ref, ko_ref, vo_ref = rest[-3:]
    scale = HEAD_DIM ** -0.5
    ko_ref[...] = k_ref[...]
    vo_ref[...] = v_ref[...]
    for h in range(HEADS):
        cs = slice(h * HEAD_DIM, (h + 1) * HEAD_DIM)
        q = q_ref[:, cs].astype(BF16)
        k = k_ref[:, cs].astype(BF16)
        v = v_ref[:, cs].astype(BF16)
        o_ref[:, cs] = _softmax_pv([_qkt(q, k) * scale], [v]).astype(BF16)


def ctx_attention(z, l, k_all, v_all):
    spec = lambda col: pl.BlockSpec((SEQ, WIDTH), lambda b: (b, col))
    kv_spec = pl.BlockSpec((None, None, SEQ, WIDTH), lambda b: (b, l, 0, 0))
    kv_shape = jax.ShapeDtypeStruct((BATCH, DEPTH, SEQ, WIDTH), F32)
    raw = pl.BlockSpec(memory_space=pl.ANY)
    return pl.pallas_call(
        _ctx_attn_kernel,
        grid=(BATCH,),
        in_specs=[spec(COL_BQ), spec(COL_BK), spec(COL_BV), raw, raw],
        out_specs=[pl.BlockSpec((SEQ, WIDTH), lambda b: (b, 0)), kv_spec, kv_spec],
        out_shape=[jax.ShapeDtypeStruct((N_CTX, WIDTH), BF16), kv_shape, kv_shape],
        input_output_aliases={3: 1, 4: 2},
        compiler_params=_cparams(1),
        name="ctx_attn",
    )(z, z, z, k_all, v_all)


ROPE_AXIS = HEAD_DIM // 2
ROPE_PAIR = ROPE_AXIS // 2


def _rope(x, cos, sin_signed, first_of_pair):
    partner = jnp.where(first_of_pair, pltpu.roll(x, HEAD_DIM - ROPE_PAIR, 1), pltpu.roll(x, ROPE_PAIR, 1))
    return x * cos + partner * sin_signed


NB_GRID_ROWS = DEC_SEQ // GRID_W
NB_WIN_ROWS = min(NB_ROWS_MAX, NB_GRID_ROWS)
NB_Q_ROWS = 4


def _nb_row_start(qr):
    return min(max(qr - NB_WIN_ROWS // 2, 0), NB_GRID_ROWS - NB_WIN_ROWS)


def _nb_tiles():
    tiles = []
    for q0 in range(0, NB_GRID_ROWS, NB_Q_ROWS):
        starts = [_nb_row_start(qr) for qr in range(q0, q0 + NB_Q_ROWS)]
        lo, hi = min(starts), max(starts) + NB_WIN_ROWS
        tiles.append((q0, lo - lo % 2, hi + hi % 2))
    return tiles


def _nbhd_attn_kernel(q_ref, k_ref, v_ref, kc_ref, vc_ref, pair_ref, cos_ref, sin_ref, o_ref, bias_scr):
    scale = HEAD_DIM ** -0.5
    lane = lax.broadcasted_iota(jnp.int32, (1, HEAD_DIM), 1)
    first_quarter = (lane % ROPE_AXIS) < ROPE_PAIR

    @pl.when(pl.program_id(1) == 0)
    def _():
        left = lane < GRID_W
        neg = jnp.full((GRID_W, 2 * GRID_W), NEG_INF, F32)
        for q0, lo, hi in _nb_tiles():
            for qr in range(q0, q0 + NB_Q_ROWS):
                r0 = _nb_row_start(qr)
                for p in range(lo // 2, hi // 2):
                    ok0 = r0 <= 2 * p < r0 + NB_WIN_ROWS
                    ok1 = r0 <= 2 * p + 1 < r0 + NB_WIN_ROWS
                    tile = neg
                    if ok0 or ok1:
                        tile = pair_ref[2 * p - qr + NB_ROWS_MAX]
                        if not ok0:
                            tile = jnp.where(left, NEG_INF, tile)
                        if not ok1:
                            tile = jnp.where(left, tile, NEG_INF)
                    bias_scr[qr * GRID_W:(qr + 1) * GRID_W, p * 2 * GRID_W:(p + 1) * 2 * GRID_W] = tile

    k = _rope(k_ref[...], cos_ref[...], sin_ref[...], first_quarter).astype(BF16)
    v = v_ref[...].astype(BF16)
    kc = kc_ref[...].astype(BF16)
    vc = vc_ref[...].astype(BF16)
    for q0, lo, hi in _nb_tiles():
        rs = slice(q0 * GRID_W, (q0 + NB_Q_ROWS) * GRID_W)
        ks = slice(lo * GRID_W, hi * GRID_W)
        q = _rope(q_ref[rs, :], cos_ref[rs, :], sin_ref[rs, :], first_quarter).astype(BF16)
        s_loc = _qkt(q, k[ks]) * scale + bias_scr[rs, ks]
        s_ctx = _qkt(q, kc) * scale
        o_ref[rs, :] = _softmax_pv([s_loc, s_ctx], [v[ks], vc]).astype(BF16)


def nbhd_attention(z, cache_k, cache_v, pair_table, cos, sin_signed, l):
    lat0 = N_CTX // DEC_SEQ
    col = lambda c: (lambda h, b: (lat0 + b, c * HEADS + h))
    tok = lambda c: pl.BlockSpec((DEC_SEQ, HEAD_DIM), col(c))
    cache = pl.BlockSpec((None, None, PAST_LEN, HEAD_DIM), lambda h, b: (b, l, 0, h))
    table = pl.BlockSpec((DEC_SEQ, HEAD_DIM), lambda h, b: (0, 0))
    pairs = pl.BlockSpec((None, 2 * NB_ROWS_MAX, GRID_W, 2 * GRID_W), lambda h, b: (h, 0, 0, 0))
    return pl.pallas_call(
        _nbhd_attn_kernel,
        grid=(HEADS, DEC_BATCH),
        in_specs=[tok(COL_BQ), tok(COL_BK), tok(COL_BV), cache, cache, pairs, table, table],
        out_specs=pl.BlockSpec((DEC_SEQ, HEAD_DIM), lambda h, b: (b, h)),
        out_shape=jax.ShapeDtypeStruct((N_LAT, WIDTH), BF16),
        scratch_shapes=[pltpu.VMEM((DEC_SEQ, DEC_SEQ), F32)],
        compiler_params=_cparams(2),
        name="nbhd_attn",
    )(z, z, z, cache_k, cache_v, pair_table, cos, sin_signed)


def nbhd_pair_tables(rpb):
    cols = jnp.arange(GRID_W)
    c_start = jnp.clip(cols - NB_COLS // 2, 0, GRID_W - NB_COLS)
    col_ok = (cols[None, :] >= c_start[:, None]) & (cols[None, :] < c_start[:, None] + NB_COLS)
    dc_idx = jnp.clip(cols[None, :] - cols[:, None] + NB_COLS - 1, 0, 2 * NB_COLS - 2)
    onehot = (dc_idx[None] == jnp.arange(2 * NB_COLS - 1)[:, None, None]).astype(F32)
    t = jnp.einsum('lhrd,dqk->lhrqk', rpb.astype(F32), onehot, precision=lax.Precision.HIGHEST)
    t = jnp.where(col_ok, t, NEG_INF)
    neg = jnp.full((DEPTH, HEADS, 1, GRID_W, GRID_W), NEG_INF, F32)
    tp = jnp.concatenate([neg, t, neg], 2)
    return jnp.concatenate([tp[:, :, :-1], tp[:, :, 1:]], -1)


def rope_tables():
    ax = HEAD_DIM // 2
    pos = jnp.arange(DEC_SEQ)
    inv = ROPE_THETA ** (-jnp.arange(0, ax, 2, dtype=F32) / ax)

    def ang(p):
        a = p.astype(F32)[:, None] * inv[None, :]
        return jnp.concatenate([a, a], -1)

    a = jnp.concatenate([ang(pos // GRID_W), ang(pos % GRID_W)], -1)
    sign = jnp.where((jnp.arange(HEAD_DIM) % ax) < ax // 2, -1.0, 1.0).astype(F32)
    return jnp.cos(a), jnp.sin(a) * sign[None, :]


def _split3(x):
    hi = x.astype(BF16)
    r = x - hi.astype(F32)
    mid = r.astype(BF16)
    lo = (r - mid.astype(F32)).astype(BF16)
    return hi, mid, lo


def _hgrn_chunks(chains, st, tri_ref, pairs_ref, side_ref):
    c = HGRN_CHUNK
    st = list(st)
    qs, kk, ff, bb, vb = [], [], [], [], []
    for q, zf, v, (log2lb, log21mlb, half1mlb), d, b_scr, _ in chains:
        z2 = zf * LOG2E
        ls = jnp.minimum(z2, 0.0) - jnp.log2(1.0 + jnp.exp2(-jnp.abs(z2)))
        x2 = log21mlb + ls
        lf = jnp.maximum(log2lb, x2) + jnp.log2(1.0 + jnp.exp2(-jnp.abs(log2lb - x2)))
        k = half1mlb * (1.0 - jnp.tanh(0.5 * zf))
        b = sum(jnp.dot(tri_ref[d], p, preferred_element_type=F32) for p in _split3(lf))
        b_scr[...] = b
        qs.append(_silu(q))
        kk.append(k)
        ff.append(1.0 - k)
        bb.append(b)
        vb.append(v.astype(BF16))

    outs = []
    for i, (_, _, _, _, d, b_scr, s) in enumerate(chains):
        end = 0 if d else c - 1
        b_end = b_scr[end:end + 1, :]
        k_dec = (kk[i] * jnp.exp2(b_end - bb[i])).astype(BF16)
        st_add = lax.dot_general(vb[i], k_dec, (((0,), (0,)), ((), ())), preferred_element_type=F32)
        if st[s] is None:
            outs.append(None)
            st[s] = st_add
        else:
            outs.append(_qkt((qs[i] * jnp.exp2(bb[i])).astype(BF16), st[s].astype(BF16)))
            st[s] = st[s] * jnp.exp2(b_end) + st_add

    attn = []
    for i, (_, _, _, _, d, _, _) in enumerate(chains):
        a = jnp.zeros((c, c), F32)
        kd = kk[i]
        for delta in range(HGRN_BAND):
            if delta:
                kd3 = kd.reshape(c // SUBLANES, SUBLANES, HEAD_DIM)
                kd = ff[i] * pltpu.roll(kd3, SUBLANES - 1 if d else 1, 1).reshape(c, HEAD_DIM)
            a = jnp.where(pairs_ref[d] == delta, (qs[i] * kd).sum(-1, keepdims=True), a)
        attn.append(a)

    qb = [x.astype(BF16) for x in qs]
    kb = [x.astype(BF16) for x in kk]
    for lv, m in enumerate(HGRN_LEVELS):
        for i, (_, _, _, _, d, b_scr, _) in enumerate(chains):
            seam = m // 2 if d else m // 2 - 1
            g = jnp.concatenate([jnp.broadcast_to(b_scr[j * m + seam:j * m + seam + 1, :], (m, HEAD_DIM))
                                 for j in range(c // m)], 0)
            e = jnp.exp2((bb[i] - g) * side_ref[d, lv]).astype(BF16)
            attn[i] = jnp.where(pairs_ref[d] == m, _qkt(qb[i] * e, kb[i] * e), attn[i])
    for i in range(len(chains)):
        o_intra = jnp.dot(attn[i].astype(BF16), vb[i], preferred_element_type=F32)
        outs[i] = o_intra if outs[i] is None else outs[i] + o_intra
    return outs, st


def _hgrn_kernel(q_ref, ff_ref, fb_ref, v_ref, g_ref, lbp_ref, tri_ref, pairs_ref, side_ref, *rest,
                 n, heads, latent):
    of_scr, ob_scr, b_scr = rest[-3:]
    if latent:
        s0_ref, y_ref = rest[:2]
    else:
        y_ref, s_ref = rest[1:3]
    c = HGRN_CHUNK
    nc = n // c
    lanes = [slice(h * HEAD_DIM, (h + 1) * HEAD_DIM) for h in range(heads)]

    def body(i, st):
        chains, dst = [], []
        for u in range(HGRN_CHUNKS_PER_TRIP):
            cf = i * HGRN_CHUNKS_PER_TRIP + u
            start = (lambda r: r) if isinstance(cf, int) else (lambda r: pl.multiple_of(r, c))
            rf = pl.ds(start(cf * c), c)
            rb = pl.ds(start((nc - 1 - cf) * c), c)
            for h, ls in enumerate(lanes):
                gate_f = [lbp_ref[r:r + 1, ls] for r in range(3)]
                gate_b = [lbp_ref[3 + r:4 + r, ls] for r in range(3)]
                slot = 2 * (u * heads + h)
                chains.append((q_ref[rf, ls], ff_ref[rf, ls], v_ref[rf, ls], gate_f, 0, b_scr.at[slot], 2 * h))
                chains.append((q_ref[rb, ls], fb_ref[rb, ls], v_ref[rb, ls], gate_b, 1, b_scr.at[slot + 1],
                               2 * h + 1))
                dst += [(of_scr, rf, ls), (ob_scr, rb, ls)]
        outs, st = _hgrn_chunks(chains, st, tri_ref, pairs_ref, side_ref)
        for (scr, rows, ls), o in zip(dst, outs):
            scr[rows, ls] = o
        return tuple(st)

    trips = nc // HGRN_CHUNKS_PER_TRIP
    if latent:
        st = lax.fori_loop(0, trips, body, tuple(s0_ref[d, h].T for h in range(heads) for d in range(2)))
    else:
        st = body(0, (None,) * (2 * heads))
        if trips > 1:
            st = lax.fori_loop(1, trips, body, st)
    if not latent:
        for h in range(heads):
            for d in range(2):
                s_ref[d, h] = st[2 * h + d].T

    def finish(i, carry):
        rs = pl.ds(pl.multiple_of(i * c, c), c)
        for ls in lanes:
            y_ref[rs, ls] = (_rms(of_scr[rs, ls] + ob_scr[rs, ls]) * _silu(g_ref[rs, ls])).astype(BF16)
        return carry

    lax.fori_loop(0, nc, finish, 0)


def _hgrn_constants():
    c = HGRN_CHUNK
    t = jnp.arange(c)[:, None]
    s = jnp.arange(c)[None, :]
    tri_f = (s <= t)
    pairs_f = jnp.where((t // HGRN_BAND == s // HGRN_BAND) & (s <= t), t - s, -1).astype(jnp.int32)
    for m in HGRN_LEVELS:
        h = m // 2
        pairs_f = jnp.where((t // h == s // h + 1) & ((t // h) % 2 == 1), m, pairs_f)
    tri = jnp.stack([tri_f, tri_f.T]).astype(BF16)
    pairs = jnp.stack([pairs_f, pairs_f.T])
    later_half = jnp.stack([(jnp.arange(c) % m) >= m // 2 for m in HGRN_LEVELS])
    side_f = jnp.broadcast_to(jnp.where(later_half, 1.0, -1.0)[:, :, None], (len(HGRN_LEVELS), c, HEAD_DIM))
    side = jnp.stack([side_f, -side_f]).astype(F32)
    return tri, pairs, side


def hgrn(z, lbp, l, s_all=None, s0=None):
    latent = s0 is not None
    n = DEC_SEQ if latent else SEQ
    bsz = DEC_BATCH if latent else BATCH
    blk0 = N_CTX // DEC_SEQ if latent else 0
    hb = HGRN_LAT_HEADS_PER_STEP if latent else HGRN_CTX_HEADS_PER_STEP
    wb = hb * HEAD_DIM
    tri, pairs, side = _hgrn_constants()
    tok = lambda c: pl.BlockSpec((n, wb), lambda b, h: (blk0 + b, c * (HEADS // hb) + h))
    raw = pl.BlockSpec(memory_space=pl.ANY)
    in_specs = [tok(COL_CQ), tok(COL_CFF), tok(COL_CFB), tok(COL_CI), tok(COL_CG),
                pl.BlockSpec((8, wb), lambda b, h: (0, h)),
                pl.BlockSpec((2, HGRN_CHUNK, HGRN_CHUNK), lambda b, h: (0, 0, 0)),
                pl.BlockSpec((2, HGRN_CHUNK, HGRN_CHUNK), lambda b, h: (0, 0, 0)),
                pl.BlockSpec(side.shape, lambda b, h: (0, 0, 0, 0))]
    args = [z, z, z, z, z, lbp, tri, pairs, side]
    y_spec = pl.BlockSpec((n, wb), lambda b, h: (b, h))
    y_shape = jax.ShapeDtypeStruct((bsz * n, WIDTH), BF16)
    state = pl.BlockSpec((None, None, 2, hb, HEAD_DIM, HEAD_DIM), lambda b, h: (b, l, 0, h, 0, 0))
    if latent:
        in_specs.append(state)
        args.append(s0)
        out_specs, out_shape, aliases = y_spec, y_shape, {}
    else:
        aliases = {len(args): 1}
        in_specs.append(raw)
        args.append(s_all)
        out_specs = [y_spec, state]
        out_shape = [y_shape, jax.ShapeDtypeStruct(s_all.shape, F32)]
    return pl.pallas_call(
        functools.partial(_hgrn_kernel, n=n, heads=hb, latent=latent),
        grid=(bsz, HEADS // hb),
        in_specs=in_specs, out_specs=out_specs, out_shape=out_shape,
        scratch_shapes=[pltpu.VMEM((n, wb), F32)] * 2
        + [pltpu.VMEM((2 * HGRN_CHUNKS_PER_TRIP * hb, HGRN_CHUNK, HEAD_DIM), F32)],
        input_output_aliases=aliases,
        compiler_params=_cparams(2),
        name="hgrn",
    )(*args)


def hgrn_gate_params(lb_logits):
    p = jax.nn.softmax(lb_logits.astype(F32), axis=1)
    cs = jnp.cumsum(p, axis=1)
    lb = cs - cs[:, :1]
    rows = [jnp.log(lb[0]) * LOG2E, jnp.log1p(-lb[0]) * LOG2E, 0.5 * (1.0 - lb[0]),
            jnp.log(lb[1]) * LOG2E, jnp.log1p(-lb[1]) * LOG2E, 0.5 * (1.0 - lb[1]),
            jnp.zeros_like(lb[0]), jnp.zeros_like(lb[0])]
    return jnp.stack(rows, axis=1)


def kernel(x_prompt, x_sample, cache_attn_k, cache_attn_v, state_hgrn, c, c_ctx, w_ada, b_ada, norm_g, w_in,
           a_spatial_w, a_spatial_b, nb_rpb, hgrn_lb_logits, w_up_a, w_up_b, w_up_c, w_out, w_ffn_in, w_ffn_out):
    x = (x_prompt.reshape(N_CTX, D_MODEL), x_sample.reshape(N_LAT, D_MODEL))
    cond = jnp.zeros((COND_ROWS, D_MODEL), F32).at[0].set(c_ctx).at[1:N_COND].set(c)
    mod = ada_all_layers(cond, w_ada, b_ada)
    mod = mod.reshape(DEPTH, COND_ROWS, 6, 1, D_MODEL).transpose(0, 2, 1, 3, 4)
    lbp = hgrn_gate_params(hgrn_lb_logits)
    cos, sin_signed = rope_tables()
    pair_tables = nbhd_pair_tables(nb_rpb)
    cache_k = cache_attn_k.reshape(DEC_BATCH, DEPTH, PAST_LEN, WIDTH)
    cache_v = cache_attn_v.reshape(DEC_BATCH, DEPTH, PAST_LEN, WIDTH)
    gain = lambda l, i: norm_g[l, i][None, :]
    w_out_b = w_out.astype(BF16)
    w_ffn_out_b = w_ffn_out.astype(BF16)
    w_up_a, w_up_b, w_up_c = (w.astype(BF16) for w in (w_up_a, w_up_b, w_up_c))

    k_all = jnp.zeros((BATCH, DEPTH, SEQ, WIDTH), F32)
    v_all = jnp.zeros((BATCH, DEPTH, SEQ, WIDTH), F32)
    s_all = jnp.zeros((BATCH, DEPTH, 2, HEADS, HEAD_DIM, HEAD_DIM), F32)
    h = prenorm(*x, gain(0, 0), mod[0, 1], mod[0, 0])
    for l in range(DEPTH):
        sh1, sc1, g1, sh2, sc2, g2 = (mod[l, i] for i in range(6))
        z = matmul_layer(h, w_in, l, tm=1024, tn=1024)
        y_a = mix_a(z, a_spatial_w[l], a_spatial_b[l])
        yb_ctx, k_all, v_all = ctx_attention(z, l, k_all, v_all)
        yb_lat = nbhd_attention(z, cache_k, cache_v, pair_tables[l], cos, sin_signed, l)
        yc_ctx, s_all = hgrn(z, lbp[l], l, s_all=s_all)
        yc_lat = hgrn(z, lbp[l], l, s0=state_hgrn)
        merged = merge_branches(y_a, (yb_ctx, yb_lat), (yc_ctx, yc_lat), z, w_up_a, w_up_b, w_up_c, l)
        x, h2 = proj_resnorm(merged, w_out_b, l, x, gain(l, 1), g1, nxt=(gain(l, 2), sc2, sh2))
        act = swiglu_in(h2, w_ffn_in, l)
        nxt = None if l == DEPTH - 1 else (gain(l + 1, 0), mod[l + 1, 1], mod[l + 1, 0])
        x, h = proj_resnorm(act, w_ffn_out_b, l, x, gain(l, 3), g2, nxt=nxt, tk=D_FF // 4,
                            split_out=l == DEPTH - 1)

    y_prompt = x[0].reshape(BATCH, SEQ, D_MODEL)
    y_sample = x[1].reshape(DEC_BATCH, DEC_SEQ, D_MODEL)
    kv_shape = (BATCH, DEPTH, SEQ, HEADS, HEAD_DIM)
    return (y_prompt, y_sample, k_all.reshape(kv_shape), v_all.reshape(kv_shape), s_all)
```

```python
import functools
import math

import jax
import jax.numpy as jnp
from jax import lax
from jax.experimental import pallas as pl
from jax.experimental.pallas import tpu as pltpu

F32 = jnp.float32
BF16 = jnp.bfloat16

D_MODEL = 2048
BATCH = 32
SEQ = 256
DEPTH = 4
DEC_BATCH = 2
DEC_SEQ = 1024
PAST_LEN = 512
GRID_W = 64
HEADS = 8
HEAD_DIM = 128
WIDTH = HEADS * HEAD_DIM
A_CHUNK = 128
NB_ROWS_MAX = 8
NB_COLS = 16
ROPE_THETA = 10000.0
D_FF = -(-8 * D_MODEL // (3 * 256)) * 256
NORM_EPS = 1e-6
NEG_INF = -1e30
IN_WIDTH = 10 * WIDTH + 3 * D_MODEL

N_CTX = BATCH * SEQ
N_LAT = DEC_BATCH * DEC_SEQ
N_TOK = N_CTX + N_LAT
N_COND = 1 + DEC_BATCH
COND_ROWS = 8

COL_AU, COL_AV, COL_BQ, COL_BK, COL_BV, COL_CQ, COL_CFF, COL_CFB, COL_CI, COL_CG = range(10)
COL_GATES = 10 * WIDTH

HGRN_CHUNK = 128
HGRN_BAND = 4
HGRN_LEVELS = (8, 16, 32, 64, 128)
SUBLANES = 8
HGRN_CHUNKS_PER_TRIP = 2
HGRN_CTX_HEADS_PER_STEP = 8
HGRN_LAT_HEADS_PER_STEP = 4

LOG2E = math.log2(math.e)

MXU_TILE = 256
EPILOGUE_PARTS = 4
V7X_VMEM_BYTES = 64 * 1024 * 1024
VMEM_LIMIT = V7X_VMEM_BYTES * 3 // 4
VMEM_LIMIT_BIG = V7X_VMEM_BYTES * 7 // 8


def _cparams(n_axes, vmem=VMEM_LIMIT):
    return pltpu.CompilerParams(dimension_semantics=("arbitrary",) * n_axes, vmem_limit_bytes=vmem)


def _cond_row(i, tm):
    n_ctx_tiles = N_CTX // tm
    return jnp.where(i < n_ctx_tiles, 0, 1 + (i - n_ctx_tiles) // (DEC_SEQ // tm))


def _sigmoid(x):
    return 0.5 * (1.0 + jnp.tanh(0.5 * x))


def _silu(x):
    hx = 0.5 * x
    return hx + hx * jnp.tanh(hx)


def _gelu_tanh(x):
    c = math.sqrt(2.0 / math.pi)
    return x * (0.5 * (1.0 + jnp.tanh(c * (x + 0.044715 * (x * x * x)))))


def _rms(x):
    return x * lax.rsqrt(jnp.mean(x * x, -1, keepdims=True) + NORM_EPS)


def _ada_kernel(c_ref, w_ref, b_ref, o_ref):
    s = _silu(c_ref[...]).astype(BF16)
    o_ref[...] = jnp.dot(s, w_ref[...].astype(BF16), preferred_element_type=F32) + b_ref[...]


def ada_all_layers(cond, w_ada, b_ada):
    tn = 1024
    n_out = 6 * D_MODEL
    return pl.pallas_call(
        _ada_kernel,
        grid=(DEPTH, n_out // tn),
        in_specs=[pl.BlockSpec((COND_ROWS, D_MODEL), lambda l, j: (0, 0)),
                  pl.BlockSpec((None, D_MODEL, tn), lambda l, j: (l, 0, j)),
                  pl.BlockSpec((None, 1, tn), lambda l, j: (l, 0, j))],
        out_specs=pl.BlockSpec((None, COND_ROWS, tn), lambda l, j: (l, 0, j)),
        out_shape=jax.ShapeDtypeStruct((DEPTH, COND_ROWS, n_out), F32),
        compiler_params=_cparams(2),
        name="ada",
    )(cond, w_ada, b_ada.reshape(DEPTH, 1, n_out))


def _path_specs(tm, width=D_MODEL):
    n_ctx_tiles = N_CTX // tm
    ctx = pl.BlockSpec((tm, width), lambda i, *_: (jnp.minimum(i, n_ctx_tiles - 1), 0))
    lat = pl.BlockSpec((tm, width), lambda i, *_: (jnp.maximum(i - n_ctx_tiles, 0), 0))
    return ctx, lat


def _prenorm_kernel(xc_ref, xl_ref, g_ref, sc_ref, sh_ref, h_ref, *, n_ctx_tiles):
    x = jnp.where(pl.program_id(0) < n_ctx_tiles, xc_ref[...], xl_ref[...])
    h_ref[...] = (_rms(x) * (g_ref[...] * (1.0 + sc_ref[...])) + sh_ref[...]).astype(BF16)


def prenorm(x_ctx, x_lat, gain, sc, sh):
    tm = 512
    row = lambda i: (_cond_row(i, tm), 0, 0)
    ctx, lat = _path_specs(tm)
    return pl.pallas_call(
        functools.partial(_prenorm_kernel, n_ctx_tiles=N_CTX // tm),
        grid=(N_TOK // tm,),
        in_specs=[ctx, lat,
                  pl.BlockSpec((1, D_MODEL), lambda i: (0, 0)),
                  pl.BlockSpec((None, 1, D_MODEL), row),
                  pl.BlockSpec((None, 1, D_MODEL), row)],
        out_specs=pl.BlockSpec((tm, D_MODEL), lambda i: (i, 0)),
        out_shape=jax.ShapeDtypeStruct((N_TOK, D_MODEL), BF16),
        compiler_params=_cparams(1),
        name="prenorm",
    )(x_ctx, x_lat, gain, sc, sh)


def _proj_resnorm_kernel(a_ref, w_ref, *rest, nk, n_ctx_tiles, split_in, split_out, with_next):
    rest = list(rest)
    x_refs = [rest.pop(0) for _ in range(1 + split_in)]
    g_ref, gate_ref = rest.pop(0), rest.pop(0)
    if with_next:
        g2_ref, sc_ref, sh_ref = rest.pop(0), rest.pop(0), rest.pop(0)
    xo_refs = [rest.pop(0) for _ in range(1 + split_out)]
    if with_next:
        h_ref = rest.pop(0)
    is_ctx = pl.program_id(0) < n_ctx_tiles

    def part():
        return jnp.dot(a_ref[...], w_ref[...], preferred_element_type=F32)

    def finish(t, rows=slice(None)):
        x = jnp.where(is_ctx, x_refs[0][rows, :], x_refs[1][rows, :]) if split_in else x_refs[0][rows, :]
        x_new = x + _rms(t) * (gate_ref[...] * g_ref[...])
        if split_out:
            @pl.when(is_ctx)
            def _():
                xo_refs[0][rows, :] = x_new

            @pl.when(jnp.logical_not(is_ctx))
            def _():
                xo_refs[1][rows, :] = x_new
        else:
            xo_refs[0][rows, :] = x_new
        if with_next:
            h_ref[rows, :] = (_rms(x_new) * (g2_ref[...] * (1.0 + sc_ref[...])) + sh_ref[...]).astype(BF16)

    def last_step(acc_ref):
        parts = EPILOGUE_PARTS if nk <= 2 else 1
        part_rows = a_ref.shape[0] // parts
        for p in range(parts):
            r = slice(p * part_rows, (p + 1) * part_rows)
            t = jnp.dot(a_ref[r, :], w_ref[...], preferred_element_type=F32)
            finish(t if acc_ref is None else acc_ref[r, :] + t, r)

    if nk == 1:
        last_step(None)
        return
    acc_ref = rest[-1]
    k = pl.program_id(1)

    @pl.when(k == 0)
    def _():
        acc_ref[...] = part()

    @pl.when((k > 0) & (k < nk - 1))
    def _():
        acc_ref[...] += part()

    @pl.when(k == nk - 1)
    def _():
        last_step(acc_ref)


def proj_resnorm(a, w, l, x, gain, gate, nxt=None, tk=None, split_out=False):
    tm = 512
    tk = tk or a.shape[1]
    nk = a.shape[1] // tk
    split_in = isinstance(x, tuple)
    row = lambda i, k: (_cond_row(i, tm), 0, 0)
    tile = pl.BlockSpec((tm, D_MODEL), lambda i, k: (i, 0))
    vec = pl.BlockSpec((1, D_MODEL), lambda i, k: (0, 0))
    mod = pl.BlockSpec((None, 1, D_MODEL), row)
    w_mode = dict(pipeline_mode=pl.Buffered(1)) if nk == 1 else {}
    in_specs = [pl.BlockSpec((tm, tk), lambda i, k: (i, k)),
                pl.BlockSpec((None, tk, D_MODEL), lambda i, k: (l, k, 0), **w_mode)]
    in_specs += list(_path_specs(tm)) if split_in else [tile]
    in_specs += [vec, mod]
    args = [a, w, *(x if split_in else (x,)), gain, gate]
    if nxt is not None:
        in_specs += [vec, mod, mod]
        args += list(nxt)
    if split_out:
        out_specs = list(_path_specs(tm))
        out_shape = [jax.ShapeDtypeStruct((N_CTX, D_MODEL), F32), jax.ShapeDtypeStruct((N_LAT, D_MODEL), F32)]
    else:
        out_specs = [tile]
        out_shape = [jax.ShapeDtypeStruct((N_TOK, D_MODEL), F32)]
    if nxt is not None:
        out_specs.append(tile)
        out_shape.append(jax.ShapeDtypeStruct((N_TOK, D_MODEL), BF16))
    out = pl.pallas_call(
        functools.partial(_proj_resnorm_kernel, nk=nk, n_ctx_tiles=N_CTX // tm, split_in=split_in,
                          split_out=split_out, with_next=nxt is not None),
        grid=(N_TOK // tm, nk),
        in_specs=in_specs, out_specs=out_specs, out_shape=out_shape,
        scratch_shapes=[pltpu.VMEM((tm, D_MODEL), F32)] if nk > 1 else [],
        compiler_params=_cparams(2, VMEM_LIMIT_BIG),
        name="proj_resnorm",
    )(*args)
    n_x = 1 + split_out
    x_new = tuple(out[:n_x]) if split_out else out[0]
    return x_new, (out[n_x] if nxt is not None else None)


def _mm_kernel(a_ref, w_ref, o_ref, wb_ref):
    @pl.when(pl.program_id(1) == 0)
    def _():
        wb_ref[...] = w_ref[...].astype(BF16)

    o_ref[...] = jnp.dot(a_ref[...], wb_ref[...], preferred_element_type=F32)


def matmul_layer(a, w, l, tm, tn, vmem=VMEM_LIMIT):
    m, k = a.shape
    n = w.shape[2]
    return pl.pallas_call(
        _mm_kernel,
        grid=(n // tn, m // tm),
        in_specs=[pl.BlockSpec((tm, k), lambda j, i: (i, 0)),
                  pl.BlockSpec((None, k, tn), lambda j, i: (l, 0, j))],
        out_specs=pl.BlockSpec((tm, tn), lambda j, i: (i, j)),
        out_shape=jax.ShapeDtypeStruct((m, n), F32),
        scratch_shapes=[pltpu.VMEM((k, tn), BF16)],
        compiler_params=_cparams(2, vmem),
        name="matmul",
    )(a, w)


def _merge_kernel(ya_ref, ybc_ref, ybl_ref, ycc_ref, ycl_ref, wa_ref, wb_ref, wc_ref, ga_ref, gb_ref, gc_ref,
                  o_ref, *, n_ctx_tiles):
    def run(yb_ref, yc_ref):
        acc = _sigmoid(ga_ref[...]) * jnp.dot(ya_ref[...], wa_ref[...], preferred_element_type=F32)
        acc += _sigmoid(gb_ref[...]) * jnp.dot(yb_ref[...], wb_ref[...], preferred_element_type=F32)
        acc += _sigmoid(gc_ref[...]) * jnp.dot(yc_ref[...], wc_ref[...], preferred_element_type=F32)
        o_ref[...] = acc.astype(BF16)

    pl.when(pl.program_id(0) < n_ctx_tiles)(lambda: run(ybc_ref, ycc_ref))
    pl.when(pl.program_id(0) >= n_ctx_tiles)(lambda: run(ybl_ref, ycl_ref))


def merge_branches(y_a, y_b, y_c, z, w_up_a, w_up_b, w_up_c, l):
    tm, tn = 512, 2048
    gate0 = COL_GATES // tn
    per_gate = D_MODEL // tn
    y_spec = pl.BlockSpec((tm, WIDTH), lambda i, j: (i, 0))
    ctx_spec, lat_spec = _path_specs(tm, WIDTH)
    w_mode = dict(pipeline_mode=pl.Buffered(1)) if tn == D_MODEL else {}
    w_spec = pl.BlockSpec((None, WIDTH, tn), lambda i, j: (l, 0, j), **w_mode)
    g_spec = lambda b: pl.BlockSpec((tm, tn), lambda i, j: (i, gate0 + b * per_gate + j))
    return pl.pallas_call(
        functools.partial(_merge_kernel, n_ctx_tiles=N_CTX // tm),
        grid=(N_TOK // tm, D_MODEL // tn),
        in_specs=[y_spec, ctx_spec, lat_spec, ctx_spec, lat_spec, w_spec, w_spec, w_spec,
                  g_spec(0), g_spec(1), g_spec(2)],
        out_specs=pl.BlockSpec((tm, tn), lambda i, j: (i, j)),
        out_shape=jax.ShapeDtypeStruct((N_TOK, D_MODEL), BF16),
        compiler_params=_cparams(2, VMEM_LIMIT_BIG),
        name="merge",
    )(y_a, y_b[0], y_b[1], y_c[0], y_c[1], w_up_a, w_up_b, w_up_c, z, z, z)


def _swiglu_kernel(h_ref, wu_ref, wg_ref, o_ref, su_ref, sg_ref):
    @pl.when(pl.program_id(1) == 0)
    def _():
        su_ref[...] = wu_ref[...].astype(BF16)
        sg_ref[...] = wg_ref[...].astype(BF16)

    h = h_ref[...]
    for c in range(su_ref.shape[1] // MXU_TILE):
        cs = slice(c * MXU_TILE, (c + 1) * MXU_TILE)
        up = jnp.dot(h, su_ref[:, cs], preferred_element_type=F32)
        gate = jnp.dot(h, sg_ref[:, cs], preferred_element_type=F32)
        o_ref[:, cs] = (_silu(gate) * up).astype(BF16)


def swiglu_in(h, w_ffn_in, l):
    tm, tn = 2048, 512
    n_up = D_FF // tn
    return pl.pallas_call(
        _swiglu_kernel,
        grid=(n_up, N_TOK // tm),
        in_specs=[pl.BlockSpec((tm, D_MODEL), lambda j, i: (i, 0)),
                  pl.BlockSpec((None, D_MODEL, tn), lambda j, i: (l, 0, j)),
                  pl.BlockSpec((None, D_MODEL, tn), lambda j, i: (l, 0, n_up + j))],
        out_specs=pl.BlockSpec((tm, tn), lambda j, i: (i, j)),
        out_shape=jax.ShapeDtypeStruct((N_TOK, D_FF), BF16),
        scratch_shapes=[pltpu.VMEM((D_MODEL, tn), BF16)] * 2,
        compiler_params=_cparams(2, VMEM_LIMIT_BIG),
        name="swiglu_in",
    )(h, w_ffn_in, w_ffn_in)


def _mix_a_kernel(u_ref, v_ref, w_ref, bt_ref, o_ref, *, rows):
    w = w_ref[...].astype(BF16)
    for c in range(rows // A_CHUNK):
        rs = slice(c * A_CHUNK, (c + 1) * A_CHUNK)
        u = _gelu_tanh(u_ref[rs, :])
        v = _gelu_tanh(v_ref[rs, :])
        mu = jnp.mean(v, -1, keepdims=True)
        vc = v - mu
        vn = (vc * lax.rsqrt(jnp.mean(vc * vc, -1, keepdims=True) + NORM_EPS)).astype(BF16)
        for g in range(HEADS):
            cs = slice(g * HEAD_DIM, (g + 1) * HEAD_DIM)
            zg = jnp.dot(w[g], vn[:, cs], preferred_element_type=F32) + bt_ref[:, g:g + 1]
            o_ref[rs, cs] = (u[:, cs] * zg).astype(BF16)


def mix_a(z, a_w_l, a_b_l):
    rows = 512
    return pl.pallas_call(
        functools.partial(_mix_a_kernel, rows=rows),
        grid=(N_TOK // rows,),
        in_specs=[pl.BlockSpec((rows, WIDTH), lambda i: (i, COL_AU)),
                  pl.BlockSpec((rows, WIDTH), lambda i: (i, COL_AV)),
                  pl.BlockSpec((HEADS, A_CHUNK, A_CHUNK), lambda i: (0, 0, 0)),
                  pl.BlockSpec((A_CHUNK, HEADS), lambda i: (0, 0))],
        out_specs=pl.BlockSpec((rows, WIDTH), lambda i: (i, 0)),
        out_shape=jax.ShapeDtypeStruct((N_TOK, WIDTH), BF16),
        compiler_params=_cparams(1),
        name="mix_a",
    )(z, z, a_w_l, a_b_l.T)


def _softmax_pv(scores, values):
    m = scores[0].max(-1, keepdims=True)
    for s in scores[1:]:
        m = jnp.maximum(m, s.max(-1, keepdims=True))
    den = 0.0
    acc = 0.0
    for s, v in zip(scores, values):
        p = jnp.exp(s - m)
        den = den + p.sum(-1, keepdims=True)
        acc = acc + jnp.dot(p.astype(BF16), v, preferred_element_type=F32)
    return acc / den


def _qkt(q, k):
    return lax.dot_general(q, k, (((1,), (1,)), ((), ())), preferred_element_type=F32)


def _ctx_attn_kernel(q_ref, k_ref, v_ref, *rest):
    o_ref, ko_ref, vo_ref = rest[-3:]
    scale = HEAD_DIM ** -0.5
    ko_ref[...] = k_ref[...]
    vo_ref[...] = v_ref[...]
    for h in range(HEADS):
        cs = slice(h * HEAD_DIM, (h + 1) * HEAD_DIM)
        q = q_ref[:, cs].astype(BF16)
        k = k_ref[:, cs].astype(BF16)
        v = v_ref[:, cs].astype(BF16)
        o_ref[:, cs] = _softmax_pv([_qkt(q, k) * scale], [v]).astype(BF16)


def ctx_attention(z, l, k_all, v_all):
    spec = lambda col: pl.BlockSpec((SEQ, WIDTH), lambda b: (b, col))
    kv_spec = pl.BlockSpec((None, None, SEQ, WIDTH), lambda b: (b, l, 0, 0))
    kv_shape = jax.ShapeDtypeStruct((BATCH, DEPTH, SEQ, WIDTH), F32)
    raw = pl.BlockSpec(memory_space=pl.ANY)
    return pl.pallas_call(
        _ctx_attn_kernel,
        grid=(BATCH,),
        in_specs=[spec(COL_BQ), spec(COL_BK), spec(COL_BV), raw, raw],
        out_specs=[pl.BlockSpec((SEQ, WIDTH), lambda b: (b, 0)), kv_spec, kv_spec],
        out_shape=[jax.ShapeDtypeStruct((N_CTX, WIDTH), BF16), kv_shape, kv_shape],
        input_output_aliases={3: 1, 4: 2},
        compiler_params=_cparams(1),
        name="ctx_attn",
    )(z, z, z, k_all, v_all)


ROPE_AXIS = HEAD_DIM // 2
ROPE_PAIR = ROPE_AXIS // 2


def _rope(x, cos, sin_signed, first_of_pair):
    partner = jnp.where(first_of_pair, pltpu.roll(x, HEAD_DIM - ROPE_PAIR, 1), pltpu.roll(x, ROPE_PAIR, 1))
    return x * cos + partner * sin_signed


NB_GRID_ROWS = DEC_SEQ // GRID_W
NB_WIN_ROWS = min(NB_ROWS_MAX, NB_GRID_ROWS)
NB_Q_ROWS = 4


def _nb_row_start(qr):
    return min(max(qr - NB_WIN_ROWS // 2, 0), NB_GRID_ROWS - NB_WIN_ROWS)


def _nb_tiles():
    tiles = []
    for q0 in range(0, NB_GRID_ROWS, NB_Q_ROWS):
        starts = [_nb_row_start(qr) for qr in range(q0, q0 + NB_Q_ROWS)]
        lo, hi = min(starts), max(starts) + NB_WIN_ROWS
        tiles.append((q0, lo - lo % 2, hi + hi % 2))
    return tiles


def _nbhd_attn_kernel(q_ref, k_ref, v_ref, kc_ref, vc_ref, pair_ref, cos_ref, sin_ref, o_ref, bias_scr):
    scale = HEAD_DIM ** -0.5
    lane = lax.broadcasted_iota(jnp.int32, (1, HEAD_DIM), 1)
    first_quarter = (lane % ROPE_AXIS) < ROPE_PAIR

    @pl.when(pl.program_id(1) == 0)
    def _():
        left = lane < GRID_W
        neg = jnp.full((GRID_W, 2 * GRID_W), NEG_INF, F32)
        for q0, lo, hi in _nb_tiles():
            for qr in range(q0, q0 + NB_Q_ROWS):
                r0 = _nb_row_start(qr)
                for p in range(lo // 2, hi // 2):
                    ok0 = r0 <= 2 * p < r0 + NB_WIN_ROWS
                    ok1 = r0 <= 2 * p + 1 < r0 + NB_WIN_ROWS
                    tile = neg
                    if ok0 or ok1:
                        tile = pair_ref[2 * p - qr + NB_ROWS_MAX]
                        if not ok0:
                            tile = jnp.where(left, NEG_INF, tile)
                        if not ok1:
                            tile = jnp.where(left, tile, NEG_INF)
                    bias_scr[qr * GRID_W:(qr + 1) * GRID_W, p * 2 * GRID_W:(p + 1) * 2 * GRID_W] = tile

    k = _rope(k_ref[...], cos_ref[...], sin_ref[...], first_quarter).astype(BF16)
    v = v_ref[...].astype(BF16)
    kc = kc_ref[...].astype(BF16)
    vc = vc_ref[...].astype(BF16)
    for q0, lo, hi in _nb_tiles():
        rs = slice(q0 * GRID_W, (q0 + NB_Q_ROWS) * GRID_W)
        ks = slice(lo * GRID_W, hi * GRID_W)
        q = _rope(q_ref[rs, :], cos_ref[rs, :], sin_ref[rs, :], first_quarter).astype(BF16)
        s_loc = _qkt(q, k[ks]) * scale + bias_scr[rs, ks]
        s_ctx = _qkt(q, kc) * scale
        o_ref[rs, :] = _softmax_pv([s_loc, s_ctx], [v[ks], vc]).astype(BF16)


def nbhd_attention(z, cache_k, cache_v, pair_table, cos, sin_signed, l):
    lat0 = N_CTX // DEC_SEQ
    col = lambda c: (lambda h, b: (lat0 + b, c * HEADS + h))
    tok = lambda c: pl.BlockSpec((DEC_SEQ, HEAD_DIM), col(c))
    cache = pl.BlockSpec((None, None, PAST_LEN, HEAD_DIM), lambda h, b: (b, l, 0, h))
    table = pl.BlockSpec((DEC_SEQ, HEAD_DIM), lambda h, b: (0, 0))
    pairs = pl.BlockSpec((None, 2 * NB_ROWS_MAX, GRID_W, 2 * GRID_W), lambda h, b: (h, 0, 0, 0))
    return pl.pallas_call(
        _nbhd_attn_kernel,
        grid=(HEADS, DEC_BATCH),
        in_specs=[tok(COL_BQ), tok(COL_BK), tok(COL_BV), cache, cache, pairs, table, table],
        out_specs=pl.BlockSpec((DEC_SEQ, HEAD_DIM), lambda h, b: (b, h)),
        out_shape=jax.ShapeDtypeStruct((N_LAT, WIDTH), BF16),
        scratch_shapes=[pltpu.VMEM((DEC_SEQ, DEC_SEQ), F32)],
        compiler_params=_cparams(2),
        name="nbhd_attn",
    )(z, z, z, cache_k, cache_v, pair_table, cos, sin_signed)


def nbhd_pair_tables(rpb):
    cols = jnp.arange(GRID_W)
    c_start = jnp.clip(cols - NB_COLS // 2, 0, GRID_W - NB_COLS)
    col_ok = (cols[None, :] >= c_start[:, None]) & (cols[None, :] < c_start[:, None] + NB_COLS)
    dc_idx = jnp.clip(cols[None, :] - cols[:, None] + NB_COLS - 1, 0, 2 * NB_COLS - 2)
    onehot = (dc_idx[None] == jnp.arange(2 * NB_COLS - 1)[:, None, None]).astype(F32)
    t = jnp.einsum('lhrd,dqk->lhrqk', rpb.astype(F32), onehot, precision=lax.Precision.HIGHEST)
    t = jnp.where(col_ok, t, NEG_INF)
    neg = jnp.full((DEPTH, HEADS, 1, GRID_W, GRID_W), NEG_INF, F32)
    tp = jnp.concatenate([neg, t, neg], 2)
    return jnp.concatenate([tp[:, :, :-1], tp[:, :, 1:]], -1)


def rope_tables():
    ax = HEAD_DIM // 2
    pos = jnp.arange(DEC_SEQ)
    inv = ROPE_THETA ** (-jnp.arange(0, ax, 2, dtype=F32) / ax)

    def ang(p):
        a = p.astype(F32)[:, None] * inv[None, :]
        return jnp.concatenate([a, a], -1)

    a = jnp.concatenate([ang(pos // GRID_W), ang(pos % GRID_W)], -1)
    sign = jnp.where((jnp.arange(HEAD_DIM) % ax) < ax // 2, -1.0, 1.0).astype(F32)
    return jnp.cos(a), jnp.sin(a) * sign[None, :]


def _split3(x):
    hi = x.astype(BF16)
    r = x - hi.astype(F32)
    mid = r.astype(BF16)
    lo = (r - mid.astype(F32)).astype(BF16)
    return hi, mid, lo


def _hgrn_chunks(chains, st, tri_ref, pairs_ref, side_ref):
    c = HGRN_CHUNK
    st = list(st)
    qs, kk, ff, bb, vb = [], [], [], [], []
    for q, zf, v, (log2lb, log21mlb, half1mlb), d, b_scr, _ in chains:
        z2 = zf * LOG2E
        ls = jnp.minimum(z2, 0.0) - jnp.log2(1.0 + jnp.exp2(-jnp.abs(z2)))
        x2 = log21mlb + ls
        lf = jnp.maximum(log2lb, x2) + jnp.log2(1.0 + jnp.exp2(-jnp.abs(log2lb - x2)))
        k = half1mlb * (1.0 - jnp.tanh(0.5 * zf))
        b = sum(jnp.dot(tri_ref[d], p, preferred_element_type=F32) for p in _split3(lf))
        b_scr[...] = b
        qs.append(_silu(q))
        kk.append(k)
        ff.append(1.0 - k)
        bb.append(b)
        vb.append(v.astype(BF16))

    outs = []
    for i, (_, _, _, _, d, b_scr, s) in enumerate(chains):
        end = 0 if d else c - 1
        b_end = b_scr[end:end + 1, :]
        k_dec = (kk[i] * jnp.exp2(b_end - bb[i])).astype(BF16)
        st_add = lax.dot_general(vb[i], k_dec, (((0,), (0,)), ((), ())), preferred_element_type=F32)
        if st[s] is None:
            outs.append(None)
            st[s] = st_add
        else:
            outs.append(_qkt((qs[i] * jnp.exp2(bb[i])).astype(BF16), st[s].astype(BF16)))
            st[s] = st[s] * jnp.exp2(b_end) + st_add

    attn = []
    for i, (_, _, _, _, d, _, _) in enumerate(chains):
        a = jnp.zeros((c, c), F32)
        kd = kk[i]
        for delta in range(HGRN_BAND):
            if delta:
                kd3 = kd.reshape(c // SUBLANES, SUBLANES, HEAD_DIM)
                kd = ff[i] * pltpu.roll(kd3, SUBLANES - 1 if d else 1, 1).reshape(c, HEAD_DIM)
            a = jnp.where(pairs_ref[d] == delta, (qs[i] * kd).sum(-1, keepdims=True), a)
        attn.append(a)

    qb = [x.astype(BF16) for x in qs]
    kb = [x.astype(BF16) for x in kk]
    for lv, m in enumerate(HGRN_LEVELS):
        for i, (_, _, _, _, d, b_scr, _) in enumerate(chains):
            seam = m // 2 if d else m // 2 - 1
            g = jnp.concatenate([jnp.broadcast_to(b_scr[j * m + seam:j * m + seam + 1, :], (m, HEAD_DIM))
                                 for j in range(c // m)], 0)
            e = jnp.exp2((bb[i] - g) * side_ref[d, lv]).astype(BF16)
            attn[i] = jnp.where(pairs_ref[d] == m, _qkt(qb[i] * e, kb[i] * e), attn[i])
    for i in range(len(chains)):
        o_intra = jnp.dot(attn[i].astype(BF16), vb[i], preferred_element_type=F32)
        outs[i] = o_intra if outs[i] is None else outs[i] + o_intra
    return outs, st


def _hgrn_kernel(q_ref, ff_ref, fb_ref, v_ref, g_ref, lbp_ref, tri_ref, pairs_ref, side_ref, *rest,
                 n, heads, latent):
    of_scr, ob_scr, b_scr = rest[-3:]
    if latent:
        s0_ref, y_ref = rest[:2]
    else:
        y_ref, s_ref = rest[1:3]
    c = HGRN_CHUNK
    nc = n // c
    lanes = [slice(h * HEAD_DIM, (h + 1) * HEAD_DIM) for h in range(heads)]

    def body(i, st):
        chains, dst = [], []
        for u in range(HGRN_CHUNKS_PER_TRIP):
            cf = i * HGRN_CHUNKS_PER_TRIP + u
            start = (lambda r: r) if isinstance(cf, int) else (lambda r: pl.multiple_of(r, c))
            rf = pl.ds(start(cf * c), c)
            rb = pl.ds(start((nc - 1 - cf) * c), c)
            for h, ls in enumerate(lanes):
                gate_f = [lbp_ref[r:r + 1, ls] for r in range(3)]
                gate_b = [lbp_ref[3 + r:4 + r, ls] for r in range(3)]
                slot = 2 * (u * heads + h)
                chains.append((q_ref[rf, ls], ff_ref[rf, ls], v_ref[rf, ls], gate_f, 0, b_scr.at[slot], 2 * h))
                chains.append((q_ref[rb, ls], fb_ref[rb, ls], v_ref[rb, ls], gate_b, 1, b_scr.at[slot + 1],
                               2 * h + 1))
                dst += [(of_scr, rf, ls), (ob_scr, rb, ls)]
        outs, st = _hgrn_chunks(chains, st, tri_ref, pairs_ref, side_ref)
        for (scr, rows, ls), o in zip(dst, outs):
            scr[rows, ls] = o
        return tuple(st)

    trips = nc // HGRN_CHUNKS_PER_TRIP
    if latent:
        st = lax.fori_loop(0, trips, body, tuple(s0_ref[d, h].T for h in range(heads) for d in range(2)))
    else:
        st = body(0, (None,) * (2 * heads))
        if trips > 1:
            st = lax.fori_loop(1, trips, body, st)
    if not latent:
        for h in range(heads):
            for d in range(2):
                s_ref[d, h] = st[2 * h + d].T

    def finish(i, carry):
        rs = pl.ds(pl.multiple_of(i * c, c), c)
        for ls in lanes:
            y_ref[rs, ls] = (_rms(of_scr[rs, ls] + ob_scr[rs, ls]) * _silu(g_ref[rs, ls])).astype(BF16)
        return carry

    lax.fori_loop(0, nc, finish, 0)


def _hgrn_constants():
    c = HGRN_CHUNK
    t = jnp.arange(c)[:, None]
    s = jnp.arange(c)[None, :]
    tri_f = (s <= t)
    pairs_f = jnp.where((t // HGRN_BAND == s // HGRN_BAND) & (s <= t), t - s, -1).astype(jnp.int32)
    for m in HGRN_LEVELS:
        h = m // 2
        pairs_f = jnp.where((t // h == s // h + 1) & ((t // h) % 2 == 1), m, pairs_f)
    tri = jnp.stack([tri_f, tri_f.T]).astype(BF16)
    pairs = jnp.stack([pairs_f, pairs_f.T])
    later_half = jnp.stack([(jnp.arange(c) % m) >= m // 2 for m in HGRN_LEVELS])
    side_f = jnp.broadcast_to(jnp.where(later_half, 1.0, -1.0)[:, :, None], (len(HGRN_LEVELS), c, HEAD_DIM))
    side = jnp.stack([side_f, -side_f]).astype(F32)
    return tri, pairs, side


def hgrn(z, lbp, l, s_all=None, s0=None):
    latent = s0 is not None
    n = DEC_SEQ if latent else SEQ
    bsz = DEC_BATCH if latent else BATCH
    blk0 = N_CTX // DEC_SEQ if latent else 0
    hb = HGRN_LAT_HEADS_PER_STEP if latent else HGRN_CTX_HEADS_PER_STEP
    wb = hb * HEAD_DIM
    tri, pairs, side = _hgrn_constants()
    tok = lambda c: pl.BlockSpec((n, wb), lambda b, h: (blk0 + b, c * (HEADS // hb) + h))
    raw = pl.BlockSpec(memory_space=pl.ANY)
    in_specs = [tok(COL_CQ), tok(COL_CFF), tok(COL_CFB), tok(COL_CI), tok(COL_CG),
                pl.BlockSpec((8, wb), lambda b, h: (0, h)),
                pl.BlockSpec((2, HGRN_CHUNK, HGRN_CHUNK), lambda b, h: (0, 0, 0)),
                pl.BlockSpec((2, HGRN_CHUNK, HGRN_CHUNK), lambda b, h: (0, 0, 0)),
                pl.BlockSpec(side.shape, lambda b, h: (0, 0, 0, 0))]
    args = [z, z, z, z, z, lbp, tri, pairs, side]
    y_spec = pl.BlockSpec((n, wb), lambda b, h: (b, h))
    y_shape = jax.ShapeDtypeStruct((bsz * n, WIDTH), BF16)
    state = pl.BlockSpec((None, None, 2, hb, HEAD_DIM, HEAD_DIM), lambda b, h: (b, l, 0, h, 0, 0))
    if latent:
        in_specs.append(state)
        args.append(s0)
        out_specs, out_shape, aliases = y_spec, y_shape, {}
    else:
        aliases = {len(args): 1}
        in_specs.append(raw)
        args.append(s_all)
        out_specs = [y_spec, state]
        out_shape = [y_shape, jax.ShapeDtypeStruct(s_all.shape, F32)]
    return pl.pallas_call(
        functools.partial(_hgrn_kernel, n=n, heads=hb, latent=latent),
        grid=(bsz, HEADS // hb),
        in_specs=in_specs, out_specs=out_specs, out_shape=out_shape,
        scratch_shapes=[pltpu.VMEM((n, wb), F32)] * 2
        + [pltpu.VMEM((2 * HGRN_CHUNKS_PER_TRIP * hb, HGRN_CHUNK, HEAD_DIM), F32)],
        input_output_aliases=aliases,
        compiler_params=_cparams(2),
        name="hgrn",
    )(*args)


def hgrn_gate_params(lb_logits):
    p = jax.nn.softmax(lb_logits.astype(F32), axis=1)
    cs = jnp.cumsum(p, axis=1)
    lb = cs - cs[:, :1]
    rows = [jnp.log(lb[0]) * LOG2E, jnp.log1p(-lb[0]) * LOG2E, 0.5 * (1.0 - lb[0]),
            jnp.log(lb[1]) * LOG2E, jnp.log1p(-lb[1]) * LOG2E, 0.5 * (1.0 - lb[1]),
            jnp.zeros_like(lb[0]), jnp.zeros_like(lb[0])]
    return jnp.stack(rows, axis=1)


def kernel(x_prompt, x_sample, cache_attn_k, cache_attn_v, state_hgrn, c, c_ctx, w_ada, b_ada, norm_g, w_in,
           a_spatial_w, a_spatial_b, nb_rpb, hgrn_lb_logits, w_up_a, w_up_b, w_up_c, w_out, w_ffn_in, w_ffn_out):
    x = (x_prompt.reshape(N_CTX, D_MODEL), x_sample.reshape(N_LAT, D_MODEL))
    cond = jnp.zeros((COND_ROWS, D_MODEL), F32).at[0].set(c_ctx).at[1:N_COND].set(c)
    mod = ada_all_layers(cond, w_ada, b_ada)
    mod = mod.reshape(DEPTH, COND_ROWS, 6, 1, D_MODEL).transpose(0, 2, 1, 3, 4)
    lbp = hgrn_gate_params(hgrn_lb_logits)
    cos, sin_signed = rope_tables()
    pair_tables = nbhd_pair_tables(nb_rpb)
    cache_k = cache_attn_k.reshape(DEC_BATCH, DEPTH, PAST_LEN, WIDTH)
    cache_v = cache_attn_v.reshape(DEC_BATCH, DEPTH, PAST_LEN, WIDTH)
    gain = lambda l, i: norm_g[l, i][None, :]
    w_out_b = w_out.astype(BF16)
    w_ffn_out_b = w_ffn_out.astype(BF16)
    w_up_a, w_up_b, w_up_c = (w.astype(BF16) for w in (w_up_a, w_up_b, w_up_c))

    k_all = jnp.zeros((BATCH, DEPTH, SEQ, WIDTH), F32)
    v_all = jnp.zeros((BATCH, DEPTH, SEQ, WIDTH), F32)
    s_all = jnp.zeros((BATCH, DEPTH, 2, HEADS, HEAD_DIM, HEAD_DIM), F32)
    h = prenorm(*x, gain(0, 0), mod[0, 1], mod[0, 0])
    for l in range(DEPTH):
        sh1, sc1, g1, sh2, sc2, g2 = (mod[l, i] for i in range(6))
        z = matmul_layer(h, w_in, l, tm=1024, tn=1024)
        y_a = mix_a(z, a_spatial_w[l], a_spatial_b[l])
        yb_ctx, k_all, v_all = ctx_attention(z, l, k_all, v_all)
        yb_lat = nbhd_attention(z, cache_k, cache_v, pair_tables[l], cos, sin_signed, l)
        yc_ctx, s_all = hgrn(z, lbp[l], l, s_all=s_all)
        yc_lat = hgrn(z, lbp[l], l, s0=state_hgrn)
        merged = merge_branches(y_a, (yb_ctx, yb_lat), (yc_ctx, yc_lat), z, w_up_a, w_up_b, w_up_c, l)
        x, h2 = proj_resnorm(merged, w_out_b, l, x, gain(l, 1), g1, nxt=(gain(l, 2), sc2, sh2))
        act = swiglu_in(h2, w_ffn_in, l)
        nxt = None if l == DEPTH - 1 else (gain(l + 1, 0), mod[l + 1, 1], mod[l + 1, 0])
        last = l == DEPTH - 1
        x, h = proj_resnorm(act, w_ffn_out_b, l, x, gain(l, 3), g2, nxt=nxt, tk=D_FF // (4 if last else 2),
                            split_out=last)

    y_prompt = x[0].reshape(BATCH, SEQ, D_MODEL)
    y_sample = x[1].reshape(DEC_BATCH, DEC_SEQ, D_MODEL)
    kv_shape = (BATCH, DEPTH, SEQ, HEADS, HEAD_DIM)
    return (y_prompt, y_sample, k_all.reshape(kv_shape), v_all.reshape(kv_shape), s_all)
```

```python
import functools
import math

import jax
import jax.numpy as jnp
from jax import lax
from jax.experimental import pallas as pl
from jax.experimental.pallas import tpu as pltpu

F32 = jnp.float32
BF16 = jnp.bfloat16

D_MODEL = 2048
BATCH = 32
SEQ = 256
DEPTH = 4
DEC_BATCH = 2
DEC_SEQ = 1024
PAST_LEN = 512
GRID_W = 64
HEADS = 8
HEAD_DIM = 128
WIDTH = HEADS * HEAD_DIM
A_CHUNK = 128
NB_ROWS_MAX = 8
NB_COLS = 16
ROPE_THETA = 10000.0
D_FF = -(-8 * D_MODEL // (3 * 256)) * 256
NORM_EPS = 1e-6
NEG_INF = -1e30
IN_WIDTH = 10 * WIDTH + 3 * D_MODEL

N_CTX = BATCH * SEQ
N_LAT = DEC_BATCH * DEC_SEQ
N_TOK = N_CTX + N_LAT
N_COND = 1 + DEC_BATCH
COND_ROWS = 8

COL_AU, COL_AV, COL_BQ, COL_BK, COL_BV, COL_CQ, COL_CFF, COL_CFB, COL_CI, COL_CG = range(10)
COL_GATES = 10 * WIDTH

HGRN_CHUNK = 128
HGRN_BAND = 4
HGRN_LEVELS = (8, 16, 32, 64, 128)
SUBLANES = 8
HGRN_CHUNKS_PER_TRIP = 2
HGRN_CTX_HEADS_PER_STEP = 4
HGRN_LAT_HEADS_PER_STEP = 4

LOG2E = math.log2(math.e)

MXU_TILE = 256
EPILOGUE_PARTS = 4
V7X_VMEM_BYTES = 64 * 1024 * 1024
VMEM_LIMIT = V7X_VMEM_BYTES * 3 // 4
VMEM_LIMIT_BIG = V7X_VMEM_BYTES * 7 // 8


def _cparams(n_axes, vmem=VMEM_LIMIT):
    return pltpu.CompilerParams(dimension_semantics=("arbitrary",) * n_axes, vmem_limit_bytes=vmem)


def _cond_row(i, tm):
    n_ctx_tiles = N_CTX // tm
    return jnp.where(i < n_ctx_tiles, 0, 1 + (i - n_ctx_tiles) // (DEC_SEQ // tm))


def _sigmoid(x):
    return 0.5 * (1.0 + jnp.tanh(0.5 * x))


def _silu(x):
    hx = 0.5 * x
    return hx + hx * jnp.tanh(hx)


def _gelu_tanh(x):
    c = math.sqrt(2.0 / math.pi)
    return x * (0.5 * (1.0 + jnp.tanh(c * (x + 0.044715 * (x * x * x)))))


def _rms(x):
    return x * lax.rsqrt(jnp.mean(x * x, -1, keepdims=True) + NORM_EPS)


def _ada_kernel(c_ref, w_ref, b_ref, o_ref):
    s = _silu(c_ref[...]).astype(BF16)
    o_ref[...] = jnp.dot(s, w_ref[...].astype(BF16), preferred_element_type=F32) + b_ref[...]


def ada_all_layers(cond, w_ada, b_ada):
    tn = 1024
    n_out = 6 * D_MODEL
    return pl.pallas_call(
        _ada_kernel,
        grid=(DEPTH, n_out // tn),
        in_specs=[pl.BlockSpec((COND_ROWS, D_MODEL), lambda l, j: (0, 0)),
                  pl.BlockSpec((None, D_MODEL, tn), lambda l, j: (l, 0, j)),
                  pl.BlockSpec((None, 1, tn), lambda l, j: (l, 0, j))],
        out_specs=pl.BlockSpec((None, COND_ROWS, tn), lambda l, j: (l, 0, j)),
        out_shape=jax.ShapeDtypeStruct((DEPTH, COND_ROWS, n_out), F32),
        compiler_params=_cparams(2),
        name="ada",
    )(cond, w_ada, b_ada.reshape(DEPTH, 1, n_out))


def _path_specs(tm, width=D_MODEL):
    n_ctx_tiles = N_CTX // tm
    ctx = pl.BlockSpec((tm, width), lambda i, *_: (jnp.minimum(i, n_ctx_tiles - 1), 0))
    lat = pl.BlockSpec((tm, width), lambda i, *_: (jnp.maximum(i - n_ctx_tiles, 0), 0))
    return ctx, lat


def _prenorm_kernel(xc_ref, xl_ref, g_ref, sc_ref, sh_ref, h_ref, *, n_ctx_tiles):
    x = jnp.where(pl.program_id(0) < n_ctx_tiles, xc_ref[...], xl_ref[...])
    h_ref[...] = (_rms(x) * (g_ref[...] * (1.0 + sc_ref[...])) + sh_ref[...]).astype(BF16)


def prenorm(x_ctx, x_lat, gain, sc, sh):
    tm = 512
    row = lambda i: (_cond_row(i, tm), 0, 0)
    ctx, lat = _path_specs(tm)
    return pl.pallas_call(
        functools.partial(_prenorm_kernel, n_ctx_tiles=N_CTX // tm),
        grid=(N_TOK // tm,),
        in_specs=[ctx, lat,
                  pl.BlockSpec((1, D_MODEL), lambda i: (0, 0)),
                  pl.BlockSpec((None, 1, D_MODEL), row),
                  pl.BlockSpec((None, 1, D_MODEL), row)],
        out_specs=pl.BlockSpec((tm, D_MODEL), lambda i: (i, 0)),
        out_shape=jax.ShapeDtypeStruct((N_TOK, D_MODEL), BF16),
        compiler_params=_cparams(1),
        name="prenorm",
    )(x_ctx, x_lat, gain, sc, sh)


def _proj_resnorm_kernel(a_ref, w_ref, *rest, nk, n_ctx_tiles, split_in, split_out, with_next):
    rest = list(rest)
    x_refs = [rest.pop(0) for _ in range(1 + split_in)]
    g_ref, gate_ref = rest.pop(0), rest.pop(0)
    if with_next:
        g2_ref, sc_ref, sh_ref = rest.pop(0), rest.pop(0), rest.pop(0)
    xo_refs = [rest.pop(0) for _ in range(1 + split_out)]
    if with_next:
        h_ref = rest.pop(0)
    is_ctx = pl.program_id(0) < n_ctx_tiles

    def part():
        return jnp.dot(a_ref[...], w_ref[...], preferred_element_type=F32)

    def finish(t, rows=slice(None)):
        x = jnp.where(is_ctx, x_refs[0][rows, :], x_refs[1][rows, :]) if split_in else x_refs[0][rows, :]
        x_new = x + _rms(t) * (gate_ref[...] * g_ref[...])
        if split_out:
            @pl.when(is_ctx)
            def _():
                xo_refs[0][rows, :] = x_new

            @pl.when(jnp.logical_not(is_ctx))
            def _():
                xo_refs[1][rows, :] = x_new
        else:
            xo_refs[0][rows, :] = x_new
        if with_next:
            h_ref[rows, :] = (_rms(x_new) * (g2_ref[...] * (1.0 + sc_ref[...])) + sh_ref[...]).astype(BF16)

    def last_step(acc_ref):
        parts = EPILOGUE_PARTS if nk <= 2 else 1
        part_rows = a_ref.shape[0] // parts
        for p in range(parts):
            r = slice(p * part_rows, (p + 1) * part_rows)
            t = jnp.dot(a_ref[r, :], w_ref[...], preferred_element_type=F32)
            finish(t if acc_ref is None else acc_ref[r, :] + t, r)

    if nk == 1:
        last_step(None)
        return
    acc_ref = rest[-1]
    k = pl.program_id(1)

    @pl.when(k == 0)
    def _():
        acc_ref[...] = part()

    @pl.when((k > 0) & (k < nk - 1))
    def _():
        acc_ref[...] += part()

    @pl.when(k == nk - 1)
    def _():
        last_step(acc_ref)


def proj_resnorm(a, w, l, x, gain, gate, nxt=None, tk=None, split_out=False):
    tm = 512
    tk = tk or a.shape[1]
    nk = a.shape[1] // tk
    split_in = isinstance(x, tuple)
    row = lambda i, k: (_cond_row(i, tm), 0, 0)
    tile = pl.BlockSpec((tm, D_MODEL), lambda i, k: (i, 0))
    vec = pl.BlockSpec((1, D_MODEL), lambda i, k: (0, 0))
    mod = pl.BlockSpec((None, 1, D_MODEL), row)
    w_mode = dict(pipeline_mode=pl.Buffered(1)) if nk == 1 else {}
    in_specs = [pl.BlockSpec((tm, tk), lambda i, k: (i, k)),
                pl.BlockSpec((None, tk, D_MODEL), lambda i, k: (l, k, 0), **w_mode)]
    in_specs += list(_path_specs(tm)) if split_in else [tile]
    in_specs += [vec, mod]
    args = [a, w, *(x if split_in else (x,)), gain, gate]
    if nxt is not None:
        in_specs += [vec, mod, mod]
        args += list(nxt)
    if split_out:
        out_specs = list(_path_specs(tm))
        out_shape = [jax.ShapeDtypeStruct((N_CTX, D_MODEL), F32), jax.ShapeDtypeStruct((N_LAT, D_MODEL), F32)]
    else:
        out_specs = [tile]
        out_shape = [jax.ShapeDtypeStruct((N_TOK, D_MODEL), F32)]
    if nxt is not None:
        out_specs.append(tile)
        out_shape.append(jax.ShapeDtypeStruct((N_TOK, D_MODEL), BF16))
    out = pl.pallas_call(
        functools.partial(_proj_resnorm_kernel, nk=nk, n_ctx_tiles=N_CTX // tm, split_in=split_in,
                          split_out=split_out, with_next=nxt is not None),
        grid=(N_TOK // tm, nk),
        in_specs=in_specs, out_specs=out_specs, out_shape=out_shape,
        scratch_shapes=[pltpu.VMEM((tm, D_MODEL), F32)] if nk > 1 else [],
        compiler_params=_cparams(2, VMEM_LIMIT_BIG),
        name="proj_resnorm",
    )(*args)
    n_x = 1 + split_out
    x_new = tuple(out[:n_x]) if split_out else out[0]
    return x_new, (out[n_x] if nxt is not None else None)


def _mm_kernel(a_ref, w_ref, o_ref, wb_ref):
    @pl.when(pl.program_id(1) == 0)
    def _():
        wb_ref[...] = w_ref[...].astype(BF16)

    o_ref[...] = jnp.dot(a_ref[...], wb_ref[...], preferred_element_type=F32)


def matmul_layer(a, w, l, tm, tn, vmem=VMEM_LIMIT):
    m, k = a.shape
    n = w.shape[2]
    return pl.pallas_call(
        _mm_kernel,
        grid=(n // tn, m // tm),
        in_specs=[pl.BlockSpec((tm, k), lambda j, i: (i, 0)),
                  pl.BlockSpec((None, k, tn), lambda j, i: (l, 0, j))],
        out_specs=pl.BlockSpec((tm, tn), lambda j, i: (i, j)),
        out_shape=jax.ShapeDtypeStruct((m, n), F32),
        scratch_shapes=[pltpu.VMEM((k, tn), BF16)],
        compiler_params=_cparams(2, vmem),
        name="matmul",
    )(a, w)


def _merge_kernel(ya_ref, ybc_ref, ybl_ref, ycc_ref, ycl_ref, wa_ref, wb_ref, wc_ref, ga_ref, gb_ref, gc_ref,
                  o_ref, *, n_ctx_tiles):
    def run(yb_ref, yc_ref):
        acc = _sigmoid(ga_ref[...]) * jnp.dot(ya_ref[...], wa_ref[...], preferred_element_type=F32)
        acc += _sigmoid(gb_ref[...]) * jnp.dot(yb_ref[...], wb_ref[...], preferred_element_type=F32)
        acc += _sigmoid(gc_ref[...]) * jnp.dot(yc_ref[...], wc_ref[...], preferred_element_type=F32)
        o_ref[...] = acc.astype(BF16)

    pl.when(pl.program_id(0) < n_ctx_tiles)(lambda: run(ybc_ref, ycc_ref))
    pl.when(pl.program_id(0) >= n_ctx_tiles)(lambda: run(ybl_ref, ycl_ref))


def merge_branches(y_a, y_b, y_c, z, w_up_a, w_up_b, w_up_c, l):
    tm, tn = 512, 2048
    gate0 = COL_GATES // tn
    per_gate = D_MODEL // tn
    y_spec = pl.BlockSpec((tm, WIDTH), lambda i, j: (i, 0))
    ctx_spec, lat_spec = _path_specs(tm, WIDTH)
    w_mode = dict(pipeline_mode=pl.Buffered(1)) if tn == D_MODEL else {}
    w_spec = pl.BlockSpec((None, WIDTH, tn), lambda i, j: (l, 0, j), **w_mode)
    g_spec = lambda b: pl.BlockSpec((tm, tn), lambda i, j: (i, gate0 + b * per_gate + j))
    return pl.pallas_call(
        functools.partial(_merge_kernel, n_ctx_tiles=N_CTX // tm),
        grid=(N_TOK // tm, D_MODEL // tn),
        in_specs=[y_spec, ctx_spec, lat_spec, ctx_spec, lat_spec, w_spec, w_spec, w_spec,
                  g_spec(0), g_spec(1), g_spec(2)],
        out_specs=pl.BlockSpec((tm, tn), lambda i, j: (i, j)),
        out_shape=jax.ShapeDtypeStruct((N_TOK, D_MODEL), BF16),
        compiler_params=_cparams(2, VMEM_LIMIT_BIG),
        name="merge",
    )(y_a, y_b[0], y_b[1], y_c[0], y_c[1], w_up_a, w_up_b, w_up_c, z, z, z)


def _swiglu_kernel(h_ref, wu_ref, wg_ref, o_ref, su_ref, sg_ref):
    @pl.when(pl.program_id(1) == 0)
    def _():
        su_ref[...] = wu_ref[...].astype(BF16)
        sg_ref[...] = wg_ref[...].astype(BF16)

    h = h_ref[...]
    for c in range(su_ref.shape[1] // MXU_TILE):
        cs = slice(c * MXU_TILE, (c + 1) * MXU_TILE)
        up = jnp.dot(h, su_ref[:, cs], preferred_element_type=F32)
        gate = jnp.dot(h, sg_ref[:, cs], preferred_element_type=F32)
        o_ref[:, cs] = (_silu(gate) * up).astype(BF16)


def swiglu_in(h, w_ffn_in, l):
    tm, tn = 2048, 512
    n_up = D_FF // tn
    return pl.pallas_call(
        _swiglu_kernel,
        grid=(n_up, N_TOK // tm),
        in_specs=[pl.BlockSpec((tm, D_MODEL), lambda j, i: (i, 0)),
                  pl.BlockSpec((None, D_MODEL, tn), lambda j, i: (l, 0, j)),
                  pl.BlockSpec((None, D_MODEL, tn), lambda j, i: (l, 0, n_up + j))],
        out_specs=pl.BlockSpec((tm, tn), lambda j, i: (i, j)),
        out_shape=jax.ShapeDtypeStruct((N_TOK, D_FF), BF16),
        scratch_shapes=[pltpu.VMEM((D_MODEL, tn), BF16)] * 2,
        compiler_params=_cparams(2, VMEM_LIMIT_BIG),
        name="swiglu_in",
    )(h, w_ffn_in, w_ffn_in)


def _mix_a_kernel(u_ref, v_ref, w_ref, bt_ref, o_ref, *, rows):
    w = w_ref[...].astype(BF16)
    for c in range(rows // A_CHUNK):
        rs = slice(c * A_CHUNK, (c + 1) * A_CHUNK)
        u = _gelu_tanh(u_ref[rs, :])
        v = _gelu_tanh(v_ref[rs, :])
        mu = jnp.mean(v, -1, keepdims=True)
        vc = v - mu
        vn = (vc * lax.rsqrt(jnp.mean(vc * vc, -1, keepdims=True) + NORM_EPS)).astype(BF16)
        for g in range(HEADS):
            cs = slice(g * HEAD_DIM, (g + 1) * HEAD_DIM)
            zg = jnp.dot(w[g], vn[:, cs], preferred_element_type=F32) + bt_ref[:, g:g + 1]
            o_ref[rs, cs] = (u[:, cs] * zg).astype(BF16)


def mix_a(z, a_w_l, a_b_l):
    rows = 512
    return pl.pallas_call(
        functools.partial(_mix_a_kernel, rows=rows),
        grid=(N_TOK // rows,),
        in_specs=[pl.BlockSpec((rows, WIDTH), lambda i: (i, COL_AU)),
                  pl.BlockSpec((rows, WIDTH), lambda i: (i, COL_AV)),
                  pl.BlockSpec((HEADS, A_CHUNK, A_CHUNK), lambda i: (0, 0, 0)),
                  pl.BlockSpec((A_CHUNK, HEADS), lambda i: (0, 0))],
        out_specs=pl.BlockSpec((rows, WIDTH), lambda i: (i, 0)),
        out_shape=jax.ShapeDtypeStruct((N_TOK, WIDTH), BF16),
        compiler_params=_cparams(1),
        name="mix_a",
    )(z, z, a_w_l, a_b_l.T)


def _softmax_pv(scores, values):
    m = scores[0].max(-1, keepdims=True)
    for s in scores[1:]:
        m = jnp.maximum(m, s.max(-1, keepdims=True))
    den = 0.0
    acc = 0.0
    for s, v in zip(scores, values):
        p = jnp.exp(s - m)
        den = den + p.sum(-1, keepdims=True)
        acc = acc + jnp.dot(p.astype(BF16), v, preferred_element_type=F32)
    return acc / den


def _qkt(q, k):
    return lax.dot_general(q, k, (((1,), (1,)), ((), ())), preferred_element_type=F32)


CTX_RING = 3


def _ctx_attn_kernel(z_hbm, kin_hbm, vin_hbm, o_ref, ko_ref, vo_ref, buf, sem):
    del kin_hbm, vin_hbm
    b = pl.program_id(0)

    def copy(step, stream):
        col = (COL_BQ, COL_BK, COL_BV)[stream]
        rows = pl.ds(pl.multiple_of(step * SEQ, SEQ), SEQ)
        slot = step % CTX_RING
        return pltpu.make_async_copy(z_hbm.at[rows, pl.ds(col * WIDTH, WIDTH)], buf.at[slot, stream],
                                     sem.at[slot, stream])

    @pl.when(b == 0)
    def _():
        for step in range(CTX_RING - 1):
            for stream in range(3):
                copy(step, stream).start()

    @pl.when(b + CTX_RING - 1 < BATCH)
    def _():
        for stream in range(3):
            copy(b + CTX_RING - 1, stream).start()

    for stream in range(3):
        copy(b, stream).wait()
    slot = b % CTX_RING
    q_ref, k_ref, v_ref = buf.at[slot, 0], buf.at[slot, 1], buf.at[slot, 2]
    scale = HEAD_DIM ** -0.5
    ko_ref[...] = k_ref[...]
    vo_ref[...] = v_ref[...]
    for h in range(HEADS):
        cs = slice(h * HEAD_DIM, (h + 1) * HEAD_DIM)
        q = q_ref[:, cs].astype(BF16)
        k = k_ref[:, cs].astype(BF16)
        v = v_ref[:, cs].astype(BF16)
        o_ref[:, cs] = _softmax_pv([_qkt(q, k) * scale], [v]).astype(BF16)


def ctx_attention(z, l, k_all, v_all):
    kv_spec = pl.BlockSpec((None, None, SEQ, WIDTH), lambda b: (b, l, 0, 0))
    kv_shape = jax.ShapeDtypeStruct((BATCH, DEPTH, SEQ, WIDTH), F32)
    raw = pl.BlockSpec(memory_space=pl.ANY)
    return pl.pallas_call(
        _ctx_attn_kernel,
        grid=(BATCH,),
        in_specs=[raw, raw, raw],
        out_specs=[pl.BlockSpec((SEQ, WIDTH), lambda b: (b, 0)), kv_spec, kv_spec],
        out_shape=[jax.ShapeDtypeStruct((N_CTX, WIDTH), BF16), kv_shape, kv_shape],
        scratch_shapes=[pltpu.VMEM((CTX_RING, 3, SEQ, WIDTH), F32), pltpu.SemaphoreType.DMA((CTX_RING, 3))],
        input_output_aliases={1: 1, 2: 2},
        compiler_params=_cparams(1),
        name="ctx_attn",
    )(z, k_all, v_all)


ROPE_AXIS = HEAD_DIM // 2
ROPE_PAIR = ROPE_AXIS // 2


def _rope(x, cos, sin_signed, first_of_pair):
    partner = jnp.where(first_of_pair, pltpu.roll(x, HEAD_DIM - ROPE_PAIR, 1), pltpu.roll(x, ROPE_PAIR, 1))
    return x * cos + partner * sin_signed


NB_GRID_ROWS = DEC_SEQ // GRID_W
NB_WIN_ROWS = min(NB_ROWS_MAX, NB_GRID_ROWS)
NB_Q_ROWS = 4


def _nb_row_start(qr):
    return min(max(qr - NB_WIN_ROWS // 2, 0), NB_GRID_ROWS - NB_WIN_ROWS)


def _nb_tiles():
    tiles = []
    for q0 in range(0, NB_GRID_ROWS, NB_Q_ROWS):
        starts = [_nb_row_start(qr) for qr in range(q0, q0 + NB_Q_ROWS)]
        lo, hi = min(starts), max(starts) + NB_WIN_ROWS
        tiles.append((q0, lo - lo % 2, hi + hi % 2))
    return tiles


def _nbhd_attn_kernel(q_ref, k_ref, v_ref, kc_ref, vc_ref, pair_ref, cos_ref, sin_ref, o_ref, bias_scr):
    scale = HEAD_DIM ** -0.5
    lane = lax.broadcasted_iota(jnp.int32, (1, HEAD_DIM), 1)
    first_quarter = (lane % ROPE_AXIS) < ROPE_PAIR

    @pl.when(pl.program_id(1) == 0)
    def _():
        left = lane < GRID_W
        neg = jnp.full((GRID_W, 2 * GRID_W), NEG_INF, F32)
        for q0, lo, hi in _nb_tiles():
            for qr in range(q0, q0 + NB_Q_ROWS):
                r0 = _nb_row_start(qr)
                for p in range(lo // 2, hi // 2):
                    ok0 = r0 <= 2 * p < r0 + NB_WIN_ROWS
                    ok1 = r0 <= 2 * p + 1 < r0 + NB_WIN_ROWS
                    tile = neg
                    if ok0 or ok1:
                        tile = pair_ref[2 * p - qr + NB_ROWS_MAX]
                        if not ok0:
                            tile = jnp.where(left, NEG_INF, tile)
                        if not ok1:
                            tile = jnp.where(left, tile, NEG_INF)
                    bias_scr[qr * GRID_W:(qr + 1) * GRID_W, p * 2 * GRID_W:(p + 1) * 2 * GRID_W] = tile

    k = _rope(k_ref[...], cos_ref[...], sin_ref[...], first_quarter).astype(BF16)
    v = v_ref[...].astype(BF16)
    kc = kc_ref[...].astype(BF16)
    vc = vc_ref[...].astype(BF16)
    for q0, lo, hi in _nb_tiles():
        rs = slice(q0 * GRID_W, (q0 + NB_Q_ROWS) * GRID_W)
        ks = slice(lo * GRID_W, hi * GRID_W)
        q = _rope(q_ref[rs, :], cos_ref[rs, :], sin_ref[rs, :], first_quarter).astype(BF16)
        s_loc = _qkt(q, k[ks]) * scale + bias_scr[rs, ks]
        s_ctx = _qkt(q, kc) * scale
        o_ref[rs, :] = _softmax_pv([s_loc, s_ctx], [v[ks], vc]).astype(BF16)


def nbhd_attention(z, cache_k, cache_v, pair_table, cos, sin_signed, l):
    lat0 = N_CTX // DEC_SEQ
    col = lambda c: (lambda h, b: (lat0 + b, c * HEADS + h))
    tok = lambda c: pl.BlockSpec((DEC_SEQ, HEAD_DIM), col(c))
    cache = pl.BlockSpec((None, None, PAST_LEN, HEAD_DIM), lambda h, b: (b, l, 0, h))
    table = pl.BlockSpec((DEC_SEQ, HEAD_DIM), lambda h, b: (0, 0))
    pairs = pl.BlockSpec((None, 2 * NB_ROWS_MAX, GRID_W, 2 * GRID_W), lambda h, b: (h, 0, 0, 0))
    return pl.pallas_call(
        _nbhd_attn_kernel,
        grid=(HEADS, DEC_BATCH),
        in_specs=[tok(COL_BQ), tok(COL_BK), tok(COL_BV), cache, cache, pairs, table, table],
        out_specs=pl.BlockSpec((DEC_SEQ, HEAD_DIM), lambda h, b: (b, h)),
        out_shape=jax.ShapeDtypeStruct((N_LAT, WIDTH), BF16),
        scratch_shapes=[pltpu.VMEM((DEC_SEQ, DEC_SEQ), F32)],
        compiler_params=_cparams(2),
        name="nbhd_attn",
    )(z, z, z, cache_k, cache_v, pair_table, cos, sin_signed)


def nbhd_pair_tables(rpb):
    cols = jnp.arange(GRID_W)
    c_start = jnp.clip(cols - NB_COLS // 2, 0, GRID_W - NB_COLS)
    col_ok = (cols[None, :] >= c_start[:, None]) & (cols[None, :] < c_start[:, None] + NB_COLS)
    dc_idx = jnp.clip(cols[None, :] - cols[:, None] + NB_COLS - 1, 0, 2 * NB_COLS - 2)
    onehot = (dc_idx[None] == jnp.arange(2 * NB_COLS - 1)[:, None, None]).astype(F32)
    t = jnp.einsum('lhrd,dqk->lhrqk', rpb.astype(F32), onehot, precision=lax.Precision.HIGHEST)
    t = jnp.where(col_ok, t, NEG_INF)
    neg = jnp.full((DEPTH, HEADS, 1, GRID_W, GRID_W), NEG_INF, F32)
    tp = jnp.concatenate([neg, t, neg], 2)
    return jnp.concatenate([tp[:, :, :-1], tp[:, :, 1:]], -1)


def rope_tables():
    ax = HEAD_DIM // 2
    pos = jnp.arange(DEC_SEQ)
    inv = ROPE_THETA ** (-jnp.arange(0, ax, 2, dtype=F32) / ax)

    def ang(p):
        a = p.astype(F32)[:, None] * inv[None, :]
        return jnp.concatenate([a, a], -1)

    a = jnp.concatenate([ang(pos // GRID_W), ang(pos % GRID_W)], -1)
    sign = jnp.where((jnp.arange(HEAD_DIM) % ax) < ax // 2, -1.0, 1.0).astype(F32)
    return jnp.cos(a), jnp.sin(a) * sign[None, :]


def _split3(x):
    hi = x.astype(BF16)
    r = x - hi.astype(F32)
    mid = r.astype(BF16)
    lo = (r - mid.astype(F32)).astype(BF16)
    return hi, mid, lo


def _hgrn_chunks(chains, st, tri_ref, pairs_ref, side_ref):
    c = HGRN_CHUNK
    st = list(st)
    qs, kk, ff, bb, vb = [], [], [], [], []
    for q, zf, v, (log2lb, log21mlb, half1mlb), d, b_scr, _ in chains:
        z2 = zf * LOG2E
        ls = jnp.minimum(z2, 0.0) - jnp.log2(1.0 + jnp.exp2(-jnp.abs(z2)))
        x2 = log21mlb + ls
        lf = jnp.maximum(log2lb, x2) + jnp.log2(1.0 + jnp.exp2(-jnp.abs(log2lb - x2)))
        k = half1mlb * (1.0 - jnp.tanh(0.5 * zf))
        b = sum(jnp.dot(tri_ref[d], p, preferred_element_type=F32) for p in _split3(lf))
        b_scr[...] = b
        qs.append(_silu(q))
        kk.append(k)
        ff.append(1.0 - k)
        bb.append(b)
        vb.append(v.astype(BF16))

    outs = []
    for i, (_, _, _, _, d, b_scr, s) in enumerate(chains):
        end = 0 if d else c - 1
        b_end = b_scr[end:end + 1, :]
        k_dec = (kk[i] * jnp.exp2(b_end - bb[i])).astype(BF16)
        st_add = lax.dot_general(vb[i], k_dec, (((0,), (0,)), ((), ())), preferred_element_type=F32)
        if st[s] is None:
            outs.append(None)
            st[s] = st_add
        else:
            outs.append(_qkt((qs[i] * jnp.exp2(bb[i])).astype(BF16), st[s].astype(BF16)))
            st[s] = st[s] * jnp.exp2(b_end) + st_add

    attn = []
    for i, (_, _, _, _, d, _, _) in enumerate(chains):
        a = jnp.zeros((c, c), F32)
        kd = kk[i]
        for delta in range(HGRN_BAND):
            if delta:
                kd3 = kd.reshape(c // SUBLANES, SUBLANES, HEAD_DIM)
                kd = ff[i] * pltpu.roll(kd3, SUBLANES - 1 if d else 1, 1).reshape(c, HEAD_DIM)
            a = jnp.where(pairs_ref[d] == delta, (qs[i] * kd).sum(-1, keepdims=True), a)
        attn.append(a)

    qb = [x.astype(BF16) for x in qs]
    kb = [x.astype(BF16) for x in kk]
    for lv, m in enumerate(HGRN_LEVELS):
        for i, (_, _, _, _, d, b_scr, _) in enumerate(chains):
            seam = m // 2 if d else m // 2 - 1
            g = jnp.concatenate([jnp.broadcast_to(b_scr[j * m + seam:j * m + seam + 1, :], (m, HEAD_DIM))
                                 for j in range(c // m)], 0)
            e = jnp.exp2((bb[i] - g) * side_ref[d, lv]).astype(BF16)
            attn[i] = jnp.where(pairs_ref[d] == m, _qkt(qb[i] * e, kb[i] * e), attn[i])
    for i in range(len(chains)):
        o_intra = jnp.dot(attn[i].astype(BF16), vb[i], preferred_element_type=F32)
        outs[i] = o_intra if outs[i] is None else outs[i] + o_intra
    return outs, st


def _hgrn_kernel(q_ref, ff_ref, fb_ref, v_ref, g_ref, lbp_ref, tri_ref, pairs_ref, side_ref, *rest,
                 n, heads, latent):
    of_scr, ob_scr, b_scr = rest[-3:]
    if latent:
        s0_ref, y_ref = rest[:2]
    else:
        y_ref, s_ref = rest[1:3]
    c = HGRN_CHUNK
    nc = n // c
    lanes = [slice(h * HEAD_DIM, (h + 1) * HEAD_DIM) for h in range(heads)]

    def body(i, st):
        chains, dst = [], []
        for u in range(HGRN_CHUNKS_PER_TRIP):
            cf = i * HGRN_CHUNKS_PER_TRIP + u
            start = (lambda r: r) if isinstance(cf, int) else (lambda r: pl.multiple_of(r, c))
            rf = pl.ds(start(cf * c), c)
            rb = pl.ds(start((nc - 1 - cf) * c), c)
            for h, ls in enumerate(lanes):
                gate_f = [lbp_ref[r:r + 1, ls] for r in range(3)]
                gate_b = [lbp_ref[3 + r:4 + r, ls] for r in range(3)]
                slot = 2 * (u * heads + h)
                chains.append((q_ref[rf, ls], ff_ref[rf, ls], v_ref[rf, ls], gate_f, 0, b_scr.at[slot], 2 * h))
                chains.append((q_ref[rb, ls], fb_ref[rb, ls], v_ref[rb, ls], gate_b, 1, b_scr.at[slot + 1],
                               2 * h + 1))
                dst += [(of_scr, rf, ls), (ob_scr, rb, ls)]
        outs, st = _hgrn_chunks(chains, st, tri_ref, pairs_ref, side_ref)
        for (scr, rows, ls), o in zip(dst, outs):
            scr[rows, ls] = o
        return tuple(st)

    trips = nc // HGRN_CHUNKS_PER_TRIP
    if latent:
        st = lax.fori_loop(0, trips, body, tuple(s0_ref[d, h].T for h in range(heads) for d in range(2)))
    else:
        st = body(0, (None,) * (2 * heads))
        if trips > 1:
            st = lax.fori_loop(1, trips, body, st)
    if not latent:
        for h in range(heads):
            for d in range(2):
                s_ref[d, h] = st[2 * h + d].T

    def finish(i, carry):
        rs = pl.ds(pl.multiple_of(i * c, c), c)
        for ls in lanes:
            y_ref[rs, ls] = (_rms(of_scr[rs, ls] + ob_scr[rs, ls]) * _silu(g_ref[rs, ls])).astype(BF16)
        return carry

    lax.fori_loop(0, nc, finish, 0)


def _hgrn_constants():
    c = HGRN_CHUNK
    t = jnp.arange(c)[:, None]
    s = jnp.arange(c)[None, :]
    tri_f = (s <= t)
    pairs_f = jnp.where((t // HGRN_BAND == s // HGRN_BAND) & (s <= t), t - s, -1).astype(jnp.int32)
    for m in HGRN_LEVELS:
        h = m // 2
        pairs_f = jnp.where((t // h == s // h + 1) & ((t // h) % 2 == 1), m, pairs_f)
    tri = jnp.stack([tri_f, tri_f.T]).astype(BF16)
    pairs = jnp.stack([pairs_f, pairs_f.T])
    later_half = jnp.stack([(jnp.arange(c) % m) >= m // 2 for m in HGRN_LEVELS])
    side_f = jnp.broadcast_to(jnp.where(later_half, 1.0, -1.0)[:, :, None], (len(HGRN_LEVELS), c, HEAD_DIM))
    side = jnp.stack([side_f, -side_f]).astype(F32)
    return tri, pairs, side


def hgrn(z, lbp, l, s_all=None, s0=None):
    latent = s0 is not None
    n = DEC_SEQ if latent else SEQ
    bsz = DEC_BATCH if latent else BATCH
    blk0 = N_CTX // DEC_SEQ if latent else 0
    hb = HGRN_LAT_HEADS_PER_STEP if latent else HGRN_CTX_HEADS_PER_STEP
    wb = hb * HEAD_DIM
    tri, pairs, side = _hgrn_constants()
    tok = lambda c: pl.BlockSpec((n, wb), lambda b, h: (blk0 + b, c * (HEADS // hb) + h))
    raw = pl.BlockSpec(memory_space=pl.ANY)
    in_specs = [tok(COL_CQ), tok(COL_CFF), tok(COL_CFB), tok(COL_CI), tok(COL_CG),
                pl.BlockSpec((8, wb), lambda b, h: (0, h)),
                pl.BlockSpec((2, HGRN_CHUNK, HGRN_CHUNK), lambda b, h: (0, 0, 0)),
                pl.BlockSpec((2, HGRN_CHUNK, HGRN_CHUNK), lambda b, h: (0, 0, 0)),
                pl.BlockSpec(side.shape, lambda b, h: (0, 0, 0, 0))]
    args = [z, z, z, z, z, lbp, tri, pairs, side]
    y_spec = pl.BlockSpec((n, wb), lambda b, h: (b, h))
    y_shape = jax.ShapeDtypeStruct((bsz * n, WIDTH), BF16)
    state = pl.BlockSpec((None, None, 2, hb, HEAD_DIM, HEAD_DIM), lambda b, h: (b, l, 0, h, 0, 0))
    if latent:
        in_specs.append(state)
        args.append(s0)
        out_specs, out_shape, aliases = y_spec, y_shape, {}
    else:
        aliases = {len(args): 1}
        in_specs.append(raw)
        args.append(s_all)
        out_specs = [y_spec, state]
        out_shape = [y_shape, jax.ShapeDtypeStruct(s_all.shape, F32)]
    return pl.pallas_call(
        functools.partial(_hgrn_kernel, n=n, heads=hb, latent=latent),
        grid=(bsz, HEADS // hb),
        in_specs=in_specs, out_specs=out_specs, out_shape=out_shape,
        scratch_shapes=[pltpu.VMEM((n, wb), F32)] * 2
        + [pltpu.VMEM((2 * HGRN_CHUNKS_PER_TRIP * hb, HGRN_CHUNK, HEAD_DIM), F32)],
        input_output_aliases=aliases,
        compiler_params=_cparams(2),
        name="hgrn",
    )(*args)


def hgrn_gate_params(lb_logits):
    p = jax.nn.softmax(lb_logits.astype(F32), axis=1)
    cs = jnp.cumsum(p, axis=1)
    lb = cs - cs[:, :1]
    rows = [jnp.log(lb[0]) * LOG2E, jnp.log1p(-lb[0]) * LOG2E, 0.5 * (1.0 - lb[0]),
            jnp.log(lb[1]) * LOG2E, jnp.log1p(-lb[1]) * LOG2E, 0.5 * (1.0 - lb[1]),
            jnp.zeros_like(lb[0]), jnp.zeros_like(lb[0])]
    return jnp.stack(rows, axis=1)


def kernel(x_prompt, x_sample, cache_attn_k, cache_attn_v, state_hgrn, c, c_ctx, w_ada, b_ada, norm_g, w_in,
           a_spatial_w, a_spatial_b, nb_rpb, hgrn_lb_logits, w_up_a, w_up_b, w_up_c, w_out, w_ffn_in, w_ffn_out):
    x = (x_prompt.reshape(N_CTX, D_MODEL), x_sample.reshape(N_LAT, D_MODEL))
    cond = jnp.zeros((COND_ROWS, D_MODEL), F32).at[0].set(c_ctx).at[1:N_COND].set(c)
    mod = ada_all_layers(cond, w_ada, b_ada)
    mod = mod.reshape(DEPTH, COND_ROWS, 6, 1, D_MODEL).transpose(0, 2, 1, 3, 4)
    lbp = hgrn_gate_params(hgrn_lb_logits)
    cos, sin_signed = rope_tables()
    pair_tables = nbhd_pair_tables(nb_rpb)
    cache_k = cache_attn_k.reshape(DEC_BATCH, DEPTH, PAST_LEN, WIDTH)
    cache_v = cache_attn_v.reshape(DEC_BATCH, DEPTH, PAST_LEN, WIDTH)
    gain = lambda l, i: norm_g[l, i][None, :]
    w_out_b = w_out.astype(BF16)
    w_ffn_out_b = w_ffn_out.astype(BF16)
    w_up_a, w_up_b, w_up_c = (w.astype(BF16) for w in (w_up_a, w_up_b, w_up_c))

    k_all = jnp.zeros((BATCH, DEPTH, SEQ, WIDTH), F32)
    v_all = jnp.zeros((BATCH, DEPTH, SEQ, WIDTH), F32)
    s_all = jnp.zeros((BATCH, DEPTH, 2, HEADS, HEAD_DIM, HEAD_DIM), F32)
    h = prenorm(*x, gain(0, 0), mod[0, 1], mod[0, 0])
    for l in range(DEPTH):
        sh1, sc1, g1, sh2, sc2, g2 = (mod[l, i] for i in range(6))
        z = matmul_layer(h, w_in, l, tm=1024, tn=1024)
        y_a = mix_a(z, a_spatial_w[l], a_spatial_b[l])
        yb_ctx, k_all, v_all = ctx_attention(z, l, k_all, v_all)
        yb_lat = nbhd_attention(z, cache_k, cache_v, pair_tables[l], cos, sin_signed, l)
        yc_ctx, s_all = hgrn(z, lbp[l], l, s_all=s_all)
        yc_lat = hgrn(z, lbp[l], l, s0=state_hgrn)
        merged = merge_branches(y_a, (yb_ctx, yb_lat), (yc_ctx, yc_lat), z, w_up_a, w_up_b, w_up_c, l)
        x, h2 = proj_resnorm(merged, w_out_b, l, x, gain(l, 1), g1, nxt=(gain(l, 2), sc2, sh2))
        act = swiglu_in(h2, w_ffn_in, l)
        nxt = None if l == DEPTH - 1 else (gain(l + 1, 0), mod[l + 1, 1], mod[l + 1, 0])
        last = l == DEPTH - 1
        x, h = proj_resnorm(act, w_ffn_out_b, l, x, gain(l, 3), g2, nxt=nxt, tk=D_FF // (4 if last else 2),
                            split_out=last)

    y_prompt = x[0].reshape(BATCH, SEQ, D_MODEL)
    y_sample = x[1].reshape(DEC_BATCH, DEC_SEQ, D_MODEL)
    kv_shape = (BATCH, DEPTH, SEQ, HEADS, HEAD_DIM)
    return (y_prompt, y_sample, k_all.reshape(kv_shape), v_all.reshape(kv_shape), s_all)
```
